```python
import math
import jax, jax.numpy as jnp
from jax import lax
import numpy as np

D_MODEL = 4096
BATCH = 1
SEQ = 8192
DEPTH = 2

HEAD_DIM = 128
N_HEADS = D_MODEL // HEAD_DIM
DILATED_PATTERNS = ((128, 1), (512, 4), (2048, 16))
N_GROUPS_A = len(DILATED_PATTERNS)
BAND_BLOCK = 128
MOBA_BLOCK = 256
MOBA_TOPK = 3
Q_CHUNK = 64
N_BUCKETS = 32
MAX_DISTANCE = 2048
D_FF = -(-8 * D_MODEL // (3 * 256)) * 256
N_A_LAYERS = DEPTH // 2
N_B_LAYERS = DEPTH - N_A_LAYERS
EPS = 1e-6
NEG = -1e30

kernel_name = "yoco_dilated_moba_hybrid"


def rms_norm(x, g):
    xf = x.astype(jnp.float32)
    y = xf * lax.rsqrt(jnp.mean(xf * xf, axis=-1, keepdims=True) + EPS)
    return (y * g.astype(jnp.float32)).astype(x.dtype)


def head_rms(t, g):
    t = t.astype(jnp.float32)
    return t * lax.rsqrt(jnp.mean(t * t, axis=-1, keepdims=True) + EPS) * g.astype(jnp.float32)


def rel_bucket(dist):
    n = jnp.maximum(dist, 0)
    exact = N_BUCKETS // 2
    nf = jnp.maximum(n, exact).astype(jnp.float32)
    large = exact + (jnp.log(nf / exact) / math.log(MAX_DISTANCE / exact)
                     * (N_BUCKETS - exact)).astype(jnp.int32)
    return jnp.where(n < exact, n, jnp.minimum(large, N_BUCKETS - 1))


def masked_softmax(logits, valid):
    logits = jnp.where(valid, logits, NEG)
    m = jnp.max(logits, axis=-1, keepdims=True)
    p = jnp.exp(logits - m)
    s = jnp.sum(p, axis=-1, keepdims=True)
    return p / s, (m + jnp.log(s))[..., 0]


def dilated_branch(q, k, v, rel_bias, window, dilation):
    B, S, H, dh = q.shape
    L = S // dilation
    nb = -(-L // BAND_BLOCK)
    Lp = nb * BAND_BLOCK
    w_sub = window // dilation

    def to_residue(t):
        t = t.reshape(B, L, dilation, H, dh)
        t = jnp.moveaxis(t, 2, 1).reshape(B * dilation, L, H, dh)
        t = jnp.pad(t, ((0, 0), (0, Lp - L), (0, 0), (0, 0)))
        return t.reshape(B * dilation, nb, BAND_BLOCK, H, dh)

    def with_prev(t):
        prev = jnp.pad(t[:, :-1], ((0, 0), (1, 0), (0, 0), (0, 0), (0, 0)))
        return jnp.concatenate([prev, t], axis=2)

    qb = to_residue(q)
    kk = with_prev(to_residue(k))
    vv = with_prev(to_residue(v))

    i = jnp.arange(BAND_BLOCK)[:, None]
    j = jnp.arange(2 * BAND_BLOCK)[None, :]
    dist = BAND_BLOCK + i - j
    band_ok = (dist >= 0) & (dist <= w_sub)
    first_ok = (jnp.arange(nb)[:, None] * BAND_BLOCK - BAND_BLOCK
                + jnp.arange(2 * BAND_BLOCK)[None, :]) >= 0
    valid = (band_ok[None] & first_ok[:, None, :])[None, :, None]
    bias = jnp.transpose(rel_bias.astype(jnp.float32)[rel_bucket(dist * dilation)], (2, 0, 1))

    logits = jnp.einsum('bnqhd,bnkhd->bnhqk', qb, kk) * (HEAD_DIM ** -0.5) + bias[None, None]
    p, lse = masked_softmax(logits, valid)
    out = jnp.einsum('bnhqk,bnkhd->bnqhd', p, vv)
    lse = jnp.moveaxis(lse, 2, 3)

    def from_residue(t):
        rest = t.shape[3:]
        t = t.reshape((B, dilation, Lp) + rest)[:, :, :L]
        t = jnp.moveaxis(t, 1, 2)
        return t.reshape((B, S) + rest)

    return from_residue(out), from_residue(lse)


def dilated_mixer(h, w_qkv, q_gain, k_gain, w_o, rel_bias):
    B, S, _ = h.shape
    proj = (h @ w_qkv).astype(jnp.float32)
    nq = N_GROUPS_A * D_MODEL
    q = proj[..., :nq].reshape(B, S, N_GROUPS_A, N_HEADS, HEAD_DIM)
    k = proj[..., nq:nq + D_MODEL].reshape(B, S, N_HEADS, HEAD_DIM)
    v = proj[..., nq + D_MODEL:].reshape(B, S, N_HEADS, HEAD_DIM)
    q = head_rms(q, q_gain[:, None, :])
    k = head_rms(k, k_gain)
    outs, lses = [], []
    for g, (window, dilation) in enumerate(DILATED_PATTERNS):
        o, l = dilated_branch(q[:, :, g], k, v, rel_bias, window, dilation)
        outs.append(o)
        lses.append(l)
    wts = jax.nn.softmax(jnp.stack(lses), axis=0)
    o = jnp.sum(wts[..., None] * jnp.stack(outs), axis=0).reshape(B, S, D_MODEL)
    return o.astype(h.dtype) @ w_o


def shared_kv(x, kv_norm, w_kv, k_gain):
    B, S, _ = x.shape
    nb = -(-S // MOBA_BLOCK)
    Sp = nb * MOBA_BLOCK
    kv = (rms_norm(x, kv_norm) @ w_kv).astype(jnp.float32)
    k = head_rms(kv[..., :D_MODEL].reshape(B, S, N_HEADS, HEAD_DIM), k_gain)
    v = kv[..., D_MODEL:].reshape(B, S, N_HEADS, HEAD_DIM)
    pad = ((0, 0), (0, Sp - S), (0, 0), (0, 0))
    kp = jnp.pad(k, pad).reshape(B, nb, MOBA_BLOCK, N_HEADS, HEAD_DIM)
    vp = jnp.pad(v, pad).reshape(B, nb, MOBA_BLOCK, N_HEADS, HEAD_DIM)
    kmean = jnp.mean(kp, axis=2)
    return kp, vp, kmean


def moba_attention(q, kp, vp, kmean, rel_bias):
    B, S, H, dh = q.shape
    nb = kp.shape[1]
    Sp = nb * MOBA_BLOCK
    n_sel = min(MOBA_TOPK, nb)
    scale = HEAD_DIM ** -0.5
    rb = rel_bias.astype(jnp.float32)
    t = jnp.arange(S)
    cur = t // MOBA_BLOCK

    gate = jnp.einsum('bshd,bnhd->bhsn', q, kmean)
    past = jnp.arange(nb)[None, :] < cur[:, None]
    gate = jnp.where(past, gate, NEG)
    _, sel = lax.top_k(gate, n_sel)
    sel = jnp.transpose(sel, (0, 2, 1, 3))

    qo = jnp.pad(q, ((0, 0), (0, Sp - S), (0, 0), (0, 0))).reshape(B, nb, MOBA_BLOCK, H, dh)
    i = jnp.arange(MOBA_BLOCK)[:, None]
    j = jnp.arange(MOBA_BLOCK)[None, :]
    bias_own = jnp.transpose(rb[rel_bucket(i - j)], (2, 0, 1))
    own_logits = jnp.einsum('bnqhd,bnkhd->bnhqk', qo, kp) * scale + bias_own[None, None]
    p_own, lse_own = masked_softmax(own_logits, j <= i)
    out_own = jnp.einsum('bnhqk,bnkhd->bnqhd', p_own, vp).reshape(B, Sp, H, dh)[:, :S]
    lse_own = jnp.moveaxis(lse_own, 2, 3).reshape(B, Sp, H)[:, :S]

    kbh = jnp.moveaxis(kp, 3, 1)
    vbh = jnp.moveaxis(vp, 3, 1)
    bias_tab = rb.T
    b_ix = jnp.arange(B)[:, None, None]
    h_ix = jnp.arange(H)[None, :, None]
    n_chunks = S // Q_CHUNK

    def to_chunks(a):
        return jnp.moveaxis(a.reshape((B, n_chunks, Q_CHUNK) + a.shape[2:]), 1, 0)

    def step(args):
        q_c, sel_c, t_c, lse_o, out_o = args
        cur_c = t_c // MOBA_BLOCK
        lses, outs = [lse_o], [out_o]
        for s in range(n_sel):
            blk = jnp.transpose(sel_c[..., s], (0, 2, 1))
            kg = kbh[b_ix, h_ix, blk]
            vg = vbh[b_ix, h_ix, blk]
            kpos = blk[..., None] * MOBA_BLOCK + jnp.arange(MOBA_BLOCK)
            bias = bias_tab[h_ix[..., None], rel_bucket(t_c[:, None] - kpos)]
            logits = jnp.einsum('bqhd,bhqkd->bhqk', q_c, kg) * scale + bias
            ok = (s < cur_c)[None, None, :, None]
            p, lse = masked_softmax(logits, ok)
            lse = jnp.where(ok[..., 0], lse, NEG)
            outs.append(jnp.einsum('bhqk,bhqkd->bqhd', p, vg))
            lses.append(jnp.transpose(lse, (0, 2, 1)))
        w = jax.nn.softmax(jnp.stack(lses), axis=0)
        return jnp.sum(w[..., None] * jnp.stack(outs), axis=0)

    res = lax.map(step, (to_chunks(q), to_chunks(sel), t.reshape(n_chunks, Q_CHUNK),
                         to_chunks(lse_own), to_chunks(out_own)))
    return jnp.moveaxis(res, 0, 1).reshape(B, S, H, dh)


def moba_mixer(h, w_q, q_gain, w_o, kp, vp, kmean, rel_bias):
    B, S, _ = h.shape
    q = head_rms((h @ w_q).astype(jnp.float32).reshape(B, S, N_HEADS, HEAD_DIM), q_gain)
    o = moba_attention(q, kp, vp, kmean, rel_bias).reshape(B, S, D_MODEL)
    return o.astype(h.dtype) @ w_o


def swiglu(h, w_gate_up, w_down):
    gu = h @ w_gate_up
    return (jax.nn.silu(gu[..., :D_FF]) * gu[..., D_FF:]) @ w_down


def setup_inputs(seed: int = 0) -> dict:
    key = jax.random.key(seed)
    ks = jax.random.split(key, 16)
    f32 = jnp.float32

    def nrm(k, shape, scale):
        return jax.random.normal(k, shape, f32) * scale

    def gain(k, shape):
        return 1.0 + 0.02 * jax.random.normal(k, shape, f32)

    d = D_MODEL
    return {
        "x": nrm(ks[0], (BATCH, SEQ, d), 1.0),
        "rel_bias": nrm(ks[1], (N_BUCKETS, N_HEADS), 0.1),
        "attn_norm": gain(ks[2], (DEPTH, d)),
        "ffn_norm": gain(ks[3], (DEPTH, d)),
        "w_qkv_a": nrm(ks[4], (N_A_LAYERS, d, (N_GROUPS_A + 2) * d), d ** -0.5),
        "q_norm_a": gain(ks[5], (N_A_LAYERS, N_GROUPS_A, HEAD_DIM)),
        "k_norm_a": gain(ks[6], (N_A_LAYERS, HEAD_DIM)),
        "w_o_a": nrm(ks[7], (N_A_LAYERS, d, d), d ** -0.5),
        "kv_norm": gain(ks[8], (d,)),
        "w_kv_b": nrm(ks[9], (d, 2 * d), d ** -0.5),
        "k_norm_b": gain(ks[10], (HEAD_DIM,)),
        "w_q_b": nrm(ks[11], (N_B_LAYERS, d, d), d ** -0.5),
        "q_norm_b": gain(ks[12], (N_B_LAYERS, HEAD_DIM)),
        "w_o_b": nrm(ks[13], (N_B_LAYERS, d, d), d ** -0.5),
        "w_gate_up": nrm(ks[14], (DEPTH, d, 2 * D_FF), d ** -0.5),
        "w_down": nrm(ks[15], (DEPTH, D_FF, d), D_FF ** -0.5),
    }


def reference(x, rel_bias, attn_norm, ffn_norm, w_qkv_a, q_norm_a, k_norm_a, w_o_a,
              kv_norm, w_kv_b, k_norm_b, w_q_b, q_norm_b, w_o_b, w_gate_up, w_down):
    for l in range(DEPTH):
        if l < N_A_LAYERS:
            x = x + dilated_mixer(rms_norm(x, attn_norm[l]), w_qkv_a[l], q_norm_a[l],
                                  k_norm_a[l], w_o_a[l], rel_bias)
        else:
            if l == N_A_LAYERS:
                kp, vp, kmean = shared_kv(x, kv_norm, w_kv_b, k_norm_b)
            i = l - N_A_LAYERS
            x = x + moba_mixer(rms_norm(x, attn_norm[l]), w_q_b[i], q_norm_b[i], w_o_b[i],
                               kp, vp, kmean, rel_bias)
        x = x + swiglu(rms_norm(x, ffn_norm[l]), w_gate_up[l], w_down[l])
    return x
```

```python
import functools
import math

import numpy as np
import jax
import jax.numpy as jnp
from jax import lax
from jax.experimental import pallas as pl
from jax.experimental.pallas import tpu as pltpu

D_MODEL = 4096
HEAD_DIM = 128
N_HEADS = D_MODEL // HEAD_DIM
DILATED_PATTERNS = ((128, 1), (512, 4), (2048, 16))
N_GROUPS = len(DILATED_PATTERNS)
BAND = 128
MOBA_BLOCK = 256
MOBA_TOPK = 3
N_BUCKETS = 32
MAX_DISTANCE = 2048
D_FF = 11008
EPS = 1e-6
NEG = -1e30
SCALE = HEAD_DIM ** -0.5

UNIT = 2048
BLOCKS_PER_UNIT = UNIT // BAND
N_BIAS_TILES = 8

V7X_SCOPED_VMEM_BYTES = 60000 * 1024
LANES = 128

F32 = jnp.float32
BF16 = jnp.bfloat16


def _params(semantics, vmem_bytes):
    return pltpu.CompilerParams(dimension_semantics=semantics,
                                vmem_limit_bytes=min(int(vmem_bytes), V7X_SCOPED_VMEM_BYTES))


def _rel_bucket_np(dist):
    n = np.maximum(dist, 0)
    exact = N_BUCKETS // 2
    nf = np.maximum(n, exact).astype(np.float64)
    large = exact + (np.log(nf / exact) / math.log(MAX_DISTANCE / exact)
                     * (N_BUCKETS - exact)).astype(np.int64)
    return np.where(n < exact, n, np.minimum(large, N_BUCKETS - 1)).astype(np.int32)


MASKED = N_BUCKETS


def _dilated_bucket_tiles():
    i = np.arange(BAND)[:, None]
    j = np.arange(2 * BAND)[None, :]
    tiles = []
    for window, d in DILATED_PATTERNS:
        w_sub = window // d
        per = []
        for first in (False, True):
            dist = (i - j) if first else (BAND + i - j)
            ok = (dist >= 0) & (dist <= w_sub)
            per.append(np.where(ok, _rel_bucket_np(dist * d), MASKED))
        tiles.append(np.stack(per))
    return np.stack(tiles).astype(np.int32)


def _moba_bucket_tiles():
    i = np.arange(MOBA_BLOCK)[:, None]
    j = np.arange(MOBA_BLOCK)[None, :]
    tiles = []
    for delta in range(N_BIAS_TILES - 1):
        dist = delta * MOBA_BLOCK + i - j
        b = _rel_bucket_np(dist)
        tiles.append(np.where(dist >= 0, b, MASKED) if delta == 0 else b)
    far = (N_BIAS_TILES - 2) * MOBA_BLOCK + 1
    assert _rel_bucket_np(np.array([far]))[0] == N_BUCKETS - 1
    tiles.append(np.full((MOBA_BLOCK, MOBA_BLOCK), N_BUCKETS - 1))
    return np.stack(tiles).astype(np.int32)


def _bias_expand_kernel(idx_ref, tab_ref, o_ref):
    idx = idx_ref[...]
    acc = jnp.full(o_ref.shape, NEG, F32)
    for b in range(N_BUCKETS):
        acc = jnp.where(idx == b, tab_ref[:, b:b + 1], acc)
    o_ref[...] = acc


def _bias_expand(idx_flat, rel_bias):
    n = idx_flat.shape[0]
    tn = 8192
    assert n % tn == 0
    tab = jnp.zeros((N_HEADS, LANES), F32).at[:, :N_BUCKETS].set(rel_bias.astype(F32).T)
    return pl.pallas_call(
        _bias_expand_kernel,
        out_shape=jax.ShapeDtypeStruct((N_HEADS, n), F32),
        grid=(n // tn,),
        in_specs=[pl.BlockSpec((1, tn), lambda i: (0, i)),
                  pl.BlockSpec((N_HEADS, LANES), lambda i: (0, 0))],
        out_specs=pl.BlockSpec((N_HEADS, tn), lambda i: (0, i)),
        compiler_params=_params(("parallel",), 16 * 2**20),
        name="bias_expand",
    )(idx_flat.reshape(1, n), tab)


def _rmsnorm_kernel(x_ref, g_ref, *o_refs):
    x = x_ref[...]
    y = x * lax.rsqrt(jnp.mean(x * x, axis=-1, keepdims=True) + EPS)
    for n, o_ref in enumerate(o_refs):
        o_ref[...] = (y * g_ref[n:n + 1, :]).astype(o_ref.dtype)


def _rmsnorm(x, gains):
    s, d = x.shape
    n_out = gains.shape[0]
    tr = 256
    outs = pl.pallas_call(
        _rmsnorm_kernel,
        out_shape=[jax.ShapeDtypeStruct((s, d), BF16)] * n_out,
        grid=(s // tr,),
        in_specs=[pl.BlockSpec((tr, d), lambda i: (i, 0)),
                  pl.BlockSpec((n_out, d), lambda i: (0, 0))],
        out_specs=[pl.BlockSpec((tr, d), lambda i: (i, 0))] * n_out,
        compiler_params=_params(("parallel",), 2 * tr * d * (4 + 2 * n_out) + 8 * 2**20),
        name="rmsnorm",
    )(x, gains.astype(F32))
    return outs


def _head_rms_cols(acc, gain_ref, c):
    t = acc[:, c * HEAD_DIM:(c + 1) * HEAD_DIM]
    t = t * lax.rsqrt(jnp.mean(t * t, axis=-1, keepdims=True) + EPS)
    return t * gain_ref[:, c * HEAD_DIM:(c + 1) * HEAD_DIM]


def _mm_headnorm_kernel(a_ref, b_ref, gain_ref, o_ref, *maybe_mean_ref, n_norm_blocks):
    acc = jnp.dot(a_ref[...], b_ref[...], preferred_element_type=F32)
    j = pl.program_id(1)
    tm, tn = acc.shape

    @pl.when(j < n_norm_blocks)
    def _():
        for c in range(tn // HEAD_DIM):
            t = _head_rms_cols(acc, gain_ref, c)
            o_ref[:, c * HEAD_DIM:(c + 1) * HEAD_DIM] = t.astype(o_ref.dtype)
            for mean_ref in maybe_mean_ref:
                for r in range(tm // MOBA_BLOCK):
                    blk = t[r * MOBA_BLOCK:(r + 1) * MOBA_BLOCK, :]
                    mean_ref[r:r + 1, c * HEAD_DIM:(c + 1) * HEAD_DIM] = jnp.mean(
                        blk, axis=0, keepdims=True)

    @pl.when(j >= n_norm_blocks)
    def _():
        o_ref[...] = acc.astype(o_ref.dtype)
        for mean_ref in maybe_mean_ref:
            mean_ref[...] = jnp.zeros(mean_ref.shape, mean_ref.dtype)


def _mm_headnorm(a, b, gain_row, n_norm_cols, out_dtype, with_block_means=False):
    m, k = a.shape
    n = b.shape[1]
    tm, tn = 1024, 1024
    assert m % tm == 0 and n % tn == 0 and n_norm_cols % tn == 0
    out_shape = [jax.ShapeDtypeStruct((m, n), out_dtype)]
    out_specs = [pl.BlockSpec((tm, tn), lambda i, j: (i, j))]
    if with_block_means:
        rows = tm // MOBA_BLOCK
        out_shape.append(jax.ShapeDtypeStruct((m // tm, rows, n), F32))
        out_specs.append(pl.BlockSpec((None, rows, tn), lambda i, j: (i, 0, j)))
    osz = jnp.dtype(out_dtype).itemsize
    vmem = 2 * (tm * k * 2 + k * tn * 2 + tm * tn * osz) + 3 * tm * tn * 4
    outs = pl.pallas_call(
        functools.partial(_mm_headnorm_kernel, n_norm_blocks=n_norm_cols // tn),
        out_shape=out_shape,
        grid=(m // tm, n // tn),
        in_specs=[pl.BlockSpec((tm, k), lambda i, j: (i, 0)),
                  pl.BlockSpec((k, tn), lambda i, j: (0, j)),
                  pl.BlockSpec((1, tn), lambda i, j: (0, j))],
        out_specs=out_specs,
        compiler_params=_params(("parallel", "arbitrary"), vmem),
        name="mm_headnorm",
    )(a, b, gain_row)
    if with_block_means:
        return outs[0], outs[1].reshape(m // MOBA_BLOCK, n)
    return outs[0]


def _mm_residual_kernel(a_ref, b_ref, r_ref, o_ref):
    o_ref[...] = r_ref[...] + jnp.dot(a_ref[...], b_ref[...], preferred_element_type=F32)


def _mm_residual(a, b, res, tm, tn):
    m, k = a.shape
    n = b.shape[1]
    assert m % tm == 0 and n % tn == 0
    vmem = 2 * (tm * k * 2 + k * tn * 2 + 2 * tm * tn * 4) + 2 * tm * tn * 4
    return pl.pallas_call(
        _mm_residual_kernel,
        out_shape=jax.ShapeDtypeStruct((m, n), F32),
        grid=(m // tm, n // tn),
        in_specs=[pl.BlockSpec((tm, k), lambda i, j: (i, 0)),
                  pl.BlockSpec((k, tn), lambda i, j: (0, j)),
                  pl.BlockSpec((tm, tn), lambda i, j: (i, j))],
        out_specs=pl.BlockSpec((tm, tn), lambda i, j: (i, j)),
        compiler_params=_params(("parallel", "arbitrary"), vmem),
        name="mm_residual",
    )(a, b, res)


def _mm_swiglu_kernel(a_ref, bg_ref, bu_ref, o_ref):
    a = a_ref[...]
    g = jnp.dot(a, bg_ref[...], preferred_element_type=F32)
    u = jnp.dot(a, bu_ref[...], preferred_element_type=F32)
    o_ref[...] = (g * (1.0 / (1.0 + jnp.exp(-g))) * u).astype(o_ref.dtype)


def _mm_swiglu(a, w_gate_up):
    m, k = a.shape
    tm, tn = 1024, 256
    assert m % tm == 0 and D_FF % tn == 0
    nb = D_FF // tn
    vmem = 2 * (tm * k * 2 + 2 * k * tn * 2 + tm * tn * 2) + 4 * tm * tn * 4
    return pl.pallas_call(
        _mm_swiglu_kernel,
        out_shape=jax.ShapeDtypeStruct((m, D_FF), BF16),
        grid=(m // tm, nb),
        in_specs=[pl.BlockSpec((tm, k), lambda i, j: (i, 0)),
                  pl.BlockSpec((k, tn), lambda i, j: (0, j)),
                  pl.BlockSpec((k, tn), lambda i, j: (0, j + nb))],
        out_specs=pl.BlockSpec((tm, tn), lambda i, j: (i, j)),
        compiler_params=_params(("parallel", "arbitrary"), vmem),
        name="mm_swiglu",
    )(a, w_gate_up, w_gate_up)


def _dilated_kernel(q0_ref, q1_ref, q2_ref, k_ref, v_ref, bias_ref, o_ref, *scratch):
    og, mg, lg = scratch[0:3], scratch[3:6], scratch[6:9]
    u = pl.program_id(1)
    q_refs = (q0_ref, q1_ref, q2_ref)

    for g, (_, d) in enumerate(DILATED_PATTERNS):
        shift = d.bit_length() - 1

        def block(b, carry, g=g, d=d, shift=shift):
            r = jnp.bitwise_and(b, d - 1)
            nl = jnp.right_shift(b, shift)
            rel = r + (BAND * d) * nl
            rows = pl.ds(rel, BAND, stride=d) if d > 1 else pl.ds(rel, BAND)
            q = q_refs[g][rows, :].astype(BF16)
            first = jnp.logical_and(u == 0, nl == 0)
            ks = jnp.where(first, r, u * UNIT + rel - BAND * d)
            krows = pl.ds(ks, 2 * BAND, stride=d) if d > 1 else pl.ds(ks, 2 * BAND)
            k = k_ref[krows, :].astype(BF16)
            v = v_ref[krows, :].astype(BF16)
            s = lax.dot_general(q, k, (((1,), (1,)), ((), ())), preferred_element_type=F32)
            s = s + bias_ref[g, first.astype(jnp.int32)]
            m = jnp.max(s, axis=1, keepdims=True)
            p = jnp.exp(s - m)
            l = jnp.sum(p, axis=1, keepdims=True)
            o = jnp.dot(p.astype(BF16), v, preferred_element_type=F32)
            og[g][rows, :] = o
            mg[g][rows, :] = jnp.broadcast_to(m, (BAND, HEAD_DIM))
            lg[g][rows, :] = jnp.broadcast_to(l, (BAND, HEAD_DIM))
            return carry

        lax.fori_loop(0, BLOCKS_PER_UNIT, block, 0)

    chunk = 256

    def merge(c, carry):
        rows = pl.ds(pl.multiple_of(c * chunk, chunk), chunk)
        ms = [mg[g][rows, :] for g in range(N_GROUPS)]
        top = jnp.maximum(jnp.maximum(ms[0], ms[1]), ms[2])
        num = jnp.zeros((chunk, HEAD_DIM), F32)
        den = jnp.zeros((chunk, HEAD_DIM), F32)
        for g in range(N_GROUPS):
            w = jnp.exp(ms[g] - top)
            num = num + w * og[g][rows, :]
            den = den + w * lg[g][rows, :]
        o_ref[rows, :] = (num / den).astype(o_ref.dtype)
        return carry

    lax.fori_loop(0, UNIT // chunk, merge, 0)


def _dilated_attention(qkv, bias):
    s = qkv.shape[0]
    assert s % UNIT == 0
    hcols = N_HEADS
    q_specs = [pl.BlockSpec((UNIT, HEAD_DIM), lambda h, u, g=g: (u, g * hcols + h))
               for g in range(N_GROUPS)]
    k_spec = pl.BlockSpec((s, HEAD_DIM), lambda h, u: (0, N_GROUPS * hcols + h))
    v_spec = pl.BlockSpec((s, HEAD_DIM), lambda h, u: (0, (N_GROUPS + 1) * hcols + h))
    b_spec = pl.BlockSpec((None, N_GROUPS, 2, BAND, 2 * BAND), lambda h, u: (h, 0, 0, 0, 0))
    blk = UNIT * HEAD_DIM * 4
    vmem = 2 * (3 * blk + 2 * s * HEAD_DIM * 4 + N_GROUPS * 2 * BAND * 2 * BAND * 4 + blk // 2) \
        + 9 * blk + 8 * 2**20
    return pl.pallas_call(
        _dilated_kernel,
        out_shape=jax.ShapeDtypeStruct((s, D_MODEL), BF16),
        grid=(N_HEADS, s // UNIT),
        in_specs=q_specs + [k_spec, v_spec, b_spec],
        out_specs=pl.BlockSpec((UNIT, HEAD_DIM), lambda h, u: (u, h)),
        scratch_shapes=[pltpu.VMEM((UNIT, HEAD_DIM), F32)] * 9,
        compiler_params=_params(("parallel", "arbitrary"), vmem),
        name="dilated_attention",
    )(qkv, qkv, qkv, qkv, qkv, bias)


def _moba_kernel(q_ref, k_ref, v_ref, kmean_ref, bias_ref, o_ref):
    i = pl.program_id(1)
    nb = kmean_ref.shape[0]
    q = q_ref[...]
    nt = (((1,), (1,)), ((), ()))

    gate = lax.dot_general(q, kmean_ref[...].astype(BF16), nt, preferred_element_type=F32)
    lane = lax.broadcasted_iota(jnp.int32, gate.shape, 1)
    past = lane < i
    gm = jnp.where(past, gate, NEG)
    memb = jnp.zeros(gate.shape, F32)
    for _ in range(MOBA_TOPK):
        mx = jnp.max(gm, axis=1, keepdims=True)
        idx = jnp.min(jnp.where(gm == mx, lane, nb), axis=1, keepdims=True)
        pick = lane == idx
        memb = jnp.where(pick, 1.0, memb)
        gm = jnp.where(pick, -jnp.inf, gm)
    memb = jnp.where(past, memb, 0.0).astype(BF16)

    own = pl.ds(pl.multiple_of(i * MOBA_BLOCK, MOBA_BLOCK), MOBA_BLOCK)
    s = lax.dot_general(q, k_ref[own, :], nt, preferred_element_type=F32) + bias_ref[0]
    m0 = jnp.max(s, axis=1, keepdims=True)
    p = jnp.exp(s - m0)
    l0 = jnp.sum(p, axis=1, keepdims=True)
    acc0 = jnp.dot(p.astype(BF16), v_ref[own, :], preferred_element_type=F32)

    def past_block(j, carry):
        m, l, acc = carry
        rows = pl.ds(pl.multiple_of(j * MOBA_BLOCK, MOBA_BLOCK), MOBA_BLOCK)
        s = lax.dot_general(q, k_ref[rows, :], nt, preferred_element_type=F32)
        s = s + bias_ref[jnp.minimum(i - j, N_BIAS_TILES - 1)]
        onehot = (lax.broadcasted_iota(jnp.int32, (nb, MOBA_BLOCK), 0) == j).astype(BF16)
        chosen = jnp.dot(memb, onehot, preferred_element_type=F32)
        s = jnp.where(chosen > 0.5, s, NEG)
        m_new = jnp.maximum(m, jnp.max(s, axis=1, keepdims=True))
        alpha = jnp.exp(m - m_new)
        p = jnp.exp(s - m_new)
        l = alpha * l + jnp.sum(p, axis=1, keepdims=True)
        acc = alpha * acc + jnp.dot(p.astype(BF16), v_ref[rows, :], preferred_element_type=F32)
        return m_new, l, acc

    _, l, acc = lax.fori_loop(0, i, past_block, (m0, l0, acc0))
    o_ref[...] = (acc / l).astype(o_ref.dtype)


def _moba_attention(q, kv, kmean, bias):
    s = q.shape[0]
    nb = s // MOBA_BLOCK
    kv_blk = s * HEAD_DIM * 2
    vmem = 2 * (2 * kv_blk + N_BIAS_TILES * MOBA_BLOCK * MOBA_BLOCK * 4) + 16 * 2**20
    return pl.pallas_call(
        _moba_kernel,
        out_shape=jax.ShapeDtypeStruct((s, D_MODEL), BF16),
        grid=(N_HEADS, nb),
        in_specs=[pl.BlockSpec((MOBA_BLOCK, HEAD_DIM), lambda h, i: (i, h)),
                  pl.BlockSpec((s, HEAD_DIM), lambda h, i: (0, h)),
                  pl.BlockSpec((s, HEAD_DIM), lambda h, i: (0, N_HEADS + h)),
                  pl.BlockSpec((nb, HEAD_DIM), lambda h, i: (0, h)),
                  pl.BlockSpec((None, N_BIAS_TILES, MOBA_BLOCK, MOBA_BLOCK),
                               lambda h, i: (h, 0, 0, 0))],
        out_specs=pl.BlockSpec((MOBA_BLOCK, HEAD_DIM), lambda h, i: (i, h)),
        compiler_params=_params(("parallel", "arbitrary"), vmem),
        name="moba_attention",
    )(q, kv, kv, kmean, bias)


def _ffn(x, norm_gain, w_gate_up, w_down):
    (h,) = _rmsnorm(x, norm_gain[None, :])
    act = _mm_swiglu(h, w_gate_up.astype(BF16))
    return _mm_residual(act, w_down.astype(BF16), x, tm=512, tn=512)


def kernel(x, rel_bias, attn_norm, ffn_norm, w_qkv_a, q_norm_a, k_norm_a, w_o_a, kv_norm, w_kv_b, k_norm_b, w_q_b, q_norm_b, w_o_b, w_gate_up, w_down):
    assert x.shape[0] == 1 and x.shape[2] == D_MODEL
    xs = x[0].astype(F32)
    d = D_MODEL

    dil_idx = _dilated_bucket_tiles()
    moba_idx = _moba_bucket_tiles()
    idx_flat = jnp.asarray(np.concatenate([dil_idx.ravel(), moba_idx.ravel()]))
    bias_all = _bias_expand(idx_flat, rel_bias)
    bias_dil = bias_all[:, :dil_idx.size].reshape((N_HEADS,) + dil_idx.shape)
    bias_moba = bias_all[:, dil_idx.size:].reshape((N_HEADS,) + moba_idx.shape)

    (h,) = _rmsnorm(xs, attn_norm[0][None, :])
    gain_a = jnp.concatenate(
        [jnp.tile(q_norm_a[0][g].astype(F32) * SCALE, N_HEADS) for g in range(N_GROUPS)]
        + [jnp.tile(k_norm_a[0].astype(F32), N_HEADS), jnp.ones((d,), F32)])[None, :]
    qkv = _mm_headnorm(h, w_qkv_a[0].astype(BF16), gain_a, (N_GROUPS + 1) * d, F32)
    o = _dilated_attention(qkv, bias_dil)
    xs = _mm_residual(o, w_o_a[0].astype(BF16), xs, tm=1024, tn=1024)
    xs = _ffn(xs, ffn_norm[0], w_gate_up[0], w_down[0])

    hk, hq = _rmsnorm(xs, jnp.stack([kv_norm, attn_norm[1]]))
    gain_kv = jnp.concatenate([jnp.tile(k_norm_b.astype(F32), N_HEADS), jnp.ones((d,), F32)])[None, :]
    kv, kmean = _mm_headnorm(hk, w_kv_b.astype(BF16), gain_kv, d, BF16, with_block_means=True)
    gain_q = jnp.tile(q_norm_b[0].astype(F32) * SCALE, N_HEADS)[None, :]
    q = _mm_headnorm(hq, w_q_b[0].astype(BF16), gain_q, d, BF16)
    o = _moba_attention(q, kv, kmean, bias_moba)
    xs = _mm_residual(o, w_o_b[0].astype(BF16), xs, tm=1024, tn=1024)
    xs = _ffn(xs, ffn_norm[1], w_gate_up[1], w_down[1])
    return xs[None].astype(x.dtype)
```

```python
import functools
import math

import numpy as np
import jax
import jax.numpy as jnp
from jax import lax
from jax.experimental import pallas as pl
from jax.experimental.pallas import tpu as pltpu

D_MODEL = 4096
HEAD_DIM = 128
N_HEADS = D_MODEL // HEAD_DIM
DILATED_PATTERNS = ((128, 1), (512, 4), (2048, 16))
N_GROUPS = len(DILATED_PATTERNS)
BAND = 128
MOBA_BLOCK = 256
MOBA_TOPK = 3
N_BUCKETS = 32
MAX_DISTANCE = 2048
D_FF = 11008
EPS = 1e-6
NEG = -1e30
SCALE = HEAD_DIM ** -0.5
LOG2E = math.log2(math.e)

UNIT = 2048
BLOCKS_PER_UNIT = UNIT // BAND
DIL_UNROLL = 4
N_BIAS_TILES = 8
MOBA_TILE_BLOCKS = 4

V7X_SCOPED_VMEM_BYTES = 60000 * 1024
LANES = 128

F32 = jnp.float32
BF16 = jnp.bfloat16


def _params(semantics, vmem_bytes):
    return pltpu.CompilerParams(dimension_semantics=semantics,
                                vmem_limit_bytes=min(int(vmem_bytes), V7X_SCOPED_VMEM_BYTES))


def _rel_bucket_np(dist):
    n = np.maximum(dist, 0)
    exact = N_BUCKETS // 2
    nf = np.maximum(n, exact).astype(np.float64)
    large = exact + (np.log(nf / exact) / math.log(MAX_DISTANCE / exact)
                     * (N_BUCKETS - exact)).astype(np.int64)
    return np.where(n < exact, n, np.minimum(large, N_BUCKETS - 1)).astype(np.int32)


MASKED = N_BUCKETS


def _dilated_bucket_tiles():
    i = np.arange(BAND)[:, None]
    j = np.arange(2 * BAND)[None, :]
    tiles = []
    for window, d in DILATED_PATTERNS:
        w_sub = window // d
        per = []
        for first in (False, True):
            dist = (i - j) if first else (BAND + i - j)
            ok = (dist >= 0) & (dist <= w_sub)
            per.append(np.where(ok, _rel_bucket_np(dist * d), MASKED))
        tiles.append(np.stack(per))
    return np.stack(tiles).astype(np.int32)


def _moba_bucket_tiles():
    i = np.arange(MOBA_BLOCK)[None, :]
    j = np.arange(MOBA_BLOCK)[:, None]
    tiles = []
    for delta in range(N_BIAS_TILES - 1):
        dist = delta * MOBA_BLOCK + i - j
        b = _rel_bucket_np(dist)
        tiles.append(np.where(dist >= 0, b, MASKED) if delta == 0 else b)
    far = (N_BIAS_TILES - 2) * MOBA_BLOCK + 1
    assert _rel_bucket_np(np.array([far]))[0] == N_BUCKETS - 1
    tiles.append(np.full((MOBA_BLOCK, MOBA_BLOCK), N_BUCKETS - 1))
    return np.stack(tiles).astype(np.int32)


def _bias_expand_kernel(idx_ref, tab_ref, o_ref):
    idx = idx_ref[...]
    acc = jnp.full(o_ref.shape, NEG, F32)
    for b in range(N_BUCKETS):
        acc = jnp.where(idx == b, tab_ref[:, b:b + 1], acc)
    o_ref[...] = acc


def _bias_expand(idx_np, table):
    idx_flat = jnp.asarray(idx_np.ravel())
    n = idx_flat.shape[0]
    tn = 8192
    assert n % tn == 0
    tab = jnp.zeros((N_HEADS, LANES), F32).at[:, :N_BUCKETS].set(table.astype(F32).T)
    return pl.pallas_call(
        _bias_expand_kernel,
        out_shape=jax.ShapeDtypeStruct((N_HEADS, n), F32),
        grid=(n // tn,),
        in_specs=[pl.BlockSpec((1, tn), lambda i: (0, i)),
                  pl.BlockSpec((N_HEADS, LANES), lambda i: (0, 0))],
        out_specs=pl.BlockSpec((N_HEADS, tn), lambda i: (0, i)),
        compiler_params=_params(("parallel",), 16 * 2**20),
        name="bias_expand",
    )(idx_flat.reshape(1, n), tab).reshape((N_HEADS,) + idx_np.shape)


def _rmsnorm_kernel(x_ref, g_ref, *o_refs):
    x = x_ref[...]
    y = x * lax.rsqrt(jnp.mean(x * x, axis=-1, keepdims=True) + EPS)
    for n, o_ref in enumerate(o_refs):
        o_ref[...] = (y * g_ref[n:n + 1, :]).astype(o_ref.dtype)


def _rmsnorm(x, gains):
    s, d = x.shape
    n_out = gains.shape[0]
    tr = 256
    outs = pl.pallas_call(
        _rmsnorm_kernel,
        out_shape=[jax.ShapeDtypeStruct((s, d), BF16)] * n_out,
        grid=(s // tr,),
        in_specs=[pl.BlockSpec((tr, d), lambda i: (i, 0)),
                  pl.BlockSpec((n_out, d), lambda i: (0, 0))],
        out_specs=[pl.BlockSpec((tr, d), lambda i: (i, 0))] * n_out,
        compiler_params=_params(("parallel",), 2 * tr * d * (4 + 2 * n_out) + 8 * 2**20),
        name="rmsnorm",
    )(x, gains.astype(F32))
    return outs


def _head_rms_cols(acc, gain_ref, c):
    t = acc[:, c * HEAD_DIM:(c + 1) * HEAD_DIM]
    t = t * lax.rsqrt(jnp.mean(t * t, axis=-1, keepdims=True) + EPS)
    return t * gain_ref[:, c * HEAD_DIM:(c + 1) * HEAD_DIM]


def _mm_headnorm_kernel(a_ref, b_ref, gain_ref, o_ref, *, n_norm_blocks):
    acc = jnp.dot(a_ref[...], b_ref[...], preferred_element_type=F32)
    j = pl.program_id(1)

    @pl.when(j < n_norm_blocks)
    def _():
        for c in range(acc.shape[1] // HEAD_DIM):
            t = _head_rms_cols(acc, gain_ref, c)
            o_ref[:, c * HEAD_DIM:(c + 1) * HEAD_DIM] = t.astype(o_ref.dtype)

    @pl.when(j >= n_norm_blocks)
    def _():
        o_ref[...] = acc.astype(o_ref.dtype)


def _mm_headnorm(a, b, gain_row, n_norm_cols, out_dtype):
    m, k = a.shape
    n = b.shape[1]
    tm, tn = 1024, 1024
    assert m % tm == 0 and n % tn == 0 and n_norm_cols % tn == 0
    osz = jnp.dtype(out_dtype).itemsize
    vmem = 2 * (tm * k * 2 + k * tn * 2 + tm * tn * osz) + 3 * tm * tn * 4
    return pl.pallas_call(
        functools.partial(_mm_headnorm_kernel, n_norm_blocks=n_norm_cols // tn),
        out_shape=jax.ShapeDtypeStruct((m, n), out_dtype),
        grid=(m // tm, n // tn),
        in_specs=[pl.BlockSpec((tm, k), lambda i, j: (i, 0)),
                  pl.BlockSpec((k, tn), lambda i, j: (0, j)),
                  pl.BlockSpec((1, tn), lambda i, j: (0, j))],
        out_specs=pl.BlockSpec((tm, tn), lambda i, j: (i, j)),
        compiler_params=_params(("parallel", "arbitrary"), vmem),
        name="mm_headnorm",
    )(a, b, gain_row)


def _mm_shared_kv_kernel(a_ref, b_ref, gain_ref, k_ref, mean_ref, vt_ref, *, n_k_blocks):
    acc = jnp.dot(a_ref[...], b_ref[...], preferred_element_type=F32)
    j = pl.program_id(1)
    tm, tn = acc.shape

    @pl.when(j < n_k_blocks)
    def _():
        for c in range(tn // HEAD_DIM):
            cols = slice(c * HEAD_DIM, (c + 1) * HEAD_DIM)
            t = _head_rms_cols(acc, gain_ref, c)
            k_ref[:, cols] = t.astype(k_ref.dtype)
            for r in range(tm // MOBA_BLOCK):
                blk = t[r * MOBA_BLOCK:(r + 1) * MOBA_BLOCK, :]
                mean_ref[r:r + 1, cols] = jnp.mean(blk, axis=0, keepdims=True)

    @pl.when(j >= n_k_blocks)
    def _():
        vt_ref[...] = acc.T.astype(vt_ref.dtype)


def _mm_shared_kv(a, b, gain_row):
    m, k = a.shape
    n = b.shape[1]
    d = n // 2
    tm, tn = 1024, 1024
    assert m % tm == 0 and d % tn == 0
    nkb = d // tn
    rows = tm // MOBA_BLOCK
    vmem = 2 * (tm * k * 2 + k * tn * 2 + 2 * tm * tn * 2) + 4 * tm * tn * 4
    kn, kmean, vt = pl.pallas_call(
        functools.partial(_mm_shared_kv_kernel, n_k_blocks=nkb),
        out_shape=[jax.ShapeDtypeStruct((m, d), BF16),
                   jax.ShapeDtypeStruct((m // tm, rows, d), F32),
                   jax.ShapeDtypeStruct((m // tm, d, tm), BF16)],
        grid=(m // tm, n // tn),
        in_specs=[pl.BlockSpec((tm, k), lambda i, j: (i, 0)),
                  pl.BlockSpec((k, tn), lambda i, j: (0, j)),
                  pl.BlockSpec((1, tn), lambda i, j: (0, jnp.minimum(j, nkb - 1)))],
        out_specs=[pl.BlockSpec((tm, tn), lambda i, j: (i, jnp.minimum(j, nkb - 1))),
                   pl.BlockSpec((None, rows, tn), lambda i, j: (i, 0, jnp.minimum(j, nkb - 1))),
                   pl.BlockSpec((None, tn, tm), lambda i, j: (i, jnp.maximum(j - nkb, 0), 0))],
        compiler_params=_params(("parallel", "arbitrary"), vmem),
        name="mm_shared_kv",
    )(a, b, gain_row)
    return kn, kmean.reshape(m // MOBA_BLOCK, d), vt


def _mm_residual_kernel(a_ref, b_ref, r_ref, o_ref):
    o_ref[...] = r_ref[...] + jnp.dot(a_ref[...], b_ref[...], preferred_element_type=F32)


def _mm_residual(a, b, res, tm, tn):
    m, k = a.shape
    n = b.shape[1]
    assert m % tm == 0 and n % tn == 0
    vmem = 2 * (tm * k * 2 + k * tn * 2 + 2 * tm * tn * 4) + 2 * tm * tn * 4
    return pl.pallas_call(
        _mm_residual_kernel,
        out_shape=jax.ShapeDtypeStruct((m, n), F32),
        grid=(m // tm, n // tn),
        in_specs=[pl.BlockSpec((tm, k), lambda i, j: (i, 0)),
                  pl.BlockSpec((k, tn), lambda i, j: (0, j)),
                  pl.BlockSpec((tm, tn), lambda i, j: (i, j))],
        out_specs=pl.BlockSpec((tm, tn), lambda i, j: (i, j)),
        compiler_params=_params(("parallel", "arbitrary"), vmem),
        name="mm_residual",
    )(a, b, res)


def _mm_swiglu_kernel(a_ref, bg_ref, bu_ref, o_ref):
    a = a_ref[...]
    g = jnp.dot(a, bg_ref[...], preferred_element_type=F32)
    u = jnp.dot(a, bu_ref[...], preferred_element_type=F32)
    o_ref[...] = (g * (1.0 / (1.0 + jnp.exp(-g))) * u).astype(o_ref.dtype)


def _mm_swiglu(a, w_gate_up):
    m, k = a.shape
    tm, tn = 1024, 256
    assert m % tm == 0 and D_FF % tn == 0
    nb = D_FF // tn
    vmem = 2 * (tm * k * 2 + 2 * k * tn * 2 + tm * tn * 2) + 4 * tm * tn * 4
    return pl.pallas_call(
        _mm_swiglu_kernel,
        out_shape=jax.ShapeDtypeStruct((m, D_FF), BF16),
        grid=(m // tm, nb),
        in_specs=[pl.BlockSpec((tm, k), lambda i, j: (i, 0)),
                  pl.BlockSpec((k, tn), lambda i, j: (0, j)),
                  pl.BlockSpec((k, tn), lambda i, j: (0, j + nb))],
        out_specs=pl.BlockSpec((tm, tn), lambda i, j: (i, j)),
        compiler_params=_params(("parallel", "arbitrary"), vmem),
        name="mm_swiglu",
    )(a, w_gate_up, w_gate_up)


def _dilated_kernel(q0_ref, q1_ref, q2_ref, k_ref, v_ref, bias_ref, o_ref, *scratch):
    og, mg, lg = scratch[0:3], scratch[3:6], scratch[6:9]
    u = pl.program_id(1)
    q_refs = (q0_ref, q1_ref, q2_ref)

    for g, (_, d) in enumerate(DILATED_PATTERNS):
        shift = d.bit_length() - 1

        def block(b, g=g, d=d, shift=shift):
            r = jnp.bitwise_and(b, d - 1)
            nl = jnp.right_shift(b, shift)
            rel = r + (BAND * d) * nl
            rows = pl.ds(rel, BAND, stride=d) if d > 1 else pl.ds(rel, BAND)
            q = q_refs[g][rows, :].astype(BF16)
            first = jnp.logical_and(u == 0, nl == 0)
            ks = jnp.where(first, r, u * UNIT + rel - BAND * d)
            krows = pl.ds(ks, 2 * BAND, stride=d) if d > 1 else pl.ds(ks, 2 * BAND)
            k = k_ref[krows, :].astype(BF16)
            v = v_ref[krows, :].astype(BF16)
            s = lax.dot_general(q, k, (((1,), (1,)), ((), ())), preferred_element_type=F32)
            s = s + bias_ref[g, first.astype(jnp.int32)]
            m = jnp.max(s, axis=1, keepdims=True)
            p = jnp.exp2(s - m)
            l = jnp.sum(p, axis=1, keepdims=True)
            o = jnp.dot(p.astype(BF16), v, preferred_element_type=F32)
            og[g][rows, :] = o
            mg[g][rows, :] = jnp.broadcast_to(m, (BAND, HEAD_DIM))
            lg[g][rows, :] = jnp.broadcast_to(l, (BAND, HEAD_DIM))

        def blocks(it, carry, block=block):
            for c in range(DIL_UNROLL):
                block(it * DIL_UNROLL + c)
            return carry

        lax.fori_loop(0, BLOCKS_PER_UNIT // DIL_UNROLL, blocks, 0)

    chunk = 256

    def merge(c, carry):
        rows = pl.ds(pl.multiple_of(c * chunk, chunk), chunk)
        ms = [mg[g][rows, :] for g in range(N_GROUPS)]
        top = jnp.maximum(jnp.maximum(ms[0], ms[1]), ms[2])
        num = jnp.zeros((chunk, HEAD_DIM), F32)
        den = jnp.zeros((chunk, HEAD_DIM), F32)
        for g in range(N_GROUPS):
            w = jnp.exp2(ms[g] - top)
            num = num + w * og[g][rows, :]
            den = den + w * lg[g][rows, :]
        o_ref[rows, :] = (num / den).astype(o_ref.dtype)
        return carry

    lax.fori_loop(0, UNIT // chunk, merge, 0)


def _dilated_attention(qkv, bias):
    s = qkv.shape[0]
    assert s % UNIT == 0
    hcols = N_HEADS
    q_specs = [pl.BlockSpec((UNIT, HEAD_DIM), lambda h, u, g=g: (u, g * hcols + h))
               for g in range(N_GROUPS)]
    k_spec = pl.BlockSpec((s, HEAD_DIM), lambda h, u: (0, N_GROUPS * hcols + h))
    v_spec = pl.BlockSpec((s, HEAD_DIM), lambda h, u: (0, (N_GROUPS + 1) * hcols + h))
    b_spec = pl.BlockSpec((None, N_GROUPS, 2, BAND, 2 * BAND), lambda h, u: (h, 0, 0, 0, 0))
    blk = UNIT * HEAD_DIM * 4
    vmem = 2 * (3 * blk + 2 * s * HEAD_DIM * 4 + N_GROUPS * 2 * BAND * 2 * BAND * 4 + blk // 2) \
        + 9 * blk + 8 * 2**20
    return pl.pallas_call(
        _dilated_kernel,
        out_shape=jax.ShapeDtypeStruct((s, D_MODEL), BF16),
        grid=(N_HEADS, s // UNIT),
        in_specs=q_specs + [k_spec, v_spec, b_spec],
        out_specs=pl.BlockSpec((UNIT, HEAD_DIM), lambda h, u: (u, h)),
        scratch_shapes=[pltpu.VMEM((UNIT, HEAD_DIM), F32)] * 9,
        compiler_params=_params(("parallel", "arbitrary"), vmem),
        name="dilated_attention",
    )(qkv, qkv, qkv, qkv, qkv, bias)


def _moba_kernel(q_ref, k_ref, vt_ref, blk_ref, kmean_ref, bias_ref, o_ref, s0_ref, s1_ref):
    i = pl.program_id(1)
    nb = kmean_ref.shape[0]
    q = q_ref[...]
    nt = (((1,), (1,)), ((), ()))
    tile = MOBA_TILE_BLOCKS * MOBA_BLOCK

    kmean = jnp.concatenate([kmean_ref[...].astype(BF16), jnp.zeros((LANES - nb, HEAD_DIM), BF16)], axis=0)
    gate = lax.dot_general(q, kmean, nt, preferred_element_type=F32)
    lane = lax.broadcasted_iota(jnp.int32, gate.shape, 1)
    lane_f = lane.astype(F32)
    past = lane < i
    gm = jnp.where(past, gate, NEG)
    picked = jnp.zeros(gate.shape, jnp.bool_)
    for _ in range(MOBA_TOPK):
        mx = jnp.max(gm, axis=1, keepdims=True)
        first = jnp.min(jnp.where(gm == mx, lane_f, float(LANES)), axis=1, keepdims=True)
        pick = lane_f == first
        picked = jnp.logical_or(picked, pick)
        gm = jnp.where(pick, -jnp.inf, gm)
    allowed = jnp.logical_or(jnp.logical_and(picked, past), lane == i)
    pen = jnp.where(allowed, 0.0, NEG).astype(BF16)
    qx = jnp.concatenate([q, pen], axis=1)

    last_tile = vt_ref.shape[0] - 1

    def scores(t, s_ref):
        t = jnp.minimum(t, last_tile)
        rows = pl.ds(pl.multiple_of(t * tile, tile), tile)
        kx = jnp.concatenate([k_ref[rows, :], blk_ref[rows, :]], axis=1)
        s_ref[...] = lax.dot_general(kx, qx, nt, preferred_element_type=F32)

    def update(t, s_ref, carry):
        m, l, acc = carry
        offs = [jnp.clip(i - (t * MOBA_TILE_BLOCKS + c), 0, N_BIAS_TILES - 1)
                for c in range(MOBA_TILE_BLOCKS)]
        s = s_ref[...] + jnp.concatenate([bias_ref[o] for o in offs], axis=0)
        m_new = jnp.maximum(m, jnp.max(s, axis=0, keepdims=True))
        alpha = jnp.exp2(m - m_new)
        p = jnp.exp2(s - m_new)
        l = alpha * l + jnp.sum(p, axis=0, keepdims=True)
        acc = alpha * acc + jnp.dot(vt_ref[t], p.astype(BF16), preferred_element_type=F32)
        return m_new, l, acc

    def tile_pair(u, carry):
        t0 = 2 * u
        scores(t0 + 1, s1_ref)
        carry = update(t0, s0_ref, carry)
        scores(t0 + 2, s0_ref)
        return update(jnp.minimum(t0 + 1, last_tile), s1_ref, carry)

    n_tiles = (i + MOBA_TILE_BLOCKS) // MOBA_TILE_BLOCKS
    scores(0, s0_ref)
    carry = (jnp.full((1, MOBA_BLOCK), NEG, F32), jnp.zeros((1, MOBA_BLOCK), F32),
             jnp.zeros((HEAD_DIM, MOBA_BLOCK), F32))
    _, l, acc = lax.fori_loop(0, (n_tiles + 1) // 2, tile_pair, carry)
    o_ref[...] = (acc / l).T.astype(o_ref.dtype)


def _moba_attention(q, k, vt, kmean, bias):
    s = q.shape[0]
    nb = s // MOBA_BLOCK
    tile = MOBA_TILE_BLOCKS * MOBA_BLOCK
    assert nb % (2 * MOBA_TILE_BLOCKS) == 0 and nb <= LANES and vt.shape == (s // tile, D_MODEL, tile)
    key_block = np.arange(s)[:, None] // MOBA_BLOCK == np.arange(HEAD_DIM)[None, :]
    blk_onehot = jnp.asarray(key_block, BF16)
    kv_blk = s * HEAD_DIM * 2
    vmem = 2 * (3 * kv_blk + N_BIAS_TILES * MOBA_BLOCK * MOBA_BLOCK * 4) + 24 * 2**20
    return pl.pallas_call(
        _moba_kernel,
        out_shape=jax.ShapeDtypeStruct((s, D_MODEL), BF16),
        grid=(N_HEADS, nb),
        in_specs=[pl.BlockSpec((MOBA_BLOCK, HEAD_DIM), lambda h, i: (i, h)),
                  pl.BlockSpec((s, HEAD_DIM), lambda h, i: (0, h)),
                  pl.BlockSpec((s // tile, HEAD_DIM, tile), lambda h, i: (0, h, 0)),
                  pl.BlockSpec((s, HEAD_DIM), lambda h, i: (0, 0)),
                  pl.BlockSpec((nb, HEAD_DIM), lambda h, i: (0, h)),
                  pl.BlockSpec((None, N_BIAS_TILES, MOBA_BLOCK, MOBA_BLOCK),
                               lambda h, i: (h, 0, 0, 0))],
        out_specs=pl.BlockSpec((MOBA_BLOCK, HEAD_DIM), lambda h, i: (i, h)),
        scratch_shapes=[pltpu.VMEM((tile, MOBA_BLOCK), F32)] * 2,
        compiler_params=_params(("parallel", "arbitrary"), vmem),
        name="moba_attention",
    )(q, k, vt, blk_onehot, kmean, bias)


def _ffn(x, norm_gain, w_gate_up, w_down):
    (h,) = _rmsnorm(x, norm_gain[None, :])
    act = _mm_swiglu(h, w_gate_up.astype(BF16))
    return _mm_residual(act, w_down.astype(BF16), x, tm=512, tn=512)


def kernel(x, rel_bias, attn_norm, ffn_norm, w_qkv_a, q_norm_a, k_norm_a, w_o_a, kv_norm, w_kv_b, k_norm_b, w_q_b, q_norm_b, w_o_b, w_gate_up, w_down):
    assert x.shape[0] == 1 and x.shape[2] == D_MODEL
    xs = x[0].astype(F32)
    d = D_MODEL

    table = rel_bias.astype(F32) * LOG2E
    bias_dil = _bias_expand(_dilated_bucket_tiles(), table)
    bias_moba = _bias_expand(_moba_bucket_tiles(), table - table[N_BUCKETS - 1:, :])

    (h,) = _rmsnorm(xs, attn_norm[0][None, :])
    gain_a = jnp.concatenate(
        [jnp.tile(q_norm_a[0][g].astype(F32) * (SCALE * LOG2E), N_HEADS) for g in range(N_GROUPS)]
        + [jnp.tile(k_norm_a[0].astype(F32), N_HEADS), jnp.ones((d,), F32)])[None, :]
    qkv = _mm_headnorm(h, w_qkv_a[0].astype(BF16), gain_a, (N_GROUPS + 1) * d, F32)
    o = _dilated_attention(qkv, bias_dil)
    xs = _mm_residual(o, w_o_a[0].astype(BF16), xs, tm=1024, tn=1024)
    xs = _ffn(xs, ffn_norm[0], w_gate_up[0], w_down[0])

    hk, hq = _rmsnorm(xs, jnp.stack([kv_norm, attn_norm[1]]))
    gain_k = jnp.tile(k_norm_b.astype(F32), N_HEADS)[None, :]
    k, kmean, vt = _mm_shared_kv(hk, w_kv_b.astype(BF16), gain_k)
    gain_q = jnp.tile(q_norm_b[0].astype(F32) * (SCALE * LOG2E), N_HEADS)[None, :]
    q = _mm_headnorm(hq, w_q_b[0].astype(BF16), gain_q, d, BF16)
    o = _moba_attention(q, k, vt, kmean, bias_moba)
    xs = _mm_residual(o, w_o_b[0].astype(BF16), xs, tm=1024, tn=1024)
    xs = _ffn(xs, ffn_norm[1], w_gate_up[1], w_down[1])
    return xs[None].astype(x.dtype)
```

```python
import functools
import math

import numpy as np
import jax
import jax.numpy as jnp
from jax import lax
from jax.experimental import pallas as pl
from jax.experimental.pallas import tpu as pltpu

D_MODEL = 4096
HEAD_DIM = 128
N_HEADS = D_MODEL // HEAD_DIM
DILATED_PATTERNS = ((128, 1), (512, 4), (2048, 16))
N_GROUPS = len(DILATED_PATTERNS)
BAND = 128
MOBA_BLOCK = 256
MOBA_TOPK = 3
N_BUCKETS = 32
MAX_DISTANCE = 2048
D_FF = 11008
EPS = 1e-6
NEG = -1e30
SCALE = HEAD_DIM ** -0.5
LOG2E = math.log2(math.e)

UNIT = 2048
BLOCKS_PER_UNIT = UNIT // BAND
DIL_UNROLL = 16
N_BIAS_TILES = 8
MOBA_TILE_BLOCKS = 4

V7X_SCOPED_VMEM_BYTES = 60000 * 1024
LANES = 128

F32 = jnp.float32
BF16 = jnp.bfloat16


def _params(semantics, vmem_bytes):
    return pltpu.CompilerParams(dimension_semantics=semantics,
                                vmem_limit_bytes=min(int(vmem_bytes), V7X_SCOPED_VMEM_BYTES))


def _rel_bucket_np(dist):
    n = np.maximum(dist, 0)
    exact = N_BUCKETS // 2
    nf = np.maximum(n, exact).astype(np.float64)
    large = exact + (np.log(nf / exact) / math.log(MAX_DISTANCE / exact)
                     * (N_BUCKETS - exact)).astype(np.int64)
    return np.where(n < exact, n, np.minimum(large, N_BUCKETS - 1)).astype(np.int32)


MASKED = N_BUCKETS


def _dilated_bucket_tiles():
    i = np.arange(BAND)[:, None]
    j = np.arange(2 * BAND)[None, :]
    tiles = []
    for window, d in DILATED_PATTERNS:
        w_sub = window // d
        per = []
        for first in (False, True):
            dist = (i - j) if first else (BAND + i - j)
            ok = (dist >= 0) & (dist <= w_sub)
            per.append(np.where(ok, _rel_bucket_np(dist * d), MASKED))
        tiles.append(np.stack(per))
    return np.stack(tiles).astype(np.int32)


def _moba_bucket_tiles():
    i = np.arange(MOBA_BLOCK)[None, :]
    j = np.arange(MOBA_BLOCK)[:, None]
    tiles = []
    for delta in range(N_BIAS_TILES - 1):
        dist = delta * MOBA_BLOCK + i - j
        b = _rel_bucket_np(dist)
        tiles.append(np.where(dist >= 0, b, MASKED) if delta == 0 else b)
    far = (N_BIAS_TILES - 2) * MOBA_BLOCK + 1
    assert _rel_bucket_np(np.array([far]))[0] == N_BUCKETS - 1
    tiles.append(np.full((MOBA_BLOCK, MOBA_BLOCK), N_BUCKETS - 1))
    return np.stack(tiles).astype(np.int32)


def _bias_expand_kernel(idx_ref, tab_ref, o_ref):
    idx = idx_ref[...]
    acc = jnp.full(o_ref.shape, NEG, F32)
    for b in range(N_BUCKETS):
        acc = jnp.where(idx == b, tab_ref[:, b:b + 1], acc)
    o_ref[...] = acc


def _bias_expand(idx_np, table):
    idx_flat = jnp.asarray(idx_np.ravel())
    n = idx_flat.shape[0]
    tn = 8192
    assert n % tn == 0
    tab = jnp.zeros((N_HEADS, LANES), F32).at[:, :N_BUCKETS].set(table.astype(F32).T)
    return pl.pallas_call(
        _bias_expand_kernel,
        out_shape=jax.ShapeDtypeStruct((N_HEADS, n), F32),
        grid=(n // tn,),
        in_specs=[pl.BlockSpec((1, tn), lambda i: (0, i)),
                  pl.BlockSpec((N_HEADS, LANES), lambda i: (0, 0))],
        out_specs=pl.BlockSpec((N_HEADS, tn), lambda i: (0, i)),
        compiler_params=_params(("parallel",), 16 * 2**20),
        name="bias_expand",
    )(idx_flat.reshape(1, n), tab).reshape((N_HEADS,) + idx_np.shape)


def _rmsnorm_kernel(x_ref, g_ref, *o_refs):
    x = x_ref[...]
    y = x * lax.rsqrt(jnp.mean(x * x, axis=-1, keepdims=True) + EPS)
    for n, o_ref in enumerate(o_refs):
        o_ref[...] = (y * g_ref[n:n + 1, :]).astype(o_ref.dtype)


def _rmsnorm(x, gains):
    s, d = x.shape
    n_out = gains.shape[0]
    tr = 256
    outs = pl.pallas_call(
        _rmsnorm_kernel,
        out_shape=[jax.ShapeDtypeStruct((s, d), BF16)] * n_out,
        grid=(s // tr,),
        in_specs=[pl.BlockSpec((tr, d), lambda i: (i, 0)),
                  pl.BlockSpec((n_out, d), lambda i: (0, 0))],
        out_specs=[pl.BlockSpec((tr, d), lambda i: (i, 0))] * n_out,
        compiler_params=_params(("parallel",), 2 * tr * d * (4 + 2 * n_out) + 8 * 2**20),
        name="rmsnorm",
    )(x, gains.astype(F32))
    return outs


def _head_rms_cols(acc, gain_ref, c):
    t = acc[:, c * HEAD_DIM:(c + 1) * HEAD_DIM]
    t = t * lax.rsqrt(jnp.mean(t * t, axis=-1, keepdims=True) + EPS)
    return t * gain_ref[:, c * HEAD_DIM:(c + 1) * HEAD_DIM]


def _wmatmul_kernel(*refs, epilogue, n_w, n_extra, n_out, cast):
    a_ref = refs[0]
    w_refs = refs[1:1 + n_w]
    extra_refs = refs[1 + n_w:1 + n_w + n_extra]
    out_refs = refs[1 + n_w + n_extra:1 + n_w + n_extra + n_out]
    if cast:
        wb_refs = refs[1 + n_w + n_extra + n_out:]

        @pl.when(pl.program_id(1) == 0)
        def _():
            for w_ref, wb_ref in zip(w_refs, wb_refs):
                wb_ref[...] = w_ref[...].astype(BF16)
    else:
        wb_refs = w_refs
    a = a_ref[...]
    accs = [jnp.dot(a, wb_ref[...], preferred_element_type=F32) for wb_ref in wb_refs]
    epilogue(accs, extra_refs, out_refs)


def _wmatmul(epilogue, a, weights, extra, outs, *, n_cols, tm, tn, name):
    m, k = a.shape
    assert m % tm == 0 and n_cols % tn == 0
    cast = weights[0][0].dtype != BF16
    w_specs = [pl.BlockSpec((None, k, tn), lambda j, i, layer=layer, off=off: (layer, 0, j + off))
               for _, layer, off in weights]
    wsz = weights[0][0].dtype.itemsize
    blocks = tm * k * 2 + len(weights) * k * tn * wsz
    for arr, spec in list(extra) + [(o, s) for o, s in outs]:
        blocks += math.prod(d for d in spec.block_shape if d is not None) * jnp.dtype(arr.dtype).itemsize
    vmem = 2 * blocks + (len(weights) * k * tn * 2 if cast else 0) + (2 + 2 * len(weights)) * tm * tn * 4
    return pl.pallas_call(
        functools.partial(_wmatmul_kernel, epilogue=epilogue, n_w=len(weights), n_extra=len(extra),
                          n_out=len(outs), cast=cast),
        out_shape=[o for o, _ in outs],
        grid=(n_cols // tn, m // tm),
        in_specs=[pl.BlockSpec((tm, k), lambda j, i: (i, 0))] + w_specs + [s for _, s in extra],
        out_specs=[s for _, s in outs],
        scratch_shapes=[pltpu.VMEM((k, tn), BF16)] * len(weights) if cast else [],
        compiler_params=_params(("parallel", "arbitrary"), vmem),
        name=name,
    )(a, *[w for w, _, _ in weights], *[arr for arr, _ in extra])


PROJ_TM, PROJ_TN = 1024, 512


def _gain_spec(tn):
    return pl.BlockSpec((1, tn), lambda j, i: (0, j))


def _headnorm_epilogue(accs, extra_refs, out_refs, *, n_norm_blocks):
    (acc,), (gain_ref,), (o_ref,) = accs, extra_refs, out_refs
    j = pl.program_id(0)

    @pl.when(j < n_norm_blocks)
    def _():
        for c in range(acc.shape[1] // HEAD_DIM):
            t = _head_rms_cols(acc, gain_ref, c)
            o_ref[:, c * HEAD_DIM:(c + 1) * HEAD_DIM] = t.astype(o_ref.dtype)

    @pl.when(j >= n_norm_blocks)
    def _():
        o_ref[...] = acc.astype(o_ref.dtype)


def _mm_headnorm(a, w, layer, gain_row, n_norm_cols, out_dtype):
    m = a.shape[0]
    n = w.shape[2]
    tm, tn = PROJ_TM, PROJ_TN
    assert n_norm_cols % tn == 0
    (out,) = _wmatmul(
        functools.partial(_headnorm_epilogue, n_norm_blocks=n_norm_cols // tn),
        a, [(w, layer, 0)], [(gain_row, _gain_spec(tn))],
        [(jax.ShapeDtypeStruct((m, n), out_dtype), pl.BlockSpec((tm, tn), lambda j, i: (i, j)))],
        n_cols=n, tm=tm, tn=tn, name="mm_headnorm")
    return out


def _headnorm_t_epilogue(accs, extra_refs, out_refs):
    (acc,), (gain_ref,), (ot_ref,) = accs, extra_refs, out_refs
    for c in range(acc.shape[1] // HEAD_DIM):
        t = _head_rms_cols(acc, gain_ref, c)
        ot_ref[c * HEAD_DIM:(c + 1) * HEAD_DIM, :] = t.T.astype(ot_ref.dtype)


def _mm_headnorm_t(a, w, layer, gain_row):
    m = a.shape[0]
    n = w.shape[2]
    tm, tn = PROJ_TM, PROJ_TN
    (out,) = _wmatmul(
        _headnorm_t_epilogue, a, [(w, layer, 0)], [(gain_row, _gain_spec(tn))],
        [(jax.ShapeDtypeStruct((n, m), BF16), pl.BlockSpec((tn, tm), lambda j, i: (j, i)))],
        n_cols=n, tm=tm, tn=tn, name="mm_headnorm_t")
    return out


def _key_epilogue(accs, extra_refs, out_refs):
    (acc,), (gain_ref,), (k_ref, mean_ref) = accs, extra_refs, out_refs
    tm, tn = acc.shape
    for c in range(tn // HEAD_DIM):
        cols = slice(c * HEAD_DIM, (c + 1) * HEAD_DIM)
        t = _head_rms_cols(acc, gain_ref, c)
        k_ref[:, cols] = t.astype(k_ref.dtype)
        for r in range(tm // MOBA_BLOCK):
            blk = t[r * MOBA_BLOCK:(r + 1) * MOBA_BLOCK, :]
            mean_ref[r:r + 1, cols] = jnp.mean(blk, axis=0, keepdims=True)


def _transpose_epilogue(accs, extra_refs, out_refs):
    out_refs[0][...] = accs[0].T.astype(out_refs[0].dtype)


def _mm_shared_kv(a, w_kv, gain_row):
    m = a.shape[0]
    d = w_kv.shape[1] // 2
    w = w_kv[None]
    tm, tn = PROJ_TM, PROJ_TN
    rows = tm // MOBA_BLOCK
    kn, kmean = _wmatmul(
        _key_epilogue, a, [(w, 0, 0)], [(gain_row, _gain_spec(tn))],
        [(jax.ShapeDtypeStruct((m, d), BF16), pl.BlockSpec((tm, tn), lambda j, i: (i, j))),
         (jax.ShapeDtypeStruct((m // tm, rows, d), F32),
          pl.BlockSpec((None, rows, tn), lambda j, i: (i, 0, j)))],
        n_cols=d, tm=tm, tn=tn, name="mm_keys")
    assert tm == MOBA_TILE_BLOCKS * MOBA_BLOCK
    (vt,) = _wmatmul(
        _transpose_epilogue, a, [(w, 0, d // tn)], [],
        [(jax.ShapeDtypeStruct((m // tm, d, tm), BF16),
          pl.BlockSpec((None, tn, tm), lambda j, i: (i, j, 0)))],
        n_cols=d, tm=tm, tn=tn, name="mm_values_t")
    return kn, kmean.reshape(m // MOBA_BLOCK, d), vt


def _residual_epilogue(accs, extra_refs, out_refs):
    out_refs[0][...] = extra_refs[0][...] + accs[0]


def _mm_residual(a, w, layer, res, tm, tn):
    m = a.shape[0]
    n = w.shape[2]
    tile = pl.BlockSpec((tm, tn), lambda j, i: (i, j))
    (out,) = _wmatmul(_residual_epilogue, a, [(w, layer, 0)], [(res, tile)],
                      [(jax.ShapeDtypeStruct((m, n), F32), tile)],
                      n_cols=n, tm=tm, tn=tn, name="mm_residual")
    return out


def _swiglu_epilogue(accs, extra_refs, out_refs):
    g, u = accs
    out_refs[0][...] = (g * (1.0 / (1.0 + jnp.exp(-g))) * u).astype(out_refs[0].dtype)


def _mm_swiglu(a, w_gate_up, layer):
    m = a.shape[0]
    tm, tn = 1024, 256
    (out,) = _wmatmul(
        _swiglu_epilogue, a, [(w_gate_up, layer, 0), (w_gate_up, layer, D_FF // tn)], [],
        [(jax.ShapeDtypeStruct((m, D_FF), BF16), pl.BlockSpec((tm, tn), lambda j, i: (i, j)))],
        n_cols=D_FF, tm=tm, tn=tn, name="mm_swiglu")
    return out


def _dilated_kernel(q0_ref, q1_ref, q2_ref, k_ref, v_ref, bias_ref, o_ref, *scratch):
    og, mg, lg = scratch[0:3], scratch[3:6], scratch[6:9]
    u = pl.program_id(1)
    q_refs = (q0_ref, q1_ref, q2_ref)

    for g, (_, d) in enumerate(DILATED_PATTERNS):
        shift = d.bit_length() - 1

        def block(b, g=g, d=d, shift=shift):
            r = jnp.bitwise_and(b, d - 1)
            nl = jnp.right_shift(b, shift)
            rel = r + (BAND * d) * nl
            rows = pl.ds(rel, BAND, stride=d) if d > 1 else pl.ds(rel, BAND)
            q = q_refs[g][rows, :].astype(BF16)
            first = jnp.logical_and(u == 0, nl == 0)
            ks = jnp.where(first, r, u * UNIT + rel - BAND * d)
            krows = pl.ds(ks, 2 * BAND, stride=d) if d > 1 else pl.ds(ks, 2 * BAND)
            k = k_ref[krows, :].astype(BF16)
            v = v_ref[krows, :].astype(BF16)
            s = lax.dot_general(q, k, (((1,), (1,)), ((), ())), preferred_element_type=F32)
            s = s + bias_ref[g, first.astype(jnp.int32)]
            m = jnp.max(s, axis=1, keepdims=True)
            p = jnp.exp2(s - m)
            l = jnp.sum(p, axis=1, keepdims=True)
            o = jnp.dot(p.astype(BF16), v, preferred_element_type=F32)
            og[g][rows, :] = o
            mg[g][rows, :] = jnp.broadcast_to(m, (BAND, HEAD_DIM))
            lg[g][rows, :] = jnp.broadcast_to(l, (BAND, HEAD_DIM))

        def blocks(it, carry, block=block):
            for c in range(DIL_UNROLL):
                block(it * DIL_UNROLL + c)
            return carry

        lax.fori_loop(0, BLOCKS_PER_UNIT // DIL_UNROLL, blocks, 0)

    chunk = 256

    def merge(c, carry):
        rows = pl.ds(pl.multiple_of(c * chunk, chunk), chunk)
        ms = [mg[g][rows, :] for g in range(N_GROUPS)]
        top = jnp.maximum(jnp.maximum(ms[0], ms[1]), ms[2])
        num = jnp.zeros((chunk, HEAD_DIM), F32)
        den = jnp.zeros((chunk, HEAD_DIM), F32)
        for g in range(N_GROUPS):
            w = jnp.exp2(ms[g] - top)
            num = num + w * og[g][rows, :]
            den = den + w * lg[g][rows, :]
        o_ref[rows, :] = (num / den).astype(o_ref.dtype)
        return carry

    lax.fori_loop(0, UNIT // chunk, merge, 0)


def _dilated_attention(qkv, bias):
    s = qkv.shape[0]
    assert s % UNIT == 0
    hcols = N_HEADS
    q_specs = [pl.BlockSpec((UNIT, HEAD_DIM), lambda h, u, g=g: (u, g * hcols + h))
               for g in range(N_GROUPS)]
    k_spec = pl.BlockSpec((s, HEAD_DIM), lambda h, u: (0, N_GROUPS * hcols + h))
    v_spec = pl.BlockSpec((s, HEAD_DIM), lambda h, u: (0, (N_GROUPS + 1) * hcols + h))
    b_spec = pl.BlockSpec((None, N_GROUPS, 2, BAND, 2 * BAND), lambda h, u: (h, 0, 0, 0, 0))
    blk = UNIT * HEAD_DIM * 4
    vmem = 2 * (3 * blk + 2 * s * HEAD_DIM * 4 + N_GROUPS * 2 * BAND * 2 * BAND * 4 + blk // 2) \
        + 9 * blk + 8 * 2**20
    return pl.pallas_call(
        _dilated_kernel,
        out_shape=jax.ShapeDtypeStruct((s, D_MODEL), BF16),
        grid=(N_HEADS, s // UNIT),
        in_specs=q_specs + [k_spec, v_spec, b_spec],
        out_specs=pl.BlockSpec((UNIT, HEAD_DIM), lambda h, u: (u, h)),
        scratch_shapes=[pltpu.VMEM((UNIT, HEAD_DIM), F32)] * 9,
        compiler_params=_params(("parallel", "arbitrary"), vmem),
        name="dilated_attention",
    )(qkv, qkv, qkv, qkv, qkv, bias)


def _moba_gate_kernel(qt_ref, kmean_ref, pen_ref):
    nb = kmean_ref.shape[0]
    width = qt_ref.shape[1]
    gate = jnp.dot(kmean_ref[...].astype(BF16), qt_ref[...], preferred_element_type=F32)
    blk = lax.broadcasted_iota(jnp.int32, gate.shape, 0)
    blk_f = blk.astype(F32)
    t0 = pl.program_id(1) * width
    own = (t0 + lax.broadcasted_iota(jnp.int32, gate.shape, 1)) // MOBA_BLOCK
    past = blk < own
    gm = jnp.where(past, gate, NEG)
    picked = jnp.zeros(gate.shape, jnp.bool_)
    for _ in range(MOBA_TOPK):
        mx = jnp.max(gm, axis=0, keepdims=True)
        first = jnp.min(jnp.where(gm == mx, blk_f, float(LANES)), axis=0, keepdims=True)
        pick = blk_f == first
        picked = jnp.logical_or(picked, pick)
        gm = jnp.where(pick, -jnp.inf, gm)
    allowed = jnp.logical_or(jnp.logical_and(picked, past), blk == own)
    pen_ref[0:nb, :] = jnp.where(allowed, 0.0, NEG).astype(pen_ref.dtype)
    pen_ref[nb:, :] = jnp.zeros((pen_ref.shape[0] - nb, width), pen_ref.dtype)


def _moba_gate(qt, kmean):
    d, s = qt.shape
    nb = s // MOBA_BLOCK
    width = 2048
    assert s % width == 0 and nb <= LANES
    return pl.pallas_call(
        _moba_gate_kernel,
        out_shape=jax.ShapeDtypeStruct((N_HEADS * LANES, s), BF16),
        grid=(N_HEADS, s // width),
        in_specs=[pl.BlockSpec((HEAD_DIM, width), lambda h, c: (h, c)),
                  pl.BlockSpec((nb, HEAD_DIM), lambda h, c: (0, h))],
        out_specs=pl.BlockSpec((LANES, width), lambda h, c: (h, c)),
        compiler_params=_params(("parallel", "parallel"), 16 * 2**20),
        name="moba_gate",
    )(qt, kmean)


def _moba_kernel(qt_ref, pen_ref, qt_next_ref, pen_next_ref, k_ref, vt_ref, blk_ref, bias_ref,
                 o_ref, s0_ref, s1_ref, p0_ref, p1_ref):
    i = pl.program_id(1)
    tile = MOBA_TILE_BLOCKS * MOBA_BLOCK
    last_tile = vt_ref.shape[0] - 1
    qx = jnp.concatenate([qt_ref[...], pen_ref[...]], axis=0)
    qx_next = jnp.concatenate([qt_next_ref[...], pen_next_ref[...]], axis=0)

    def scores(t, s_ref, qx):
        t = jnp.minimum(t, last_tile)
        rows = pl.ds(pl.multiple_of(t * tile, tile), tile)
        kx = jnp.concatenate([k_ref[rows, :], blk_ref[rows, :]], axis=1)
        s_ref[...] = jnp.dot(kx, qx, preferred_element_type=F32)

    def value_dot(t, p_ref):
        return jnp.dot(vt_ref[jnp.clip(t, 0, last_tile)], p_ref[...], preferred_element_type=F32)

    def update(t, s_ref, p_ref, p_prev_ref, carry, with_bias):
        m, l, acc, alpha_prev = carry
        pv = value_dot(t - 1, p_prev_ref)
        if with_bias:
            offs = [jnp.clip(i - (t * MOBA_TILE_BLOCKS + c), 0, N_BIAS_TILES - 1)
                    for c in range(MOBA_TILE_BLOCKS)]
            s_ref[...] += jnp.concatenate([bias_ref[o] for o in offs], axis=0)
        m_new = jnp.maximum(m, jnp.max(s_ref[...], axis=0, keepdims=True))
        alpha = jnp.exp2(m - m_new)
        p = jnp.exp2(s_ref[...] - m_new)
        l = alpha * l + jnp.sum(p, axis=0, keepdims=True)
        p_ref[...] = p.astype(BF16)
        return m_new, l, alpha_prev * acc + pv, alpha

    n_tiles = (i + MOBA_TILE_BLOCKS) // MOBA_TILE_BLOCKS
    n_far_pairs = jnp.maximum(i - (N_BIAS_TILES - 2), 0) // (2 * MOBA_TILE_BLOCKS)
    n_pairs = (n_tiles + 1) // 2

    def tile_pair(u, carry, with_bias):
        t0 = 2 * u
        scores(t0 + 1, s1_ref, qx)
        carry = update(t0, s0_ref, p0_ref, p1_ref, carry, with_bias)
        if with_bias:
            is_last = u == n_pairs - 1
            scores(jnp.where(is_last, 0, t0 + 2), s0_ref, jnp.where(is_last, qx_next, qx))
        else:
            scores(t0 + 2, s0_ref, qx)
        return update(t0 + 1, s1_ref, p1_ref, p0_ref, carry, with_bias)

    @pl.when(i == 0)
    def _():
        scores(0, s0_ref, qx)

    p1_ref[...] = jnp.zeros(p1_ref.shape, p1_ref.dtype)
    carry = (jnp.full((1, MOBA_BLOCK), NEG, F32), jnp.zeros((1, MOBA_BLOCK), F32),
             jnp.zeros((HEAD_DIM, MOBA_BLOCK), F32), jnp.ones((1, MOBA_BLOCK), F32))
    carry = lax.fori_loop(0, n_far_pairs, functools.partial(tile_pair, with_bias=False), carry)
    _, l, acc, alpha = lax.fori_loop(n_far_pairs, n_pairs,
                                     functools.partial(tile_pair, with_bias=True), carry)
    acc = alpha * acc + value_dot(2 * n_pairs - 1, p1_ref)
    o_ref[...] = (acc / l).T.astype(o_ref.dtype)


def _moba_attention(qt, pen, k, vt, bias):
    s = k.shape[0]
    nb = s // MOBA_BLOCK
    tile = MOBA_TILE_BLOCKS * MOBA_BLOCK
    assert nb % (2 * MOBA_TILE_BLOCKS) == 0 and nb <= LANES and vt.shape == (s // tile, D_MODEL, tile)
    key_block = np.arange(s)[:, None] // MOBA_BLOCK == np.arange(LANES)[None, :]
    blk_onehot = jnp.asarray(key_block, BF16)
    kv_blk = s * HEAD_DIM * 2
    vmem = 2 * (3 * kv_blk + N_BIAS_TILES * MOBA_BLOCK * MOBA_BLOCK * 4) + 24 * 2**20
    cur = lambda h, i: (h, i)
    nxt = lambda h, i: (h, jnp.minimum(i + 1, nb - 1))
    return pl.pallas_call(
        _moba_kernel,
        out_shape=jax.ShapeDtypeStruct((s, D_MODEL), BF16),
        grid=(N_HEADS, nb),
        in_specs=[pl.BlockSpec((HEAD_DIM, MOBA_BLOCK), cur),
                  pl.BlockSpec((LANES, MOBA_BLOCK), cur),
                  pl.BlockSpec((HEAD_DIM, MOBA_BLOCK), nxt),
                  pl.BlockSpec((LANES, MOBA_BLOCK), nxt),
                  pl.BlockSpec((s, HEAD_DIM), lambda h, i: (0, h)),
                  pl.BlockSpec((s // tile, HEAD_DIM, tile), lambda h, i: (0, h, 0)),
                  pl.BlockSpec((s, LANES), lambda h, i: (0, 0)),
                  pl.BlockSpec((None, N_BIAS_TILES, MOBA_BLOCK, MOBA_BLOCK),
                               lambda h, i: (h, 0, 0, 0))],
        out_specs=pl.BlockSpec((MOBA_BLOCK, HEAD_DIM), lambda h, i: (i, h)),
        scratch_shapes=[pltpu.VMEM((tile, MOBA_BLOCK), F32)] * 2
        + [pltpu.VMEM((tile, MOBA_BLOCK), BF16)] * 2,
        compiler_params=_params(("parallel", "arbitrary"), vmem),
        name="moba_attention",
    )(qt, pen, qt, pen, k, vt, blk_onehot, bias)


def _ffn(x, norm_gain, w_gate_up, w_down, layer):
    (h,) = _rmsnorm(x, norm_gain[None, :])
    act = _mm_swiglu(h, w_gate_up, layer)
    return _mm_residual(act, w_down[layer].astype(BF16)[None], 0, x, tm=512, tn=512)


def kernel(x, rel_bias, attn_norm, ffn_norm, w_qkv_a, q_norm_a, k_norm_a, w_o_a, kv_norm, w_kv_b, k_norm_b, w_q_b, q_norm_b, w_o_b, w_gate_up, w_down):
    assert x.shape[0] == 1 and x.shape[2] == D_MODEL
    xs = x[0].astype(F32)
    d = D_MODEL

    table = rel_bias.astype(F32) * LOG2E
    bias_dil = _bias_expand(_dilated_bucket_tiles(), table)
    bias_moba = _bias_expand(_moba_bucket_tiles(), table - table[N_BUCKETS - 1:, :])

    (h,) = _rmsnorm(xs, attn_norm[0][None, :])
    gain_a = jnp.concatenate(
        [jnp.tile(q_norm_a[0][g].astype(F32) * (SCALE * LOG2E), N_HEADS) for g in range(N_GROUPS)]
        + [jnp.tile(k_norm_a[0].astype(F32), N_HEADS), jnp.ones((d,), F32)])[None, :]
    qkv = _mm_headnorm(h, w_qkv_a, 0, gain_a, (N_GROUPS + 1) * d, F32)
    o = _dilated_attention(qkv, bias_dil)
    xs = _mm_residual(o, w_o_a, 0, xs, tm=PROJ_TM, tn=PROJ_TN)
    xs = _ffn(xs, ffn_norm[0], w_gate_up, w_down, 0)

    hk, hq = _rmsnorm(xs, jnp.stack([kv_norm, attn_norm[1]]))
    gain_k = jnp.tile(k_norm_b.astype(F32), N_HEADS)[None, :]
    k, kmean, vt = _mm_shared_kv(hk, w_kv_b, gain_k)
    gain_q = jnp.tile(q_norm_b[0].astype(F32) * (SCALE * LOG2E), N_HEADS)[None, :]
    qt = _mm_headnorm_t(hq, w_q_b, 0, gain_q)
    o = _moba_attention(qt, _moba_gate(qt, kmean), k, vt, bias_moba)
    xs = _mm_residual(o, w_o_b, 0, xs, tm=PROJ_TM, tn=PROJ_TN)
    xs = _ffn(xs, ffn_norm[1], w_gate_up, w_down, 1)
    return xs[None].astype(x.dtype)
```

```python
import functools
import math

import numpy as np
import jax
import jax.numpy as jnp
from jax import lax
from jax.experimental import pallas as pl
from jax.experimental.pallas import tpu as pltpu

D_MODEL = 4096
HEAD_DIM = 128
N_HEADS = D_MODEL // HEAD_DIM
DILATED_PATTERNS = ((128, 1), (512, 4), (2048, 16))
N_GROUPS = len(DILATED_PATTERNS)
BAND = 128
MOBA_BLOCK = 256
MOBA_TOPK = 3
N_BUCKETS = 32
MAX_DISTANCE = 2048
D_FF = 11008
EPS = 1e-6
NEG = -1e30
SCALE = HEAD_DIM ** -0.5
LOG2E = math.log2(math.e)

UNIT = 2048
BLOCKS_PER_UNIT = UNIT // BAND
DIL_UNROLL = 16
N_BIAS_TILES = 8
MOBA_TILE_BLOCKS = 4
MOBA_QUERY_BLOCKS = 2

V7X_SCOPED_VMEM_BYTES = 60000 * 1024
LANES = 128

F32 = jnp.float32
BF16 = jnp.bfloat16


def _params(semantics, vmem_bytes):
    return pltpu.CompilerParams(dimension_semantics=semantics,
                                vmem_limit_bytes=min(int(vmem_bytes), V7X_SCOPED_VMEM_BYTES))


def _rel_bucket_np(dist):
    n = np.maximum(dist, 0)
    exact = N_BUCKETS // 2
    nf = np.maximum(n, exact).astype(np.float64)
    large = exact + (np.log(nf / exact) / math.log(MAX_DISTANCE / exact)
                     * (N_BUCKETS - exact)).astype(np.int64)
    return np.where(n < exact, n, np.minimum(large, N_BUCKETS - 1)).astype(np.int32)


MASKED = N_BUCKETS


def _dilated_bucket_tiles():
    i = np.arange(BAND)[:, None]
    j = np.arange(2 * BAND)[None, :]
    tiles = []
    for window, d in DILATED_PATTERNS:
        w_sub = window // d
        per = []
        for first in (False, True):
            dist = (i - j) if first else (BAND + i - j)
            ok = (dist >= 0) & (dist <= w_sub)
            per.append(np.where(ok, _rel_bucket_np(dist * d), MASKED))
        tiles.append(np.stack(per))
    return np.stack(tiles).astype(np.int32)


def _moba_bucket_tiles():
    i = np.arange(MOBA_BLOCK)[None, :]
    j = np.arange(MOBA_BLOCK)[:, None]
    tiles = []
    for delta in range(N_BIAS_TILES - 1):
        dist = delta * MOBA_BLOCK + i - j
        b = _rel_bucket_np(dist)
        tiles.append(np.where(dist >= 0, b, MASKED) if delta == 0 else b)
    far = (N_BIAS_TILES - 2) * MOBA_BLOCK + 1
    assert _rel_bucket_np(np.array([far]))[0] == N_BUCKETS - 1
    tiles.append(np.full((MOBA_BLOCK, MOBA_BLOCK), N_BUCKETS - 1))
    return np.stack(tiles).astype(np.int32)


TABLE_ROWS = 64


def _bias_expand_kernel(idx_ref, tab_ref, o_ref):
    idx = idx_ref[...]
    rows = lax.broadcasted_iota(jnp.int32, (TABLE_ROWS, idx.shape[1]), 0)
    onehot = (rows == idx).astype(BF16)
    tab = tab_ref[...]
    hi = tab.astype(BF16)
    rest = tab - hi.astype(F32)
    mid = rest.astype(BF16)
    lo = (rest - mid.astype(F32)).astype(BF16)
    hi, mid, lo = (jnp.dot(piece, onehot, preferred_element_type=F32) for piece in (hi, mid, lo))
    o_ref[...] = (hi + mid) + lo


def _bias_expand(idx_np, table):
    idx_flat = jnp.asarray(idx_np.ravel())
    n = idx_flat.shape[0]
    tn = 8192
    assert n % tn == 0
    tab = jnp.zeros((N_HEADS, TABLE_ROWS), F32).at[:, :N_BUCKETS].set(table.astype(F32).T)
    tab = tab.at[:, MASKED].set(NEG)
    return pl.pallas_call(
        _bias_expand_kernel,
        out_shape=jax.ShapeDtypeStruct((N_HEADS, n), F32),
        grid=(n // tn,),
        in_specs=[pl.BlockSpec((1, tn), lambda i: (0, i)),
                  pl.BlockSpec((N_HEADS, TABLE_ROWS), lambda i: (0, 0))],
        out_specs=pl.BlockSpec((N_HEADS, tn), lambda i: (0, i)),
        compiler_params=_params(("parallel",), 16 * 2**20),
        name="bias_expand",
    )(idx_flat.reshape(1, n), tab).reshape((N_HEADS,) + idx_np.shape)


def _rmsnorm_kernel(x_ref, g_ref, *o_refs):
    x = x_ref[...]
    y = x * lax.rsqrt(jnp.mean(x * x, axis=-1, keepdims=True) + EPS)
    for n, o_ref in enumerate(o_refs):
        o_ref[...] = (y * g_ref[n:n + 1, :]).astype(o_ref.dtype)


def _rmsnorm(x, gains):
    s, d = x.shape
    n_out = gains.shape[0]
    tr = 256
    outs = pl.pallas_call(
        _rmsnorm_kernel,
        out_shape=[jax.ShapeDtypeStruct((s, d), BF16)] * n_out,
        grid=(s // tr,),
        in_specs=[pl.BlockSpec((tr, d), lambda i: (i, 0)),
                  pl.BlockSpec((n_out, d), lambda i: (0, 0))],
        out_specs=[pl.BlockSpec((tr, d), lambda i: (i, 0))] * n_out,
        compiler_params=_params(("parallel",), 2 * tr * d * (4 + 2 * n_out) + 8 * 2**20),
        name="rmsnorm",
    )(x, gains.astype(F32))
    return outs


def _head_rms_cols(acc, gain_ref, c):
    t = acc[:, c * HEAD_DIM:(c + 1) * HEAD_DIM]
    t = t * lax.rsqrt(jnp.mean(t * t, axis=-1, keepdims=True) + EPS)
    return t * gain_ref[:, c * HEAD_DIM:(c + 1) * HEAD_DIM]


def _wmatmul_kernel(*refs, epilogue, n_w, n_extra, n_out, cast):
    a_ref = refs[0]
    w_refs = refs[1:1 + n_w]
    extra_refs = refs[1 + n_w:1 + n_w + n_extra]
    out_refs = refs[1 + n_w + n_extra:1 + n_w + n_extra + n_out]
    if cast:
        wb_refs = refs[1 + n_w + n_extra + n_out:]

        @pl.when(pl.program_id(1) == 0)
        def _():
            for w_ref, wb_ref in zip(w_refs, wb_refs):
                wb_ref[...] = w_ref[...].astype(BF16)
    else:
        wb_refs = w_refs
    a = a_ref[...]
    accs = [jnp.dot(a, wb_ref[...], preferred_element_type=F32) for wb_ref in wb_refs]
    epilogue(accs, extra_refs, out_refs)


def _wmatmul(epilogue, a, weights, extra, outs, *, n_cols, tm, tn, name):
    m, k = a.shape
    assert m % tm == 0 and n_cols % tn == 0
    cast = weights[0][0].dtype != BF16
    w_specs = [pl.BlockSpec((None, k, tn), lambda j, i, layer=layer, off=off: (layer, 0, j + off))
               for _, layer, off in weights]
    wsz = weights[0][0].dtype.itemsize
    blocks = tm * k * 2 + len(weights) * k * tn * wsz
    for arr, spec in list(extra) + [(o, s) for o, s in outs]:
        blocks += math.prod(d for d in spec.block_shape if d is not None) * jnp.dtype(arr.dtype).itemsize
    vmem = 2 * blocks + (len(weights) * k * tn * 2 if cast else 0) + (2 + 2 * len(weights)) * tm * tn * 4
    return pl.pallas_call(
        functools.partial(_wmatmul_kernel, epilogue=epilogue, n_w=len(weights), n_extra=len(extra),
                          n_out=len(outs), cast=cast),
        out_shape=[o for o, _ in outs],
        grid=(n_cols // tn, m // tm),
        in_specs=[pl.BlockSpec((tm, k), lambda j, i: (i, 0))] + w_specs + [s for _, s in extra],
        out_specs=[s for _, s in outs],
        scratch_shapes=[pltpu.VMEM((k, tn), BF16)] * len(weights) if cast else [],
        compiler_params=_params(("parallel", "arbitrary"), vmem),
        name=name,
    )(a, *[w for w, _, _ in weights], *[arr for arr, _ in extra])


PROJ_TM, PROJ_TN = 1024, 512


def _gain_spec(tn):
    return pl.BlockSpec((1, tn), lambda j, i: (0, j))


def _headnorm_epilogue(accs, extra_refs, out_refs, *, n_norm_blocks):
    (acc,), (gain_ref,), (o_ref,) = accs, extra_refs, out_refs
    j = pl.program_id(0)

    @pl.when(j < n_norm_blocks)
    def _():
        for c in range(acc.shape[1] // HEAD_DIM):
            t = _head_rms_cols(acc, gain_ref, c)
            o_ref[:, c * HEAD_DIM:(c + 1) * HEAD_DIM] = t.astype(o_ref.dtype)

    @pl.when(j >= n_norm_blocks)
    def _():
        o_ref[...] = acc.astype(o_ref.dtype)


def _mm_headnorm(a, w, layer, gain_row, n_norm_cols, out_dtype):
    m = a.shape[0]
    n = w.shape[2]
    tm, tn = PROJ_TM, PROJ_TN
    assert n_norm_cols % tn == 0
    (out,) = _wmatmul(
        functools.partial(_headnorm_epilogue, n_norm_blocks=n_norm_cols // tn),
        a, [(w, layer, 0)], [(gain_row, _gain_spec(tn))],
        [(jax.ShapeDtypeStruct((m, n), out_dtype), pl.BlockSpec((tm, tn), lambda j, i: (i, j)))],
        n_cols=n, tm=tm, tn=tn, name="mm_headnorm")
    return out


def _headnorm_t_epilogue(accs, extra_refs, out_refs):
    (acc,), (gain_ref,), (ot_ref,) = accs, extra_refs, out_refs
    for c in range(acc.shape[1] // HEAD_DIM):
        t = _head_rms_cols(acc, gain_ref, c)
        ot_ref[c * HEAD_DIM:(c + 1) * HEAD_DIM, :] = t.T.astype(ot_ref.dtype)


def _mm_headnorm_t(a, w, layer, gain_row):
    m = a.shape[0]
    n = w.shape[2]
    tm, tn = PROJ_TM, PROJ_TN
    (out,) = _wmatmul(
        _headnorm_t_epilogue, a, [(w, layer, 0)], [(gain_row, _gain_spec(tn))],
        [(jax.ShapeDtypeStruct((n, m), BF16), pl.BlockSpec((tn, tm), lambda j, i: (j, i)))],
        n_cols=n, tm=tm, tn=tn, name="mm_headnorm_t")
    return out


def _key_epilogue(accs, extra_refs, out_refs):
    (acc,), (gain_ref,), (kx_ref, mean_ref) = accs, extra_refs, out_refs
    tm, tn = acc.shape
    row = pl.program_id(1) * tm + lax.broadcasted_iota(jnp.int32, (tm, LANES), 0)
    onehot = (row // MOBA_BLOCK == lax.broadcasted_iota(jnp.int32, (tm, LANES), 1)).astype(kx_ref.dtype)
    for c in range(tn // HEAD_DIM):
        cols = slice(c * HEAD_DIM, (c + 1) * HEAD_DIM)
        t = _head_rms_cols(acc, gain_ref, c)
        kx_ref[:, 2 * c * HEAD_DIM:(2 * c + 1) * HEAD_DIM] = t.astype(kx_ref.dtype)
        kx_ref[:, (2 * c + 1) * HEAD_DIM:(2 * c + 2) * HEAD_DIM] = onehot
        for r in range(tm // MOBA_BLOCK):
            blk = t[r * MOBA_BLOCK:(r + 1) * MOBA_BLOCK, :]
            mean_ref[r:r + 1, cols] = jnp.mean(blk, axis=0, keepdims=True)


def _transpose_epilogue(accs, extra_refs, out_refs):
    out_refs[0][...] = accs[0].T.astype(out_refs[0].dtype)


def _mm_shared_kv(a, w_kv, gain_row):
    m = a.shape[0]
    d = w_kv.shape[1] // 2
    w = w_kv[None]
    tm, tn = PROJ_TM, PROJ_TN
    rows = tm // MOBA_BLOCK
    assert m // MOBA_BLOCK <= LANES
    kn, kmean = _wmatmul(
        _key_epilogue, a, [(w, 0, 0)], [(gain_row, _gain_spec(tn))],
        [(jax.ShapeDtypeStruct((m, 2 * d), BF16), pl.BlockSpec((tm, 2 * tn), lambda j, i: (i, j))),
         (jax.ShapeDtypeStruct((m // tm, rows, d), F32),
          pl.BlockSpec((None, rows, tn), lambda j, i: (i, 0, j)))],
        n_cols=d, tm=tm, tn=tn, name="mm_keys")
    assert tm == MOBA_TILE_BLOCKS * MOBA_BLOCK
    (vt,) = _wmatmul(
        _transpose_epilogue, a, [(w, 0, d // tn)], [],
        [(jax.ShapeDtypeStruct((m // tm, d, tm), BF16),
          pl.BlockSpec((None, tn, tm), lambda j, i: (i, j, 0)))],
        n_cols=d, tm=tm, tn=tn, name="mm_values_t")
    return kn, kmean.reshape(m // MOBA_BLOCK, d), vt


def _residual_epilogue(accs, extra_refs, out_refs):
    out_refs[0][...] = extra_refs[0][...] + accs[0]


def _mm_residual(a, w, layer, res, tm, tn):
    m = a.shape[0]
    n = w.shape[2]
    tile = pl.BlockSpec((tm, tn), lambda j, i: (i, j))
    (out,) = _wmatmul(_residual_epilogue, a, [(w, layer, 0)], [(res, tile)],
                      [(jax.ShapeDtypeStruct((m, n), F32), tile)],
                      n_cols=n, tm=tm, tn=tn, name="mm_residual")
    return out


def _swiglu_epilogue(accs, extra_refs, out_refs):
    g, u = accs
    out_refs[0][...] = (g * (1.0 / (1.0 + jnp.exp(-g))) * u).astype(out_refs[0].dtype)


def _mm_swiglu(a, w_gate_up, layer):
    m = a.shape[0]
    tm, tn = 1024, 256
    (out,) = _wmatmul(
        _swiglu_epilogue, a, [(w_gate_up, layer, 0), (w_gate_up, layer, D_FF // tn)], [],
        [(jax.ShapeDtypeStruct((m, D_FF), BF16), pl.BlockSpec((tm, tn), lambda j, i: (i, j)))],
        n_cols=D_FF, tm=tm, tn=tn, name="mm_swiglu")
    return out


def _dilated_kernel(q0_ref, q1_ref, q2_ref, k_ref, v_ref, bias_ref, o_ref, *scratch):
    og, mg, lg = scratch[0:3], scratch[3:6], scratch[6:9]
    u = pl.program_id(1)
    q_refs = (q0_ref, q1_ref, q2_ref)

    for g, (_, d) in enumerate(DILATED_PATTERNS):
        shift = d.bit_length() - 1

        def block(b, g=g, d=d, shift=shift):
            r = jnp.bitwise_and(b, d - 1)
            nl = jnp.right_shift(b, shift)
            rel = r + (BAND * d) * nl
            rows = pl.ds(rel, BAND, stride=d) if d > 1 else pl.ds(rel, BAND)
            q = q_refs[g][rows, :].astype(BF16)
            first = jnp.logical_and(u == 0, nl == 0)
            ks = jnp.where(first, r, u * UNIT + rel - BAND * d)
            krows = pl.ds(ks, 2 * BAND, stride=d) if d > 1 else pl.ds(ks, 2 * BAND)
            k = k_ref[krows, :].astype(BF16)
            v = v_ref[krows, :].astype(BF16)
            s = lax.dot_general(q, k, (((1,), (1,)), ((), ())), preferred_element_type=F32)
            s = s + bias_ref[g, first.astype(jnp.int32)]
            m = jnp.max(s, axis=1, keepdims=True)
            p = jnp.exp2(s - m)
            l = jnp.sum(p, axis=1, keepdims=True)
            o = jnp.dot(p.astype(BF16), v, preferred_element_type=F32)
            og[g][rows, :] = o
            mg[g][rows, :] = jnp.broadcast_to(m, (BAND, HEAD_DIM))
            lg[g][rows, :] = jnp.broadcast_to(l, (BAND, HEAD_DIM))

        def blocks(it, carry, block=block):
            for c in range(DIL_UNROLL):
                block(it * DIL_UNROLL + c)
            return carry

        lax.fori_loop(0, BLOCKS_PER_UNIT // DIL_UNROLL, blocks, 0)

    chunk = 256

    def merge(c, carry):
        rows = pl.ds(pl.multiple_of(c * chunk, chunk), chunk)
        ms = [mg[g][rows, :] for g in range(N_GROUPS)]
        top = jnp.maximum(jnp.maximum(ms[0], ms[1]), ms[2])
        num = jnp.zeros((chunk, HEAD_DIM), F32)
        den = jnp.zeros((chunk, HEAD_DIM), F32)
        for g in range(N_GROUPS):
            w = jnp.exp2(ms[g] - top)
            num = num + w * og[g][rows, :]
            den = den + w * lg[g][rows, :]
        o_ref[rows, :] = (num / den).astype(o_ref.dtype)
        return carry

    lax.fori_loop(0, UNIT // chunk, merge, 0)


def _dilated_attention(qkv, bias):
    s = qkv.shape[0]
    assert s % UNIT == 0
    hcols = N_HEADS
    q_specs = [pl.BlockSpec((UNIT, HEAD_DIM), lambda h, u, g=g: (u, g * hcols + h))
               for g in range(N_GROUPS)]
    k_spec = pl.BlockSpec((s, HEAD_DIM), lambda h, u: (0, N_GROUPS * hcols + h))
    v_spec = pl.BlockSpec((s, HEAD_DIM), lambda h, u: (0, (N_GROUPS + 1) * hcols + h))
    b_spec = pl.BlockSpec((None, N_GROUPS, 2, BAND, 2 * BAND), lambda h, u: (h, 0, 0, 0, 0))
    blk = UNIT * HEAD_DIM * 4
    vmem = 2 * (3 * blk + 2 * s * HEAD_DIM * 4 + N_GROUPS * 2 * BAND * 2 * BAND * 4 + blk // 2) \
        + 9 * blk + 8 * 2**20
    return pl.pallas_call(
        _dilated_kernel,
        out_shape=jax.ShapeDtypeStruct((s, D_MODEL), BF16),
        grid=(N_HEADS, s // UNIT),
        in_specs=q_specs + [k_spec, v_spec, b_spec],
        out_specs=pl.BlockSpec((UNIT, HEAD_DIM), lambda h, u: (u, h)),
        scratch_shapes=[pltpu.VMEM((UNIT, HEAD_DIM), F32)] * 9,
        compiler_params=_params(("parallel", "arbitrary"), vmem),
        name="dilated_attention",
    )(qkv, qkv, qkv, qkv, qkv, bias)


def _moba_gate_kernel(qt_ref, kmean_ref, pen_ref):
    nb = kmean_ref.shape[0]
    width = qt_ref.shape[1]
    gate = jnp.dot(kmean_ref[...].astype(BF16), qt_ref[...], preferred_element_type=F32)
    blk = lax.broadcasted_iota(jnp.int32, gate.shape, 0)
    blk_f = blk.astype(F32)
    t0 = pl.program_id(1) * width
    own = (t0 + lax.broadcasted_iota(jnp.int32, gate.shape, 1)) // MOBA_BLOCK
    past = blk < own
    gm = jnp.where(past, gate, NEG)
    picked = jnp.zeros(gate.shape, jnp.bool_)
    for _ in range(MOBA_TOPK):
        mx = jnp.max(gm, axis=0, keepdims=True)
        first = jnp.min(jnp.where(gm == mx, blk_f, float(LANES)), axis=0, keepdims=True)
        pick = blk_f == first
        picked = jnp.logical_or(picked, pick)
        gm = jnp.where(pick, -jnp.inf, gm)
    allowed = jnp.logical_or(jnp.logical_and(picked, past), blk == own)
    pen_ref[0:nb, :] = jnp.where(allowed, 0.0, NEG).astype(pen_ref.dtype)
    pen_ref[nb:, :] = jnp.zeros((pen_ref.shape[0] - nb, width), pen_ref.dtype)


def _moba_gate(qt, kmean):
    d, s = qt.shape
    nb = s // MOBA_BLOCK
    width = 2048
    assert s % width == 0 and nb <= LANES
    return pl.pallas_call(
        _moba_gate_kernel,
        out_shape=jax.ShapeDtypeStruct((N_HEADS * LANES, s), BF16),
        grid=(N_HEADS, s // width),
        in_specs=[pl.BlockSpec((HEAD_DIM, width), lambda h, c: (h, c)),
                  pl.BlockSpec((nb, HEAD_DIM), lambda h, c: (0, h))],
        out_specs=pl.BlockSpec((LANES, width), lambda h, c: (h, c)),
        compiler_params=_params(("parallel", "parallel"), 16 * 2**20),
        name="moba_gate",
    )(qt, kmean)


def _moba_kernel(qt_ref, pen_ref, qt_next_ref, pen_next_ref, kx_ref, vt_ref, bias_ref,
                 o_ref, s0_ref, s1_ref, p0_ref, p1_ref):
    step = pl.program_id(1)
    i = step * MOBA_QUERY_BLOCKS
    i_last = i + MOBA_QUERY_BLOCKS - 1
    width = MOBA_QUERY_BLOCKS * MOBA_BLOCK
    tile = MOBA_TILE_BLOCKS * MOBA_BLOCK
    last_tile = vt_ref.shape[0] - 1
    qx = jnp.concatenate([qt_ref[...], pen_ref[...]], axis=0)
    qx_next = jnp.concatenate([qt_next_ref[...], pen_next_ref[...]], axis=0)

    def scores(t, s_ref, qx):
        t = jnp.minimum(t, last_tile)
        rows = pl.ds(pl.multiple_of(t * tile, tile), tile)
        s_ref[...] = jnp.dot(kx_ref[rows, :], qx, preferred_element_type=F32)

    def value_dot(t, p_ref):
        return jnp.dot(vt_ref[jnp.clip(t, 0, last_tile)], p_ref[...], preferred_element_type=F32)

    def update(t, s_ref, p_ref, p_prev_ref, carry, with_bias):
        m, l, acc, alpha_prev = carry
        pv = value_dot(t - 1, p_prev_ref)
        if with_bias:
            def bias_tile(c, a):
                return bias_ref[jnp.clip(i + a - (t * MOBA_TILE_BLOCKS + c), 0, N_BIAS_TILES - 1)]
            s_ref[...] += jnp.concatenate(
                [jnp.concatenate([bias_tile(c, a) for a in range(MOBA_QUERY_BLOCKS)], axis=1)
                 for c in range(MOBA_TILE_BLOCKS)], axis=0)
        m_new = jnp.maximum(m, jnp.max(s_ref[...], axis=0, keepdims=True))
        alpha = jnp.exp2(m - m_new)
        p = jnp.exp2(s_ref[...] - m_new)
        l = alpha * l + jnp.sum(p, axis=0, keepdims=True)
        p_ref[...] = p.astype(BF16)
        return m_new, l, alpha_prev * acc + pv, alpha

    n_tiles = (i_last + MOBA_TILE_BLOCKS) // MOBA_TILE_BLOCKS
    n_far_pairs = jnp.maximum(i - (N_BIAS_TILES - 2), 0) // (2 * MOBA_TILE_BLOCKS)
    n_pairs = (n_tiles + 1) // 2

    def tile_pair(u, carry, with_bias):
        t0 = 2 * u
        scores(t0 + 1, s1_ref, qx)
        carry = update(t0, s0_ref, p0_ref, p1_ref, carry, with_bias)
        if with_bias:
            is_last = u == n_pairs - 1
            scores(jnp.where(is_last, 0, t0 + 2), s0_ref, jnp.where(is_last, qx_next, qx))
        else:
            scores(t0 + 2, s0_ref, qx)
        return update(t0 + 1, s1_ref, p1_ref, p0_ref, carry, with_bias)

    @pl.when(step == 0)
    def _():
        scores(0, s0_ref, qx)

    p1_ref[...] = jnp.zeros(p1_ref.shape, p1_ref.dtype)
    carry = (jnp.full((1, width), NEG, F32), jnp.zeros((1, width), F32),
             jnp.zeros((HEAD_DIM, width), F32), jnp.ones((1, width), F32))
    carry = lax.fori_loop(0, n_far_pairs, functools.partial(tile_pair, with_bias=False), carry)
    _, l, acc, alpha = lax.fori_loop(n_far_pairs, n_pairs,
                                     functools.partial(tile_pair, with_bias=True), carry)
    acc = alpha * acc + value_dot(2 * n_pairs - 1, p1_ref)
    o_ref[...] = (acc / l).T.astype(o_ref.dtype)


def _moba_attention(qt, pen, kx, vt, bias):
    s = kx.shape[0]
    nb = s // MOBA_BLOCK
    tile = MOBA_TILE_BLOCKS * MOBA_BLOCK
    assert nb % (2 * MOBA_TILE_BLOCKS) == 0 and nb <= LANES and vt.shape == (s // tile, D_MODEL, tile)
    kv_blk = s * HEAD_DIM * 2
    width = MOBA_QUERY_BLOCKS * MOBA_BLOCK
    n_steps = nb // MOBA_QUERY_BLOCKS
    assert nb % MOBA_QUERY_BLOCKS == 0
    vmem = 2 * (3 * kv_blk + N_BIAS_TILES * MOBA_BLOCK * MOBA_BLOCK * 4) + 12 * tile * width + 24 * 2**20
    cur = lambda h, i: (h, i)
    nxt = lambda h, i: (h, jnp.minimum(i + 1, n_steps - 1))
    return pl.pallas_call(
        _moba_kernel,
        out_shape=jax.ShapeDtypeStruct((s, D_MODEL), BF16),
        grid=(N_HEADS, n_steps),
        in_specs=[pl.BlockSpec((HEAD_DIM, width), cur),
                  pl.BlockSpec((LANES, width), cur),
                  pl.BlockSpec((HEAD_DIM, width), nxt),
                  pl.BlockSpec((LANES, width), nxt),
                  pl.BlockSpec((s, 2 * HEAD_DIM), lambda h, i: (0, h)),
                  pl.BlockSpec((s // tile, HEAD_DIM, tile), lambda h, i: (0, h, 0)),
                  pl.BlockSpec((None, N_BIAS_TILES, MOBA_BLOCK, MOBA_BLOCK),
                               lambda h, i: (h, 0, 0, 0))],
        out_specs=pl.BlockSpec((width, HEAD_DIM), lambda h, i: (i, h)),
        scratch_shapes=[pltpu.VMEM((tile, width), F32)] * 2
        + [pltpu.VMEM((tile, width), BF16)] * 2,
        compiler_params=_params(("parallel", "arbitrary"), vmem),
        name="moba_attention",
    )(qt, pen, qt, pen, kx, vt, bias)


def _ffn(x, norm_gain, w_gate_up, w_down, layer):
    (h,) = _rmsnorm(x, norm_gain[None, :])
    act = _mm_swiglu(h, w_gate_up, layer)
    return _mm_residual(act, w_down[layer].astype(BF16)[None], 0, x, tm=512, tn=512)


def kernel(x, rel_bias, attn_norm, ffn_norm, w_qkv_a, q_norm_a, k_norm_a, w_o_a, kv_norm, w_kv_b, k_norm_b, w_q_b, q_norm_b, w_o_b, w_gate_up, w_down):
    assert x.shape[0] == 1 and x.shape[2] == D_MODEL
    xs = x[0].astype(F32)
    d = D_MODEL

    table = rel_bias.astype(F32) * LOG2E
    bias_dil = _bias_expand(_dilated_bucket_tiles(), table)
    bias_moba = _bias_expand(_moba_bucket_tiles(), table - table[N_BUCKETS - 1:, :])

    (h,) = _rmsnorm(xs, attn_norm[0][None, :])
    gain_a = jnp.concatenate(
        [jnp.tile(q_norm_a[0][g].astype(F32) * (SCALE * LOG2E), N_HEADS) for g in range(N_GROUPS)]
        + [jnp.tile(k_norm_a[0].astype(F32), N_HEADS), jnp.ones((d,), F32)])[None, :]
    qkv = _mm_headnorm(h, w_qkv_a, 0, gain_a, (N_GROUPS + 1) * d, F32)
    o = _dilated_attention(qkv, bias_dil)
    xs = _mm_residual(o, w_o_a, 0, xs, tm=PROJ_TM, tn=PROJ_TN)
    xs = _ffn(xs, ffn_norm[0], w_gate_up, w_down, 0)

    hk, hq = _rmsnorm(xs, jnp.stack([kv_norm, attn_norm[1]]))
    gain_k = jnp.tile(k_norm_b.astype(F32), N_HEADS)[None, :]
    k, kmean, vt = _mm_shared_kv(hk, w_kv_b, gain_k)
    gain_q = jnp.tile(q_norm_b[0].astype(F32) * (SCALE * LOG2E), N_HEADS)[None, :]
    qt = _mm_headnorm_t(hq, w_q_b, 0, gain_q)
    o = _moba_attention(qt, _moba_gate(qt, kmean), k, vt, bias_moba)
    xs = _mm_residual(o, w_o_b, 0, xs, tm=PROJ_TM, tn=PROJ_TN)
    xs = _ffn(xs, ffn_norm[1], w_gate_up, w_down, 1)
    return xs[None].astype(x.dtype)
```

```python
import functools
import math

import numpy as np
import jax
import jax.numpy as jnp
from jax import lax
from jax.experimental import pallas as pl
from jax.experimental.pallas import tpu as pltpu

D_MODEL = 4096
HEAD_DIM = 128
N_HEADS = D_MODEL // HEAD_DIM
DILATED_PATTERNS = ((128, 1), (512, 4), (2048, 16))
N_GROUPS = len(DILATED_PATTERNS)
BAND = 128
MOBA_BLOCK = 256
MOBA_TOPK = 3
N_BUCKETS = 32
MAX_DISTANCE = 2048
D_FF = 11008
EPS = 1e-6
NEG = -1e30
SCALE = HEAD_DIM ** -0.5
LOG2E = math.log2(math.e)

UNIT = 2048
BLOCKS_PER_UNIT = UNIT // BAND
DIL_UNROLL = 16
N_BIAS_TILES = 8
MOBA_TILE_BLOCKS = 4
MOBA_QUERY_BLOCKS = 2

V7X_SCOPED_VMEM_BYTES = 60000 * 1024
LANES = 128

F32 = jnp.float32
BF16 = jnp.bfloat16


def _params(semantics, vmem_bytes):
    return pltpu.CompilerParams(dimension_semantics=semantics,
                                vmem_limit_bytes=min(int(vmem_bytes), V7X_SCOPED_VMEM_BYTES))


def _rel_bucket_np(dist):
    n = np.maximum(dist, 0)
    exact = N_BUCKETS // 2
    nf = np.maximum(n, exact).astype(np.float64)
    large = exact + (np.log(nf / exact) / math.log(MAX_DISTANCE / exact)
                     * (N_BUCKETS - exact)).astype(np.int64)
    return np.where(n < exact, n, np.minimum(large, N_BUCKETS - 1)).astype(np.int32)


MASKED = N_BUCKETS


def _dilated_bucket_tiles():
    i = np.arange(BAND)[:, None]
    j = np.arange(2 * BAND)[None, :]
    tiles = []
    for window, d in DILATED_PATTERNS:
        w_sub = window // d
        per = []
        for first in (False, True):
            dist = (i - j) if first else (BAND + i - j)
            ok = (dist >= 0) & (dist <= w_sub)
            per.append(np.where(ok, _rel_bucket_np(dist * d), MASKED))
        tiles.append(np.stack(per))
    return np.stack(tiles).astype(np.int32)


def _moba_bucket_tiles():
    i = np.arange(MOBA_BLOCK)[None, :]
    j = np.arange(MOBA_BLOCK)[:, None]
    tiles = []
    for delta in range(N_BIAS_TILES - 1):
        dist = delta * MOBA_BLOCK + i - j
        b = _rel_bucket_np(dist)
        tiles.append(np.where(dist >= 0, b, MASKED) if delta == 0 else b)
    far = (N_BIAS_TILES - 2) * MOBA_BLOCK + 1
    assert _rel_bucket_np(np.array([far]))[0] == N_BUCKETS - 1
    tiles.append(np.full((MOBA_BLOCK, MOBA_BLOCK), N_BUCKETS - 1))
    return np.stack(tiles).astype(np.int32)


TABLE_ROWS = 64


def _bias_expand_kernel(idx_ref, tab_ref, o_ref):
    idx = idx_ref[...]
    rows = lax.broadcasted_iota(jnp.int32, (TABLE_ROWS, idx.shape[1]), 0)
    onehot = (rows == idx).astype(BF16)
    tab = tab_ref[...]
    hi = tab.astype(BF16)
    rest = tab - hi.astype(F32)
    mid = rest.astype(BF16)
    lo = (rest - mid.astype(F32)).astype(BF16)
    hi, mid, lo = (jnp.dot(piece, onehot, preferred_element_type=F32) for piece in (hi, mid, lo))
    o_ref[...] = (hi + mid) + lo


def _bias_expand(idx_np, table):
    idx_flat = jnp.asarray(idx_np.ravel())
    n = idx_flat.shape[0]
    tn = 8192
    assert n % tn == 0
    tab = jnp.zeros((N_HEADS, TABLE_ROWS), F32).at[:, :N_BUCKETS].set(table.astype(F32).T)
    tab = tab.at[:, MASKED].set(NEG)
    return pl.pallas_call(
        _bias_expand_kernel,
        out_shape=jax.ShapeDtypeStruct((N_HEADS, n), F32),
        grid=(n // tn,),
        in_specs=[pl.BlockSpec((1, tn), lambda i: (0, i)),
                  pl.BlockSpec((N_HEADS, TABLE_ROWS), lambda i: (0, 0))],
        out_specs=pl.BlockSpec((N_HEADS, tn), lambda i: (0, i)),
        compiler_params=_params(("parallel",), 16 * 2**20),
        name="bias_expand",
    )(idx_flat.reshape(1, n), tab).reshape((N_HEADS,) + idx_np.shape)


def _rmsnorm_kernel(x_ref, g_ref, *o_refs):
    x = x_ref[...]
    y = x * lax.rsqrt(jnp.mean(x * x, axis=-1, keepdims=True) + EPS)
    for n, o_ref in enumerate(o_refs):
        o_ref[...] = (y * g_ref[n:n + 1, :]).astype(o_ref.dtype)


def _rmsnorm(x, gains):
    s, d = x.shape
    n_out = gains.shape[0]
    tr = 256
    outs = pl.pallas_call(
        _rmsnorm_kernel,
        out_shape=[jax.ShapeDtypeStruct((s, d), BF16)] * n_out,
        grid=(s // tr,),
        in_specs=[pl.BlockSpec((tr, d), lambda i: (i, 0)),
                  pl.BlockSpec((n_out, d), lambda i: (0, 0))],
        out_specs=[pl.BlockSpec((tr, d), lambda i: (i, 0))] * n_out,
        compiler_params=_params(("parallel",), 2 * tr * d * (4 + 2 * n_out) + 8 * 2**20),
        name="rmsnorm",
    )(x, gains.astype(F32))
    return outs


def _head_rms_cols(acc, gain_ref, c):
    t = acc[:, c * HEAD_DIM:(c + 1) * HEAD_DIM]
    t = t * lax.rsqrt(jnp.mean(t * t, axis=-1, keepdims=True) + EPS)
    return t * gain_ref[:, c * HEAD_DIM:(c + 1) * HEAD_DIM]


def _wmatmul_kernel(*refs, epilogue, n_w, n_extra, n_out, cast):
    a_ref = refs[0]
    w_refs = refs[1:1 + n_w]
    extra_refs = refs[1 + n_w:1 + n_w + n_extra]
    out_refs = refs[1 + n_w + n_extra:1 + n_w + n_extra + n_out]
    if cast:
        wb_refs = refs[1 + n_w + n_extra + n_out:]

        @pl.when(pl.program_id(1) == 0)
        def _():
            for w_ref, wb_ref in zip(w_refs, wb_refs):
                wb_ref[...] = w_ref[...].astype(BF16)
    else:
        wb_refs = w_refs
    half = a_ref.shape[0] // MM_ROW_SPLIT
    for h in range(MM_ROW_SPLIT):
        rows = slice(h * half, (h + 1) * half)
        a = a_ref[rows, :]
        accs = [jnp.dot(a, wb_ref[...], preferred_element_type=F32) for wb_ref in wb_refs]
        epilogue(accs, extra_refs, out_refs, rows)


def _wmatmul(epilogue, a, weights, extra, outs, *, n_cols, tm, tn, name):
    m, k = a.shape
    assert m % tm == 0 and n_cols % tn == 0
    cast = weights[0][0].dtype != BF16
    w_specs = [pl.BlockSpec((None, k, tn), lambda j, i, layer=layer, off=off: (layer, 0, j + off))
               for _, layer, off in weights]
    wsz = weights[0][0].dtype.itemsize
    blocks = tm * k * 2 + len(weights) * k * tn * wsz
    for arr, spec in list(extra) + [(o, s) for o, s in outs]:
        blocks += math.prod(d for d in spec.block_shape if d is not None) * jnp.dtype(arr.dtype).itemsize
    vmem = 2 * blocks + (len(weights) * k * tn * 2 if cast else 0) + (2 + 2 * len(weights)) * tm * tn * 4
    return pl.pallas_call(
        functools.partial(_wmatmul_kernel, epilogue=epilogue, n_w=len(weights), n_extra=len(extra),
                          n_out=len(outs), cast=cast),
        out_shape=[o for o, _ in outs],
        grid=(n_cols // tn, m // tm),
        in_specs=[pl.BlockSpec((tm, k), lambda j, i: (i, 0))] + w_specs + [s for _, s in extra],
        out_specs=[s for _, s in outs],
        scratch_shapes=[pltpu.VMEM((k, tn), BF16)] * len(weights) if cast else [],
        compiler_params=_params(("parallel", "arbitrary"), vmem),
        name=name,
    )(a, *[w for w, _, _ in weights], *[arr for arr, _ in extra])


PROJ_TM, PROJ_TN = 512, 1024
RES_TM, RES_TN = 1024, 512
MM_ROW_SPLIT = 2


def _gain_spec(tn):
    return pl.BlockSpec((1, tn), lambda j, i: (0, j))


def _headnorm_epilogue(accs, extra_refs, out_refs, rows):
    (acc,), (gain_ref,), (o_ref,) = accs, extra_refs, out_refs
    for c in range(acc.shape[1] // HEAD_DIM):
        t = _head_rms_cols(acc, gain_ref, c)
        o_ref[rows, c * HEAD_DIM:(c + 1) * HEAD_DIM] = t.astype(o_ref.dtype)


def _store_epilogue(accs, extra_refs, out_refs, rows):
    out_refs[0][rows, :] = accs[0].astype(out_refs[0].dtype)


def _mm_headnorm(a, w, layer, gain_row, n_norm_cols, out_dtype):
    m = a.shape[0]
    n = w.shape[2]
    tm, tn = PROJ_TM, PROJ_TN
    assert n_norm_cols % tn == 0
    tile = pl.BlockSpec((tm, tn), lambda j, i: (i, j))
    (normed,) = _wmatmul(
        _headnorm_epilogue, a, [(w, layer, 0)], [(gain_row, _gain_spec(tn))],
        [(jax.ShapeDtypeStruct((m, n_norm_cols), out_dtype), tile)],
        n_cols=n_norm_cols, tm=tm, tn=tn, name="mm_headnorm")
    (rest,) = _wmatmul(
        _store_epilogue, a, [(w, layer, n_norm_cols // tn)], [],
        [(jax.ShapeDtypeStruct((m, n - n_norm_cols), out_dtype), tile)],
        n_cols=n - n_norm_cols, tm=tm, tn=tn, name="mm_plain")
    return normed, rest


def _headnorm_t_epilogue(accs, extra_refs, out_refs, rows):
    (acc,), (gain_ref,), (ot_ref,) = accs, extra_refs, out_refs
    for c in range(acc.shape[1] // HEAD_DIM):
        t = _head_rms_cols(acc, gain_ref, c)
        ot_ref[c * HEAD_DIM:(c + 1) * HEAD_DIM, rows] = t.T.astype(ot_ref.dtype)


def _mm_headnorm_t(a, w, layer, gain_row):
    m = a.shape[0]
    n = w.shape[2]
    tm, tn = PROJ_TM, PROJ_TN
    (out,) = _wmatmul(
        _headnorm_t_epilogue, a, [(w, layer, 0)], [(gain_row, _gain_spec(tn))],
        [(jax.ShapeDtypeStruct((n, m), BF16), pl.BlockSpec((tn, tm), lambda j, i: (j, i)))],
        n_cols=n, tm=tm, tn=tn, name="mm_headnorm_t")
    return out


def _key_epilogue(accs, extra_refs, out_refs, rows):
    (acc,), (gain_ref,), (kx_ref, mean_ref) = accs, extra_refs, out_refs
    n_rows, tn = acc.shape
    assert rows.start % MOBA_BLOCK == 0 and n_rows % MOBA_BLOCK == 0
    row = (pl.program_id(1) * kx_ref.shape[0] + rows.start
           + lax.broadcasted_iota(jnp.int32, (n_rows, LANES), 0))
    onehot = (row // MOBA_BLOCK == lax.broadcasted_iota(jnp.int32, (n_rows, LANES), 1)
              ).astype(kx_ref.dtype)
    for c in range(tn // HEAD_DIM):
        cols = slice(c * HEAD_DIM, (c + 1) * HEAD_DIM)
        t = _head_rms_cols(acc, gain_ref, c)
        kx_ref[rows, 2 * c * HEAD_DIM:(2 * c + 1) * HEAD_DIM] = t.astype(kx_ref.dtype)
        kx_ref[rows, (2 * c + 1) * HEAD_DIM:(2 * c + 2) * HEAD_DIM] = onehot
        for r in range(n_rows // MOBA_BLOCK):
            blk = t[r * MOBA_BLOCK:(r + 1) * MOBA_BLOCK, :]
            r_out = rows.start // MOBA_BLOCK + r
            mean_ref[r_out:r_out + 1, cols] = jnp.mean(blk, axis=0, keepdims=True)


def _transpose_epilogue(accs, extra_refs, out_refs, rows):
    out_refs[0][:, rows] = accs[0].T.astype(out_refs[0].dtype)


def _mm_shared_kv(a, w_kv, gain_row):
    m = a.shape[0]
    d = w_kv.shape[1] // 2
    w = w_kv[None]
    tm, tn = PROJ_TM, PROJ_TN
    rows = tm // MOBA_BLOCK
    assert m // MOBA_BLOCK <= LANES
    kn, kmean = _wmatmul(
        _key_epilogue, a, [(w, 0, 0)], [(gain_row, _gain_spec(tn))],
        [(jax.ShapeDtypeStruct((m, 2 * d), BF16), pl.BlockSpec((tm, 2 * tn), lambda j, i: (i, j))),
         (jax.ShapeDtypeStruct((m // tm, rows, d), F32),
          pl.BlockSpec((None, rows, tn), lambda j, i: (i, 0, j)))],
        n_cols=d, tm=tm, tn=tn, name="mm_keys")
    tile = MOBA_TILE_BLOCKS * MOBA_BLOCK
    per_tile = tile // tm
    assert tile % tm == 0
    (vt,) = _wmatmul(
        _transpose_epilogue, a, [(w, 0, d // tn)], [],
        [(jax.ShapeDtypeStruct((m // tile, d, tile), BF16),
          pl.BlockSpec((None, tn, tm), lambda j, i: (i // per_tile, j, i % per_tile)))],
        n_cols=d, tm=tm, tn=tn, name="mm_values_t")
    return kn, kmean.reshape(m // MOBA_BLOCK, d), vt


def _residual_epilogue(accs, extra_refs, out_refs, rows):
    out_refs[0][rows, :] = extra_refs[0][rows, :] + accs[0]


def _mm_residual(a, w, layer, res, tm, tn):
    m = a.shape[0]
    n = w.shape[2]
    tile = pl.BlockSpec((tm, tn), lambda j, i: (i, j))
    (out,) = _wmatmul(_residual_epilogue, a, [(w, layer, 0)], [(res, tile)],
                      [(jax.ShapeDtypeStruct((m, n), F32), tile)],
                      n_cols=n, tm=tm, tn=tn, name="mm_residual")
    return out


def _swiglu_epilogue(accs, extra_refs, out_refs, rows):
    g, u = accs
    out_refs[0][rows, :] = (g * (1.0 / (1.0 + jnp.exp(-g))) * u).astype(out_refs[0].dtype)


def _mm_swiglu(a, w_gate_up, layer):
    m = a.shape[0]
    tm, tn = 1024, 256
    (out,) = _wmatmul(
        _swiglu_epilogue, a, [(w_gate_up, layer, 0), (w_gate_up, layer, D_FF // tn)], [],
        [(jax.ShapeDtypeStruct((m, D_FF), BF16), pl.BlockSpec((tm, tn), lambda j, i: (i, j)))],
        n_cols=D_FF, tm=tm, tn=tn, name="mm_swiglu")
    return out


def _dilated_kernel(q0_ref, q1_ref, q2_ref, k_ref, v_ref, bias_ref, o_ref, *scratch):
    og, mg, lg = scratch[0:3], scratch[3:6], scratch[6:9]
    u = pl.program_id(1)
    q_refs = (q0_ref, q1_ref, q2_ref)

    for g, (_, d) in enumerate(DILATED_PATTERNS):
        shift = d.bit_length() - 1

        def block(b, g=g, d=d, shift=shift):
            r = jnp.bitwise_and(b, d - 1)
            nl = jnp.right_shift(b, shift)
            rel = r + (BAND * d) * nl
            rows = pl.ds(rel, BAND, stride=d) if d > 1 else pl.ds(rel, BAND)
            q = q_refs[g][rows, :].astype(BF16)
            first = jnp.logical_and(u == 0, nl == 0)
            ks = jnp.where(first, r, u * UNIT + rel - BAND * d)
            krows = pl.ds(ks, 2 * BAND, stride=d) if d > 1 else pl.ds(ks, 2 * BAND)
            k = k_ref[krows, :].astype(BF16)
            v = v_ref[krows, :].astype(BF16)
            s = lax.dot_general(q, k, (((1,), (1,)), ((), ())), preferred_element_type=F32)
            s = s + bias_ref[g, first.astype(jnp.int32)]
            m = jnp.max(s, axis=1, keepdims=True)
            p = jnp.exp2(s - m)
            l = jnp.sum(p, axis=1, keepdims=True)
            o = jnp.dot(p.astype(BF16), v, preferred_element_type=F32)
            og[g][rows, :] = o
            mg[g][rows, :] = jnp.broadcast_to(m, (BAND, HEAD_DIM))
            lg[g][rows, :] = jnp.broadcast_to(l, (BAND, HEAD_DIM))

        def blocks(it, carry, block=block):
            for c in range(DIL_UNROLL):
                block(it * DIL_UNROLL + c)
            return carry

        lax.fori_loop(0, BLOCKS_PER_UNIT // DIL_UNROLL, blocks, 0)

    chunk = 256

    def merge(c, carry):
        rows = pl.ds(pl.multiple_of(c * chunk, chunk), chunk)
        ms = [mg[g][rows, :] for g in range(N_GROUPS)]
        top = jnp.maximum(jnp.maximum(ms[0], ms[1]), ms[2])
        num = jnp.zeros((chunk, HEAD_DIM), F32)
        den = jnp.zeros((chunk, HEAD_DIM), F32)
        for g in range(N_GROUPS):
            w = jnp.exp2(ms[g] - top)
            num = num + w * og[g][rows, :]
            den = den + w * lg[g][rows, :]
        o_ref[rows, :] = (num / den).astype(o_ref.dtype)
        return carry

    lax.fori_loop(0, UNIT // chunk, merge, 0)


def _dilated_attention(qk, v, bias):
    s = qk.shape[0]
    assert s % UNIT == 0
    hcols = N_HEADS
    q_specs = [pl.BlockSpec((UNIT, HEAD_DIM), lambda h, u, g=g: (u, g * hcols + h))
               for g in range(N_GROUPS)]
    k_spec = pl.BlockSpec((s, HEAD_DIM), lambda h, u: (0, N_GROUPS * hcols + h))
    v_spec = pl.BlockSpec((s, HEAD_DIM), lambda h, u: (0, h))
    b_spec = pl.BlockSpec((None, N_GROUPS, 2, BAND, 2 * BAND), lambda h, u: (h, 0, 0, 0, 0))
    blk = UNIT * HEAD_DIM * 4
    vmem = 2 * (3 * blk + 2 * s * HEAD_DIM * 4 + N_GROUPS * 2 * BAND * 2 * BAND * 4 + blk // 2) \
        + 9 * blk + 8 * 2**20
    return pl.pallas_call(
        _dilated_kernel,
        out_shape=jax.ShapeDtypeStruct((s, D_MODEL), BF16),
        grid=(N_HEADS, s // UNIT),
        in_specs=q_specs + [k_spec, v_spec, b_spec],
        out_specs=pl.BlockSpec((UNIT, HEAD_DIM), lambda h, u: (u, h)),
        scratch_shapes=[pltpu.VMEM((UNIT, HEAD_DIM), F32)] * 9,
        compiler_params=_params(("parallel", "arbitrary"), vmem),
        name="dilated_attention",
    )(qk, qk, qk, qk, v, bias)


def _moba_gate_kernel(qt_ref, kmean_ref, pen_ref):
    nb = kmean_ref.shape[0]
    width = qt_ref.shape[1]
    gate = jnp.dot(kmean_ref[...].astype(BF16), qt_ref[...], preferred_element_type=F32)
    blk = lax.broadcasted_iota(jnp.int32, gate.shape, 0)
    blk_f = blk.astype(F32)
    t0 = pl.program_id(1) * width
    own = (t0 + lax.broadcasted_iota(jnp.int32, gate.shape, 1)) // MOBA_BLOCK
    past = blk < own
    gm = jnp.where(past, gate, NEG)
    picked = jnp.zeros(gate.shape, jnp.bool_)
    for _ in range(MOBA_TOPK):
        mx = jnp.max(gm, axis=0, keepdims=True)
        first = jnp.min(jnp.where(gm == mx, blk_f, float(LANES)), axis=0, keepdims=True)
        pick = blk_f == first
        picked = jnp.logical_or(picked, pick)
        gm = jnp.where(pick, -jnp.inf, gm)
    allowed = jnp.logical_or(jnp.logical_and(picked, past), blk == own)
    pen_ref[0:nb, :] = jnp.where(allowed, 0.0, NEG).astype(pen_ref.dtype)
    pen_ref[nb:, :] = jnp.zeros((pen_ref.shape[0] - nb, width), pen_ref.dtype)


def _moba_gate(qt, kmean):
    d, s = qt.shape
    nb = s // MOBA_BLOCK
    width = 2048
    assert s % width == 0 and nb <= LANES
    return pl.pallas_call(
        _moba_gate_kernel,
        out_shape=jax.ShapeDtypeStruct((N_HEADS * LANES, s), BF16),
        grid=(N_HEADS, s // width),
        in_specs=[pl.BlockSpec((HEAD_DIM, width), lambda h, c: (h, c)),
                  pl.BlockSpec((nb, HEAD_DIM), lambda h, c: (0, h))],
        out_specs=pl.BlockSpec((LANES, width), lambda h, c: (h, c)),
        compiler_params=_params(("parallel", "parallel"), 16 * 2**20),
        name="moba_gate",
    )(qt, kmean)


def _moba_kernel(qt_ref, pen_ref, qt_next_ref, pen_next_ref, kx_ref, vt_ref, bias_ref,
                 o_ref, s0_ref, s1_ref, p0_ref, p1_ref):
    step = pl.program_id(1)
    i = step * MOBA_QUERY_BLOCKS
    i_last = i + MOBA_QUERY_BLOCKS - 1
    width = MOBA_QUERY_BLOCKS * MOBA_BLOCK
    tile = MOBA_TILE_BLOCKS * MOBA_BLOCK
    last_tile = vt_ref.shape[0] - 1
    qx = jnp.concatenate([qt_ref[...], pen_ref[...]], axis=0)
    qx_next = jnp.concatenate([qt_next_ref[...], pen_next_ref[...]], axis=0)

    def scores(t, s_ref, qx):
        t = jnp.minimum(t, last_tile)
        rows = pl.ds(pl.multiple_of(t * tile, tile), tile)
        s_ref[...] = jnp.dot(kx_ref[rows, :], qx, preferred_element_type=F32)

    def value_dot(t, p_ref):
        return jnp.dot(vt_ref[jnp.clip(t, 0, last_tile)], p_ref[...], preferred_element_type=F32)

    def update(t, s_ref, p_ref, p_prev_ref, carry, with_bias):
        m, l, acc, alpha_prev = carry
        pv = value_dot(t - 1, p_prev_ref)
        if with_bias:
            def bias_tile(c, a):
                return bias_ref[jnp.clip(i + a - (t * MOBA_TILE_BLOCKS + c), 0, N_BIAS_TILES - 1)]
            s_ref[...] += jnp.concatenate(
                [jnp.concatenate([bias_tile(c, a) for a in range(MOBA_QUERY_BLOCKS)], axis=1)
                 for c in range(MOBA_TILE_BLOCKS)], axis=0)
        m_new = jnp.maximum(m, jnp.max(s_ref[...], axis=0, keepdims=True))
        alpha = jnp.exp2(m - m_new)
        p = jnp.exp2(s_ref[...] - m_new)
        l = alpha * l + jnp.sum(p, axis=0, keepdims=True)
        p_ref[...] = p.astype(BF16)
        return m_new, l, alpha_prev * acc + pv, alpha

    n_tiles = (i_last + MOBA_TILE_BLOCKS) // MOBA_TILE_BLOCKS
    n_far_pairs = jnp.maximum(i - (N_BIAS_TILES - 2), 0) // (2 * MOBA_TILE_BLOCKS)
    n_pairs = (n_tiles + 1) // 2

    def tile_pair(u, carry, with_bias):
        t0 = 2 * u
        scores(t0 + 1, s1_ref, qx)
        carry = update(t0, s0_ref, p0_ref, p1_ref, carry, with_bias)
        if with_bias:
            is_last = u == n_pairs - 1
            scores(jnp.where(is_last, 0, t0 + 2), s0_ref, jnp.where(is_last, qx_next, qx))
        else:
            scores(t0 + 2, s0_ref, qx)
        return update(t0 + 1, s1_ref, p1_ref, p0_ref, carry, with_bias)

    @pl.when(step == 0)
    def _():
        scores(0, s0_ref, qx)

    p1_ref[...] = jnp.zeros(p1_ref.shape, p1_ref.dtype)
    carry = (jnp.full((1, width), NEG, F32), jnp.zeros((1, width), F32),
             jnp.zeros((HEAD_DIM, width), F32), jnp.ones((1, width), F32))
    carry = lax.fori_loop(0, n_far_pairs, functools.partial(tile_pair, with_bias=False), carry)
    _, l, acc, alpha = lax.fori_loop(n_far_pairs, n_pairs,
                                     functools.partial(tile_pair, with_bias=True), carry)
    acc = alpha * acc + value_dot(2 * n_pairs - 1, p1_ref)
    o_ref[...] = (acc / l).T.astype(o_ref.dtype)


def _moba_attention(qt, pen, kx, vt, bias):
    s = kx.shape[0]
    nb = s // MOBA_BLOCK
    tile = MOBA_TILE_BLOCKS * MOBA_BLOCK
    assert nb % (2 * MOBA_TILE_BLOCKS) == 0 and nb <= LANES and vt.shape == (s // tile, D_MODEL, tile)
    kv_blk = s * HEAD_DIM * 2
    width = MOBA_QUERY_BLOCKS * MOBA_BLOCK
    n_steps = nb // MOBA_QUERY_BLOCKS
    assert nb % MOBA_QUERY_BLOCKS == 0
    vmem = 2 * (3 * kv_blk + N_BIAS_TILES * MOBA_BLOCK * MOBA_BLOCK * 4) + 12 * tile * width + 24 * 2**20
    cur = lambda h, i: (h, i)
    nxt = lambda h, i: (h, jnp.minimum(i + 1, n_steps - 1))
    return pl.pallas_call(
        _moba_kernel,
        out_shape=jax.ShapeDtypeStruct((s, D_MODEL), BF16),
        grid=(N_HEADS, n_steps),
        in_specs=[pl.BlockSpec((HEAD_DIM, width), cur),
                  pl.BlockSpec((LANES, width), cur),
                  pl.BlockSpec((HEAD_DIM, width), nxt),
                  pl.BlockSpec((LANES, width), nxt),
                  pl.BlockSpec((s, 2 * HEAD_DIM), lambda h, i: (0, h)),
                  pl.BlockSpec((s // tile, HEAD_DIM, tile), lambda h, i: (0, h, 0)),
                  pl.BlockSpec((None, N_BIAS_TILES, MOBA_BLOCK, MOBA_BLOCK),
                               lambda h, i: (h, 0, 0, 0))],
        out_specs=pl.BlockSpec((width, HEAD_DIM), lambda h, i: (i, h)),
        scratch_shapes=[pltpu.VMEM((tile, width), F32)] * 2
        + [pltpu.VMEM((tile, width), BF16)] * 2,
        compiler_params=_params(("parallel", "arbitrary"), vmem),
        name="moba_attention",
    )(qt, pen, qt, pen, kx, vt, bias)


def _ffn(x, norm_gain, w_gate_up, w_down, layer):
    (h,) = _rmsnorm(x, norm_gain[None, :])
    act = _mm_swiglu(h, w_gate_up, layer)
    return _mm_residual(act, w_down[layer].astype(BF16)[None], 0, x, tm=512, tn=512)


def kernel(x, rel_bias, attn_norm, ffn_norm, w_qkv_a, q_norm_a, k_norm_a, w_o_a, kv_norm, w_kv_b, k_norm_b, w_q_b, q_norm_b, w_o_b, w_gate_up, w_down):
    assert x.shape[0] == 1 and x.shape[2] == D_MODEL
    xs = x[0].astype(F32)
    d = D_MODEL

    table = rel_bias.astype(F32) * LOG2E
    bias_dil = _bias_expand(_dilated_bucket_tiles(), table)
    bias_moba = _bias_expand(_moba_bucket_tiles(), table - table[N_BUCKETS - 1:, :])

    (h,) = _rmsnorm(xs, attn_norm[0][None, :])
    gain_a = jnp.concatenate(
        [jnp.tile(q_norm_a[0][g].astype(F32) * (SCALE * LOG2E), N_HEADS) for g in range(N_GROUPS)]
        + [jnp.tile(k_norm_a[0].astype(F32), N_HEADS)])[None, :]
    qk, v = _mm_headnorm(h, w_qkv_a, 0, gain_a, (N_GROUPS + 1) * d, F32)
    o = _dilated_attention(qk, v, bias_dil)
    xs = _mm_residual(o, w_o_a, 0, xs, tm=RES_TM, tn=RES_TN)
    xs = _ffn(xs, ffn_norm[0], w_gate_up, w_down, 0)

    hk, hq = _rmsnorm(xs, jnp.stack([kv_norm, attn_norm[1]]))
    gain_k = jnp.tile(k_norm_b.astype(F32), N_HEADS)[None, :]
    k, kmean, vt = _mm_shared_kv(hk, w_kv_b, gain_k)
    gain_q = jnp.tile(q_norm_b[0].astype(F32) * (SCALE * LOG2E), N_HEADS)[None, :]
    qt = _mm_headnorm_t(hq, w_q_b, 0, gain_q)
    o = _moba_attention(qt, _moba_gate(qt, kmean), k, vt, bias_moba)
    xs = _mm_residual(o, w_o_b, 0, xs, tm=RES_TM, tn=RES_TN)
    xs = _ffn(xs, ffn_norm[1], w_gate_up, w_down, 1)
    return xs[None].astype(x.dtype)
```

```python
import functools
import math

import numpy as np
import jax
import jax.numpy as jnp
from jax import lax
from jax.experimental import pallas as pl
from jax.experimental.pallas import tpu as pltpu

D_MODEL = 4096
HEAD_DIM = 128
N_HEADS = D_MODEL // HEAD_DIM
DILATED_PATTERNS = ((128, 1), (512, 4), (2048, 16))
N_GROUPS = len(DILATED_PATTERNS)
BAND = 128
MOBA_BLOCK = 256
MOBA_TOPK = 3
N_BUCKETS = 32
MAX_DISTANCE = 2048
D_FF = 11008
EPS = 1e-6
NEG = -1e30
SCALE = HEAD_DIM ** -0.5
LOG2E = math.log2(math.e)

UNIT = 2048
BLOCKS_PER_UNIT = UNIT // BAND
DIL_UNROLL = 16
N_BIAS_TILES = 8
MOBA_TILE_BLOCKS = 4
MOBA_QUERY_BLOCKS = 2

V7X_SCOPED_VMEM_BYTES = 60000 * 1024
LANES = 128

F32 = jnp.float32
BF16 = jnp.bfloat16


def _params(semantics, vmem_bytes):
    return pltpu.CompilerParams(dimension_semantics=semantics,
                                vmem_limit_bytes=min(int(vmem_bytes), V7X_SCOPED_VMEM_BYTES))


def _rel_bucket_np(dist):
    n = np.maximum(dist, 0)
    exact = N_BUCKETS // 2
    nf = np.maximum(n, exact).astype(np.float64)
    large = exact + (np.log(nf / exact) / math.log(MAX_DISTANCE / exact)
                     * (N_BUCKETS - exact)).astype(np.int64)
    return np.where(n < exact, n, np.minimum(large, N_BUCKETS - 1)).astype(np.int32)


MASKED = N_BUCKETS


def _dilated_bucket_tiles():
    i = np.arange(BAND)[:, None]
    j = np.arange(2 * BAND)[None, :]
    tiles = []
    for window, d in DILATED_PATTERNS:
        w_sub = window // d
        per = []
        for first in (False, True):
            dist = (i - j) if first else (BAND + i - j)
            ok = (dist >= 0) & (dist <= w_sub)
            per.append(np.where(ok, _rel_bucket_np(dist * d), MASKED))
        tiles.append(np.stack(per))
    return np.stack(tiles).astype(np.int32)


def _moba_bucket_tiles():
    i = np.arange(MOBA_BLOCK)[None, :]
    j = np.arange(MOBA_BLOCK)[:, None]
    tiles = []
    for delta in range(N_BIAS_TILES - 1):
        dist = delta * MOBA_BLOCK + i - j
        b = _rel_bucket_np(dist)
        tiles.append(np.where(dist >= 0, b, MASKED) if delta == 0 else b)
    far = (N_BIAS_TILES - 2) * MOBA_BLOCK + 1
    assert _rel_bucket_np(np.array([far]))[0] == N_BUCKETS - 1
    tiles.append(np.full((MOBA_BLOCK, MOBA_BLOCK), N_BUCKETS - 1))
    return np.stack(tiles).astype(np.int32)


TABLE_ROWS = 64


def _bias_expand_kernel(idx_ref, tab_ref, o_ref):
    idx = idx_ref[...]
    rows = lax.broadcasted_iota(jnp.int32, (TABLE_ROWS, idx.shape[1]), 0)
    onehot = (rows == idx).astype(BF16)
    tab = tab_ref[...]
    hi = tab.astype(BF16)
    rest = tab - hi.astype(F32)
    mid = rest.astype(BF16)
    lo = (rest - mid.astype(F32)).astype(BF16)
    hi, mid, lo = (jnp.dot(piece, onehot, preferred_element_type=F32) for piece in (hi, mid, lo))
    o_ref[...] = (hi + mid) + lo


def _bias_expand(idx_np, table):
    idx_flat = jnp.asarray(idx_np.ravel())
    n = idx_flat.shape[0]
    tn = 8192
    assert n % tn == 0
    tab = jnp.zeros((N_HEADS, TABLE_ROWS), F32).at[:, :N_BUCKETS].set(table.astype(F32).T)
    tab = tab.at[:, MASKED].set(NEG)
    return pl.pallas_call(
        _bias_expand_kernel,
        out_shape=jax.ShapeDtypeStruct((N_HEADS, n), F32),
        grid=(n // tn,),
        in_specs=[pl.BlockSpec((1, tn), lambda i: (0, i)),
                  pl.BlockSpec((N_HEADS, TABLE_ROWS), lambda i: (0, 0))],
        out_specs=pl.BlockSpec((N_HEADS, tn), lambda i: (0, i)),
        compiler_params=_params(("parallel",), 16 * 2**20),
        name="bias_expand",
    )(idx_flat.reshape(1, n), tab).reshape((N_HEADS,) + idx_np.shape)


def _rmsnorm_kernel(x_ref, g_ref, *o_refs):
    x = x_ref[...]
    y = x * lax.rsqrt(jnp.mean(x * x, axis=-1, keepdims=True) + EPS)
    for n, o_ref in enumerate(o_refs):
        o_ref[...] = (y * g_ref[n:n + 1, :]).astype(o_ref.dtype)


def _rmsnorm(x, gains):
    s, d = x.shape
    n_out = gains.shape[0]
    tr = 256
    outs = pl.pallas_call(
        _rmsnorm_kernel,
        out_shape=[jax.ShapeDtypeStruct((s, d), BF16)] * n_out,
        grid=(s // tr,),
        in_specs=[pl.BlockSpec((tr, d), lambda i: (i, 0)),
                  pl.BlockSpec((n_out, d), lambda i: (0, 0))],
        out_specs=[pl.BlockSpec((tr, d), lambda i: (i, 0))] * n_out,
        compiler_params=_params(("parallel",), 2 * tr * d * (4 + 2 * n_out) + 8 * 2**20),
        name="rmsnorm",
    )(x, gains.astype(F32))
    return outs


def _head_rms_cols(acc, gain_ref, c):
    t = acc[:, c * HEAD_DIM:(c + 1) * HEAD_DIM]
    t = t * lax.rsqrt(jnp.mean(t * t, axis=-1, keepdims=True) + EPS)
    return t * gain_ref[:, c * HEAD_DIM:(c + 1) * HEAD_DIM]


def _wmatmul_kernel(*refs, epilogue, n_w, n_extra, n_out, cast):
    a_ref = refs[0]
    w_refs = refs[1:1 + n_w]
    extra_refs = refs[1 + n_w:1 + n_w + n_extra]
    out_refs = refs[1 + n_w + n_extra:1 + n_w + n_extra + n_out]
    if cast:
        wb_refs = refs[1 + n_w + n_extra + n_out:]

        @pl.when(pl.program_id(1) == 0)
        def _():
            for w_ref, wb_ref in zip(w_refs, wb_refs):
                wb_ref[...] = w_ref[...].astype(BF16)
    else:
        wb_refs = w_refs
    half = a_ref.shape[0] // MM_ROW_SPLIT
    for h in range(MM_ROW_SPLIT):
        rows = slice(h * half, (h + 1) * half)
        a = a_ref[rows, :]
        accs = [jnp.dot(a, wb_ref[...], preferred_element_type=F32) for wb_ref in wb_refs]
        epilogue(accs, extra_refs, out_refs, rows)


def _wmatmul(epilogue, a, weights, extra, outs, *, n_cols, tm, tn, name):
    m, k = a.shape
    assert m % tm == 0 and n_cols % tn == 0
    cast = weights[0][0].dtype != BF16
    w_specs = [pl.BlockSpec((None, k, tn), lambda j, i, layer=layer, off=off: (layer, 0, j + off))
               for _, layer, off in weights]
    wsz = weights[0][0].dtype.itemsize
    blocks = tm * k * 2 + len(weights) * k * tn * wsz
    for arr, spec in list(extra) + [(o, s) for o, s in outs]:
        blocks += math.prod(d for d in spec.block_shape if d is not None) * jnp.dtype(arr.dtype).itemsize
    vmem = 2 * blocks + (len(weights) * k * tn * 2 if cast else 0) + (2 + 2 * len(weights)) * tm * tn * 4
    return pl.pallas_call(
        functools.partial(_wmatmul_kernel, epilogue=epilogue, n_w=len(weights), n_extra=len(extra),
                          n_out=len(outs), cast=cast),
        out_shape=[o for o, _ in outs],
        grid=(n_cols // tn, m // tm),
        in_specs=[pl.BlockSpec((tm, k), lambda j, i: (i, 0))] + w_specs + [s for _, s in extra],
        out_specs=[s for _, s in outs],
        scratch_shapes=[pltpu.VMEM((k, tn), BF16)] * len(weights) if cast else [],
        compiler_params=_params(("parallel", "arbitrary"), vmem),
        name=name,
    )(a, *[w for w, _, _ in weights], *[arr for arr, _ in extra])


PROJ_TM, PROJ_TN = 512, 1024
RES_TM, RES_TN = 1024, 512
MM_ROW_SPLIT = 2


def _gain_spec(tn):
    return pl.BlockSpec((1, tn), lambda j, i: (0, j))


def _headnorm_epilogue(accs, extra_refs, out_refs, rows):
    (acc,), (gain_ref,), (o_ref,) = accs, extra_refs, out_refs
    for c in range(acc.shape[1] // HEAD_DIM):
        t = _head_rms_cols(acc, gain_ref, c)
        o_ref[rows, c * HEAD_DIM:(c + 1) * HEAD_DIM] = t.astype(o_ref.dtype)


def _store_epilogue(accs, extra_refs, out_refs, rows):
    out_refs[0][rows, :] = accs[0].astype(out_refs[0].dtype)


def _mm_headnorm(a, w, layer, gain_row, n_norm_cols, out_dtype):
    m = a.shape[0]
    n = w.shape[2]
    tm, tn = PROJ_TM, PROJ_TN
    assert n_norm_cols % tn == 0
    tile = pl.BlockSpec((tm, tn), lambda j, i: (i, j))
    (normed,) = _wmatmul(
        _headnorm_epilogue, a, [(w, layer, 0)], [(gain_row, _gain_spec(tn))],
        [(jax.ShapeDtypeStruct((m, n_norm_cols), out_dtype), tile)],
        n_cols=n_norm_cols, tm=tm, tn=tn, name="mm_headnorm")
    (rest,) = _wmatmul(
        _store_epilogue, a, [(w, layer, n_norm_cols // tn)], [],
        [(jax.ShapeDtypeStruct((m, n - n_norm_cols), out_dtype), tile)],
        n_cols=n - n_norm_cols, tm=tm, tn=tn, name="mm_plain")
    return normed, rest


def _headnorm_t_epilogue(accs, extra_refs, out_refs, rows):
    (acc,), (gain_ref,), (ot_ref,) = accs, extra_refs, out_refs
    for c in range(acc.shape[1] // HEAD_DIM):
        t = _head_rms_cols(acc, gain_ref, c)
        ot_ref[c * HEAD_DIM:(c + 1) * HEAD_DIM, rows] = t.T.astype(ot_ref.dtype)


def _mm_headnorm_t(a, w, layer, gain_row):
    m = a.shape[0]
    n = w.shape[2]
    tm, tn = PROJ_TM, PROJ_TN
    (out,) = _wmatmul(
        _headnorm_t_epilogue, a, [(w, layer, 0)], [(gain_row, _gain_spec(tn))],
        [(jax.ShapeDtypeStruct((n, m), BF16), pl.BlockSpec((tn, tm), lambda j, i: (j, i)))],
        n_cols=n, tm=tm, tn=tn, name="mm_headnorm_t")
    return out


def _key_epilogue(accs, extra_refs, out_refs, rows):
    (acc,), (gain_ref,), (kx_ref, mean_ref) = accs, extra_refs, out_refs
    n_rows, tn = acc.shape
    assert rows.start % MOBA_BLOCK == 0 and n_rows % MOBA_BLOCK == 0
    row = (pl.program_id(1) * kx_ref.shape[0] + rows.start
           + lax.broadcasted_iota(jnp.int32, (n_rows, LANES), 0))
    onehot = (row // MOBA_BLOCK == lax.broadcasted_iota(jnp.int32, (n_rows, LANES), 1)
              ).astype(kx_ref.dtype)
    for c in range(tn // HEAD_DIM):
        cols = slice(c * HEAD_DIM, (c + 1) * HEAD_DIM)
        t = _head_rms_cols(acc, gain_ref, c)
        kx_ref[rows, 2 * c * HEAD_DIM:(2 * c + 1) * HEAD_DIM] = t.astype(kx_ref.dtype)
        kx_ref[rows, (2 * c + 1) * HEAD_DIM:(2 * c + 2) * HEAD_DIM] = onehot
        for r in range(n_rows // MOBA_BLOCK):
            blk = t[r * MOBA_BLOCK:(r + 1) * MOBA_BLOCK, :]
            r_out = rows.start // MOBA_BLOCK + r
            mean_ref[r_out:r_out + 1, cols] = jnp.mean(blk, axis=0, keepdims=True)


def _transpose_epilogue(accs, extra_refs, out_refs, rows):
    out_refs[0][:, rows] = accs[0].T.astype(out_refs[0].dtype)


def _mm_shared_kv(a, w_kv, gain_row):
    m = a.shape[0]
    d = w_kv.shape[1] // 2
    w = w_kv[None]
    tm, tn = PROJ_TM, PROJ_TN
    rows = tm // MOBA_BLOCK
    assert m // MOBA_BLOCK <= LANES
    kn, kmean = _wmatmul(
        _key_epilogue, a, [(w, 0, 0)], [(gain_row, _gain_spec(tn))],
        [(jax.ShapeDtypeStruct((m, 2 * d), BF16), pl.BlockSpec((tm, 2 * tn), lambda j, i: (i, j))),
         (jax.ShapeDtypeStruct((m // tm, rows, d), F32),
          pl.BlockSpec((None, rows, tn), lambda j, i: (i, 0, j)))],
        n_cols=d, tm=tm, tn=tn, name="mm_keys")
    tile = MOBA_TILE_BLOCKS * MOBA_BLOCK
    per_tile = tile // tm
    assert tile % tm == 0
    (vt,) = _wmatmul(
        _transpose_epilogue, a, [(w, 0, d // tn)], [],
        [(jax.ShapeDtypeStruct((m // tile, d, tile), BF16),
          pl.BlockSpec((None, tn, tm), lambda j, i: (i // per_tile, j, i % per_tile)))],
        n_cols=d, tm=tm, tn=tn, name="mm_values_t")
    return kn, kmean.reshape(m // MOBA_BLOCK, d), vt


def _residual_epilogue(accs, extra_refs, out_refs, rows):
    out_refs[0][rows, :] = extra_refs[0][rows, :] + accs[0]


def _mm_residual(a, w, layer, res, tm, tn):
    m = a.shape[0]
    n = w.shape[2]
    tile = pl.BlockSpec((tm, tn), lambda j, i: (i, j))
    (out,) = _wmatmul(_residual_epilogue, a, [(w, layer, 0)], [(res, tile)],
                      [(jax.ShapeDtypeStruct((m, n), F32), tile)],
                      n_cols=n, tm=tm, tn=tn, name="mm_residual")
    return out


def _swiglu_epilogue(accs, extra_refs, out_refs, rows):
    g, u = accs
    out_refs[0][rows, :] = (g * (1.0 / (1.0 + jnp.exp(-g))) * u).astype(out_refs[0].dtype)


def _mm_swiglu(a, w_gate_up, layer):
    m = a.shape[0]
    tm, tn = 1024, 256
    (out,) = _wmatmul(
        _swiglu_epilogue, a, [(w_gate_up, layer, 0), (w_gate_up, layer, D_FF // tn)], [],
        [(jax.ShapeDtypeStruct((m, D_FF), BF16), pl.BlockSpec((tm, tn), lambda j, i: (i, j)))],
        n_cols=D_FF, tm=tm, tn=tn, name="mm_swiglu")
    return out


CAST_ROWS = 256


def _cast_rider_specs(w, layer, n_inner):
    _, k, n = w.shape
    assert k % CAST_ROWS == 0
    last = k // CAST_ROWS - 1

    def block(a, b):
        return jnp.minimum(a * n_inner + b, last)

    return (pl.BlockSpec((None, CAST_ROWS, n), lambda a, b: (layer, block(a, b), 0)),
            pl.BlockSpec((CAST_ROWS, n), lambda a, b: (block(a, b), 0)),
            jax.ShapeDtypeStruct((k, n), BF16))


def _cast_rider(w_ref, wb_ref, n_blocks):
    step = pl.program_id(0) * pl.num_programs(1) + pl.program_id(1)

    @pl.when(step < n_blocks)
    def _():
        wb_ref[...] = w_ref[...].astype(wb_ref.dtype)


def _dilated_kernel(q0_ref, q1_ref, q2_ref, k_ref, v_ref, bias_ref, w_ref, o_ref, wb_ref, *scratch,
                    n_cast_blocks):
    og, mg, lg = scratch[0:3], scratch[3:6], scratch[6:9]
    u = pl.program_id(1)
    q_refs = (q0_ref, q1_ref, q2_ref)
    _cast_rider(w_ref, wb_ref, n_cast_blocks)

    for g, (_, d) in enumerate(DILATED_PATTERNS):
        shift = d.bit_length() - 1

        def block(b, g=g, d=d, shift=shift):
            r = jnp.bitwise_and(b, d - 1)
            nl = jnp.right_shift(b, shift)
            rel = r + (BAND * d) * nl
            rows = pl.ds(rel, BAND, stride=d) if d > 1 else pl.ds(rel, BAND)
            q = q_refs[g][rows, :].astype(BF16)
            first = jnp.logical_and(u == 0, nl == 0)
            ks = jnp.where(first, r, u * UNIT + rel - BAND * d)
            krows = pl.ds(ks, 2 * BAND, stride=d) if d > 1 else pl.ds(ks, 2 * BAND)
            k = k_ref[krows, :].astype(BF16)
            v = v_ref[krows, :].astype(BF16)
            s = lax.dot_general(q, k, (((1,), (1,)), ((), ())), preferred_element_type=F32)
            s = s + bias_ref[g, first.astype(jnp.int32)]
            m = jnp.max(s, axis=1, keepdims=True)
            p = jnp.exp2(s - m)
            l = jnp.sum(p, axis=1, keepdims=True)
            o = jnp.dot(p.astype(BF16), v, preferred_element_type=F32)
            og[g][rows, :] = o
            mg[g][rows, :] = jnp.broadcast_to(m, (BAND, HEAD_DIM))
            lg[g][rows, :] = jnp.broadcast_to(l, (BAND, HEAD_DIM))

        def blocks(it, carry, block=block):
            for c in range(DIL_UNROLL):
                block(it * DIL_UNROLL + c)
            return carry

        lax.fori_loop(0, BLOCKS_PER_UNIT // DIL_UNROLL, blocks, 0)

    chunk = 256

    def merge(c, carry):
        rows = pl.ds(pl.multiple_of(c * chunk, chunk), chunk)
        ms = [mg[g][rows, :] for g in range(N_GROUPS)]
        top = jnp.maximum(jnp.maximum(ms[0], ms[1]), ms[2])
        num = jnp.zeros((chunk, HEAD_DIM), F32)
        den = jnp.zeros((chunk, HEAD_DIM), F32)
        for g in range(N_GROUPS):
            w = jnp.exp2(ms[g] - top)
            num = num + w * og[g][rows, :]
            den = den + w * lg[g][rows, :]
        o_ref[rows, :] = (num / den).astype(o_ref.dtype)
        return carry

    lax.fori_loop(0, UNIT // chunk, merge, 0)


def _dilated_attention(qk, v, bias, w_cast, cast_layer):
    s = qk.shape[0]
    assert s % UNIT == 0
    n_units = s // UNIT
    w_in, w_out, w_shape = _cast_rider_specs(w_cast, cast_layer, n_units)
    n_cast_blocks = w_cast.shape[1] // CAST_ROWS
    assert n_cast_blocks <= N_HEADS * n_units
    hcols = N_HEADS
    q_specs = [pl.BlockSpec((UNIT, HEAD_DIM), lambda h, u, g=g: (u, g * hcols + h))
               for g in range(N_GROUPS)]
    k_spec = pl.BlockSpec((s, HEAD_DIM), lambda h, u: (0, N_GROUPS * hcols + h))
    v_spec = pl.BlockSpec((s, HEAD_DIM), lambda h, u: (0, h))
    b_spec = pl.BlockSpec((None, N_GROUPS, 2, BAND, 2 * BAND), lambda h, u: (h, 0, 0, 0, 0))
    blk = UNIT * HEAD_DIM * 4
    vmem = 2 * (3 * blk + 2 * s * HEAD_DIM * 4 + N_GROUPS * 2 * BAND * 2 * BAND * 4 + blk // 2) \
        + 9 * blk + 2 * CAST_ROWS * w_cast.shape[2] * 6 + 8 * 2**20
    return pl.pallas_call(
        functools.partial(_dilated_kernel, n_cast_blocks=n_cast_blocks),
        out_shape=[jax.ShapeDtypeStruct((s, D_MODEL), BF16), w_shape],
        grid=(N_HEADS, n_units),
        in_specs=q_specs + [k_spec, v_spec, b_spec, w_in],
        out_specs=[pl.BlockSpec((UNIT, HEAD_DIM), lambda h, u: (u, h)), w_out],
        scratch_shapes=[pltpu.VMEM((UNIT, HEAD_DIM), F32)] * 9,
        compiler_params=_params(("arbitrary", "arbitrary"), vmem),
        name="dilated_attention",
    )(qk, qk, qk, qk, v, bias, w_cast)


def _moba_gate_kernel(qt_ref, kmean_ref, pen_ref):
    nb = kmean_ref.shape[0]
    width = qt_ref.shape[1]
    gate = jnp.dot(kmean_ref[...].astype(BF16), qt_ref[...], preferred_element_type=F32)
    blk = lax.broadcasted_iota(jnp.int32, gate.shape, 0)
    blk_f = blk.astype(F32)
    t0 = pl.program_id(1) * width
    own = (t0 + lax.broadcasted_iota(jnp.int32, gate.shape, 1)) // MOBA_BLOCK
    past = blk < own
    gm = jnp.where(past, gate, NEG)
    picked = jnp.zeros(gate.shape, jnp.bool_)
    for _ in range(MOBA_TOPK):
        mx = jnp.max(gm, axis=0, keepdims=True)
        first = jnp.min(jnp.where(gm == mx, blk_f, float(LANES)), axis=0, keepdims=True)
        pick = blk_f == first
        picked = jnp.logical_or(picked, pick)
        gm = jnp.where(pick, -jnp.inf, gm)
    allowed = jnp.logical_or(jnp.logical_and(picked, past), blk == own)
    pen_ref[0:nb, :] = jnp.where(allowed, 0.0, NEG).astype(pen_ref.dtype)
    pen_ref[nb:, :] = jnp.zeros((pen_ref.shape[0] - nb, width), pen_ref.dtype)


def _moba_gate(qt, kmean):
    d, s = qt.shape
    nb = s // MOBA_BLOCK
    width = 2048
    assert s % width == 0 and nb <= LANES
    return pl.pallas_call(
        _moba_gate_kernel,
        out_shape=jax.ShapeDtypeStruct((N_HEADS * LANES, s), BF16),
        grid=(N_HEADS, s // width),
        in_specs=[pl.BlockSpec((HEAD_DIM, width), lambda h, c: (h, c)),
                  pl.BlockSpec((nb, HEAD_DIM), lambda h, c: (0, h))],
        out_specs=pl.BlockSpec((LANES, width), lambda h, c: (h, c)),
        compiler_params=_params(("parallel", "parallel"), 16 * 2**20),
        name="moba_gate",
    )(qt, kmean)


def _moba_kernel(qt_ref, pen_ref, qt_next_ref, pen_next_ref, kx_ref, vt_ref, bias_ref, w_ref,
                 o_ref, wb_ref, s0_ref, s1_ref, p0_ref, p1_ref, *, n_cast_blocks):
    step = pl.program_id(1)
    i = step * MOBA_QUERY_BLOCKS
    i_last = i + MOBA_QUERY_BLOCKS - 1
    width = MOBA_QUERY_BLOCKS * MOBA_BLOCK
    tile = MOBA_TILE_BLOCKS * MOBA_BLOCK
    last_tile = vt_ref.shape[0] - 1
    _cast_rider(w_ref, wb_ref, n_cast_blocks)
    qx = jnp.concatenate([qt_ref[...], pen_ref[...]], axis=0)
    qx_next = jnp.concatenate([qt_next_ref[...], pen_next_ref[...]], axis=0)

    def scores(t, s_ref, qx):
        t = jnp.minimum(t, last_tile)
        rows = pl.ds(pl.multiple_of(t * tile, tile), tile)
        s_ref[...] = jnp.dot(kx_ref[rows, :], qx, preferred_element_type=F32)

    def value_dot(t, p_ref):
        return jnp.dot(vt_ref[jnp.clip(t, 0, last_tile)], p_ref[...], preferred_element_type=F32)

    def update(t, s_ref, p_ref, p_prev_ref, carry, with_bias):
        m, l, acc, alpha_prev = carry
        pv = value_dot(t - 1, p_prev_ref)
        if with_bias:
            def bias_tile(c, a):
                return bias_ref[jnp.clip(i + a - (t * MOBA_TILE_BLOCKS + c), 0, N_BIAS_TILES - 1)]
            s_ref[...] += jnp.concatenate(
                [jnp.concatenate([bias_tile(c, a) for a in range(MOBA_QUERY_BLOCKS)], axis=1)
                 for c in range(MOBA_TILE_BLOCKS)], axis=0)
        m_new = jnp.maximum(m, jnp.max(s_ref[...], axis=0, keepdims=True))
        alpha = jnp.exp2(m - m_new)
        p = jnp.exp2(s_ref[...] - m_new)
        l = alpha * l + jnp.sum(p, axis=0, keepdims=True)
        p_ref[...] = p.astype(BF16)
        return m_new, l, alpha_prev * acc + pv, alpha

    n_tiles = (i_last + MOBA_TILE_BLOCKS) // MOBA_TILE_BLOCKS
    n_far_pairs = jnp.maximum(i - (N_BIAS_TILES - 2), 0) // (2 * MOBA_TILE_BLOCKS)
    n_pairs = (n_tiles + 1) // 2

    def tile_pair(u, carry, with_bias):
        t0 = 2 * u
        scores(t0 + 1, s1_ref, qx)
        carry = update(t0, s0_ref, p0_ref, p1_ref, carry, with_bias)
        if with_bias:
            is_last = u == n_pairs - 1
            scores(jnp.where(is_last, 0, t0 + 2), s0_ref, jnp.where(is_last, qx_next, qx))
        else:
            scores(t0 + 2, s0_ref, qx)
        return update(t0 + 1, s1_ref, p1_ref, p0_ref, carry, with_bias)

    @pl.when(step == 0)
    def _():
        scores(0, s0_ref, qx)

    p1_ref[...] = jnp.zeros(p1_ref.shape, p1_ref.dtype)
    carry = (jnp.full((1, width), NEG, F32), jnp.zeros((1, width), F32),
             jnp.zeros((HEAD_DIM, width), F32), jnp.ones((1, width), F32))
    carry = lax.fori_loop(0, n_far_pairs, functools.partial(tile_pair, with_bias=False), carry)
    _, l, acc, alpha = lax.fori_loop(n_far_pairs, n_pairs,
                                     functools.partial(tile_pair, with_bias=True), carry)
    acc = alpha * acc + value_dot(2 * n_pairs - 1, p1_ref)
    o_ref[...] = (acc / l).T.astype(o_ref.dtype)


def _moba_attention(qt, pen, kx, vt, bias, w_cast, cast_layer):
    s = kx.shape[0]
    nb = s // MOBA_BLOCK
    tile = MOBA_TILE_BLOCKS * MOBA_BLOCK
    assert nb % (2 * MOBA_TILE_BLOCKS) == 0 and nb <= LANES and vt.shape == (s // tile, D_MODEL, tile)
    kv_blk = s * HEAD_DIM * 2
    width = MOBA_QUERY_BLOCKS * MOBA_BLOCK
    n_steps = nb // MOBA_QUERY_BLOCKS
    assert nb % MOBA_QUERY_BLOCKS == 0
    w_in, w_out, w_shape = _cast_rider_specs(w_cast, cast_layer, n_steps)
    n_cast_blocks = w_cast.shape[1] // CAST_ROWS
    assert n_cast_blocks <= N_HEADS * n_steps
    vmem = 2 * (3 * kv_blk + N_BIAS_TILES * MOBA_BLOCK * MOBA_BLOCK * 4) + 12 * tile * width \
        + 2 * CAST_ROWS * w_cast.shape[2] * 6 + 24 * 2**20
    cur = lambda h, i: (h, i)
    nxt = lambda h, i: (h, jnp.minimum(i + 1, n_steps - 1))
    return pl.pallas_call(
        functools.partial(_moba_kernel, n_cast_blocks=n_cast_blocks),
        out_shape=[jax.ShapeDtypeStruct((s, D_MODEL), BF16), w_shape],
        grid=(N_HEADS, n_steps),
        in_specs=[pl.BlockSpec((HEAD_DIM, width), cur),
                  pl.BlockSpec((LANES, width), cur),
                  pl.BlockSpec((HEAD_DIM, width), nxt),
                  pl.BlockSpec((LANES, width), nxt),
                  pl.BlockSpec((s, 2 * HEAD_DIM), lambda h, i: (0, h)),
                  pl.BlockSpec((s // tile, HEAD_DIM, tile), lambda h, i: (0, h, 0)),
                  pl.BlockSpec((None, N_BIAS_TILES, MOBA_BLOCK, MOBA_BLOCK),
                               lambda h, i: (h, 0, 0, 0)),
                  w_in],
        out_specs=[pl.BlockSpec((width, HEAD_DIM), lambda h, i: (i, h)), w_out],
        scratch_shapes=[pltpu.VMEM((tile, width), F32)] * 2
        + [pltpu.VMEM((tile, width), BF16)] * 2,
        compiler_params=_params(("arbitrary", "arbitrary"), vmem),
        name="moba_attention",
    )(qt, pen, qt, pen, kx, vt, bias, w_cast)


def _ffn(x, norm_gain, w_gate_up, layer, w_down_bf16):
    (h,) = _rmsnorm(x, norm_gain[None, :])
    act = _mm_swiglu(h, w_gate_up, layer)
    return _mm_residual(act, w_down_bf16[None], 0, x, tm=512, tn=512)


def kernel(x, rel_bias, attn_norm, ffn_norm, w_qkv_a, q_norm_a, k_norm_a, w_o_a, kv_norm, w_kv_b, k_norm_b, w_q_b, q_norm_b, w_o_b, w_gate_up, w_down):
    assert x.shape[0] == 1 and x.shape[2] == D_MODEL
    xs = x[0].astype(F32)
    d = D_MODEL

    table = rel_bias.astype(F32) * LOG2E
    bias_dil = _bias_expand(_dilated_bucket_tiles(), table)
    bias_moba = _bias_expand(_moba_bucket_tiles(), table - table[N_BUCKETS - 1:, :])

    (h,) = _rmsnorm(xs, attn_norm[0][None, :])
    gain_a = jnp.concatenate(
        [jnp.tile(q_norm_a[0][g].astype(F32) * (SCALE * LOG2E), N_HEADS) for g in range(N_GROUPS)]
        + [jnp.tile(k_norm_a[0].astype(F32), N_HEADS)])[None, :]
    qk, v = _mm_headnorm(h, w_qkv_a, 0, gain_a, (N_GROUPS + 1) * d, F32)
    o, w_down_0 = _dilated_attention(qk, v, bias_dil, w_down, 0)
    xs = _mm_residual(o, w_o_a, 0, xs, tm=RES_TM, tn=RES_TN)
    xs = _ffn(xs, ffn_norm[0], w_gate_up, 0, w_down_0)

    hk, hq = _rmsnorm(xs, jnp.stack([kv_norm, attn_norm[1]]))
    gain_k = jnp.tile(k_norm_b.astype(F32), N_HEADS)[None, :]
    k, kmean, vt = _mm_shared_kv(hk, w_kv_b, gain_k)
    gain_q = jnp.tile(q_norm_b[0].astype(F32) * (SCALE * LOG2E), N_HEADS)[None, :]
    qt = _mm_headnorm_t(hq, w_q_b, 0, gain_q)
    o, w_down_1 = _moba_attention(qt, _moba_gate(qt, kmean), k, vt, bias_moba, w_down, 1)
    xs = _mm_residual(o, w_o_b, 0, xs, tm=RES_TM, tn=RES_TN)
    xs = _ffn(xs, ffn_norm[1], w_gate_up, 1, w_down_1)
    return xs[None].astype(x.dtype)
```

```python
import functools
import math

import numpy as np
import jax
import jax.numpy as jnp
from jax import lax
from jax.experimental import pallas as pl
from jax.experimental.pallas import tpu as pltpu

D_MODEL = 4096
HEAD_DIM = 128
N_HEADS = D_MODEL // HEAD_DIM
DILATED_PATTERNS = ((128, 1), (512, 4), (2048, 16))
N_GROUPS = len(DILATED_PATTERNS)
BAND = 128
MOBA_BLOCK = 256
MOBA_TOPK = 3
N_BUCKETS = 32
MAX_DISTANCE = 2048
D_FF = 11008
EPS = 1e-6
NEG = -1e30
SCALE = HEAD_DIM ** -0.5
LOG2E = math.log2(math.e)

UNIT = 2048
BLOCKS_PER_UNIT = UNIT // BAND
DIL_UNROLL = 16
N_BIAS_TILES = 8
MOBA_TILE_BLOCKS = 4
MOBA_QUERY_BLOCKS = 2

V7X_SCOPED_VMEM_BYTES = 60000 * 1024
LANES = 128

F32 = jnp.float32
BF16 = jnp.bfloat16


def _params(semantics, vmem_bytes):
    return pltpu.CompilerParams(dimension_semantics=semantics,
                                vmem_limit_bytes=min(int(vmem_bytes), V7X_SCOPED_VMEM_BYTES))


def _rel_bucket_np(dist):
    n = np.maximum(dist, 0)
    exact = N_BUCKETS // 2
    nf = np.maximum(n, exact).astype(np.float64)
    large = exact + (np.log(nf / exact) / math.log(MAX_DISTANCE / exact)
                     * (N_BUCKETS - exact)).astype(np.int64)
    return np.where(n < exact, n, np.minimum(large, N_BUCKETS - 1)).astype(np.int32)


MASKED = N_BUCKETS


def _dilated_bucket_tiles():
    i = np.arange(BAND)[:, None]
    j = np.arange(2 * BAND)[None, :]
    tiles = []
    for window, d in DILATED_PATTERNS:
        w_sub = window // d
        per = []
        for first in (False, True):
            dist = (i - j) if first else (BAND + i - j)
            ok = (dist >= 0) & (dist <= w_sub)
            per.append(np.where(ok, _rel_bucket_np(dist * d), MASKED))
        tiles.append(np.stack(per))
    return np.stack(tiles).astype(np.int32)


def _moba_bucket_tiles():
    i = np.arange(MOBA_BLOCK)[None, :]
    j = np.arange(MOBA_BLOCK)[:, None]
    tiles = []
    for delta in range(N_BIAS_TILES - 1):
        dist = delta * MOBA_BLOCK + i - j
        b = _rel_bucket_np(dist)
        tiles.append(np.where(dist >= 0, b, MASKED) if delta == 0 else b)
    far = (N_BIAS_TILES - 2) * MOBA_BLOCK + 1
    assert _rel_bucket_np(np.array([far]))[0] == N_BUCKETS - 1
    tiles.append(np.full((MOBA_BLOCK, MOBA_BLOCK), N_BUCKETS - 1))
    return np.stack(tiles).astype(np.int32)


TABLE_ROWS = 64


def _bias_expand_kernel(idx_ref, tab_ref, o_ref):
    idx = idx_ref[...]
    rows = lax.broadcasted_iota(jnp.int32, (TABLE_ROWS, idx.shape[1]), 0)
    onehot = (rows == idx).astype(BF16)
    tab = tab_ref[...]
    hi = tab.astype(BF16)
    rest = tab - hi.astype(F32)
    mid = rest.astype(BF16)
    lo = (rest - mid.astype(F32)).astype(BF16)
    hi, mid, lo = (jnp.dot(piece, onehot, preferred_element_type=F32) for piece in (hi, mid, lo))
    o_ref[...] = (hi + mid) + lo


def _bias_expand(idx_np, table):
    idx_flat = jnp.asarray(idx_np.ravel())
    n = idx_flat.shape[0]
    tn = 8192
    assert n % tn == 0
    tab = jnp.zeros((N_HEADS, TABLE_ROWS), F32).at[:, :N_BUCKETS].set(table.astype(F32).T)
    tab = tab.at[:, MASKED].set(NEG)
    return pl.pallas_call(
        _bias_expand_kernel,
        out_shape=jax.ShapeDtypeStruct((N_HEADS, n), F32),
        grid=(n // tn,),
        in_specs=[pl.BlockSpec((1, tn), lambda i: (0, i)),
                  pl.BlockSpec((N_HEADS, TABLE_ROWS), lambda i: (0, 0))],
        out_specs=pl.BlockSpec((N_HEADS, tn), lambda i: (0, i)),
        compiler_params=_params(("parallel",), 16 * 2**20),
        name="bias_expand",
    )(idx_flat.reshape(1, n), tab).reshape((N_HEADS,) + idx_np.shape)


def _rmsnorm_kernel(x_ref, g_ref, *o_refs):
    x = x_ref[...]
    y = x * lax.rsqrt(jnp.mean(x * x, axis=-1, keepdims=True) + EPS)
    for n, o_ref in enumerate(o_refs):
        o_ref[...] = (y * g_ref[n:n + 1, :]).astype(o_ref.dtype)


def _rmsnorm(x, gains):
    s, d = x.shape
    n_out = gains.shape[0]
    tr = 256
    outs = pl.pallas_call(
        _rmsnorm_kernel,
        out_shape=[jax.ShapeDtypeStruct((s, d), BF16)] * n_out,
        grid=(s // tr,),
        in_specs=[pl.BlockSpec((tr, d), lambda i: (i, 0)),
                  pl.BlockSpec((n_out, d), lambda i: (0, 0))],
        out_specs=[pl.BlockSpec((tr, d), lambda i: (i, 0))] * n_out,
        compiler_params=_params(("parallel",), 2 * tr * d * (4 + 2 * n_out) + 8 * 2**20),
        name="rmsnorm",
    )(x, gains.astype(F32))
    return outs


def _head_rms_cols(acc, gain_ref, c):
    t = acc[:, c * HEAD_DIM:(c + 1) * HEAD_DIM]
    t = t * lax.rsqrt(jnp.mean(t * t, axis=-1, keepdims=True) + EPS)
    return t * gain_ref[:, c * HEAD_DIM:(c + 1) * HEAD_DIM]


def _wmatmul_kernel(*refs, epilogue, n_w, n_extra, n_out, cast):
    a_ref = refs[0]
    w_refs = refs[1:1 + n_w]
    extra_refs = refs[1 + n_w:1 + n_w + n_extra]
    out_refs = refs[1 + n_w + n_extra:1 + n_w + n_extra + n_out]
    if cast:
        wb_refs = refs[1 + n_w + n_extra + n_out:]

        @pl.when(pl.program_id(1) == 0)
        def _():
            for w_ref, wb_ref in zip(w_refs, wb_refs):
                wb_ref[...] = w_ref[...].astype(BF16)
    else:
        wb_refs = w_refs
    half = a_ref.shape[0] // MM_ROW_SPLIT
    for h in range(MM_ROW_SPLIT):
        rows = slice(h * half, (h + 1) * half)
        a = a_ref[rows, :]
        accs = [jnp.dot(a, wb_ref[...], preferred_element_type=F32) for wb_ref in wb_refs]
        epilogue(accs, extra_refs, out_refs, rows)


def _wmatmul(epilogue, a, weights, extra, outs, *, n_cols, tm, tn, name):
    m, k = a.shape
    assert m % tm == 0 and n_cols % tn == 0
    cast = weights[0][0].dtype != BF16
    w_specs = [pl.BlockSpec((None, k, tn), lambda j, i, layer=layer, off=off: (layer, 0, j + off))
               for _, layer, off in weights]
    wsz = weights[0][0].dtype.itemsize
    blocks = tm * k * 2 + len(weights) * k * tn * wsz
    for arr, spec in list(extra) + [(o, s) for o, s in outs]:
        blocks += math.prod(d for d in spec.block_shape if d is not None) * jnp.dtype(arr.dtype).itemsize
    vmem = 2 * blocks + (len(weights) * k * tn * 2 if cast else 0) + (2 + 2 * len(weights)) * tm * tn * 4
    return pl.pallas_call(
        functools.partial(_wmatmul_kernel, epilogue=epilogue, n_w=len(weights), n_extra=len(extra),
                          n_out=len(outs), cast=cast),
        out_shape=[o for o, _ in outs],
        grid=(n_cols // tn, m // tm),
        in_specs=[pl.BlockSpec((tm, k), lambda j, i: (i, 0))] + w_specs + [s for _, s in extra],
        out_specs=[s for _, s in outs],
        scratch_shapes=[pltpu.VMEM((k, tn), BF16)] * len(weights) if cast else [],
        compiler_params=_params(("parallel", "arbitrary"), vmem),
        name=name,
    )(a, *[w for w, _, _ in weights], *[arr for arr, _ in extra])


PROJ_TM, PROJ_TN = 512, 1024
RES_TM, RES_TN = 1024, 512
MM_ROW_SPLIT = 2


def _gain_spec(tn):
    return pl.BlockSpec((1, tn), lambda j, i: (0, j))


def _headnorm_epilogue(accs, extra_refs, out_refs, rows):
    (acc,), (gain_ref,), (o_ref,) = accs, extra_refs, out_refs
    for c in range(acc.shape[1] // HEAD_DIM):
        t = _head_rms_cols(acc, gain_ref, c)
        o_ref[rows, c * HEAD_DIM:(c + 1) * HEAD_DIM] = t.astype(o_ref.dtype)


def _store_epilogue(accs, extra_refs, out_refs, rows):
    out_refs[0][rows, :] = accs[0].astype(out_refs[0].dtype)


def _mm_headnorm(a, w, layer, gain_row, n_norm_cols, out_dtype):
    m = a.shape[0]
    n = w.shape[2]
    tm, tn = PROJ_TM, PROJ_TN
    assert n_norm_cols % tn == 0
    tile = pl.BlockSpec((tm, tn), lambda j, i: (i, j))
    (normed,) = _wmatmul(
        _headnorm_epilogue, a, [(w, layer, 0)], [(gain_row, _gain_spec(tn))],
        [(jax.ShapeDtypeStruct((m, n_norm_cols), out_dtype), tile)],
        n_cols=n_norm_cols, tm=tm, tn=tn, name="mm_headnorm")
    (rest,) = _wmatmul(
        _store_epilogue, a, [(w, layer, n_norm_cols // tn)], [],
        [(jax.ShapeDtypeStruct((m, n - n_norm_cols), out_dtype), tile)],
        n_cols=n - n_norm_cols, tm=tm, tn=tn, name="mm_plain")
    return normed, rest


def _block_gate(gate, own):
    blk = lax.broadcasted_iota(jnp.int32, gate.shape, 0)
    blk_f = blk.astype(F32)
    past = blk < own
    gm = jnp.where(past, gate, NEG)
    picked = jnp.zeros(gate.shape, jnp.bool_)
    for _ in range(MOBA_TOPK):
        mx = jnp.max(gm, axis=0, keepdims=True)
        first = jnp.min(jnp.where(gm == mx, blk_f, float(LANES)), axis=0, keepdims=True)
        pick = blk_f == first
        picked = jnp.logical_or(picked, pick)
        gm = jnp.where(pick, -jnp.inf, gm)
    allowed = jnp.logical_or(jnp.logical_and(picked, past), blk == own)
    return jnp.where(allowed, 0.0, NEG)


def _query_gate_epilogue(accs, extra_refs, out_refs, rows):
    (acc,), (gain_ref, kmean_ref), (qt_ref, pen_ref) = accs, extra_refs, out_refs
    n_rows = acc.shape[0]
    nb = kmean_ref.shape[0]
    query = (pl.program_id(1) * qt_ref.shape[1] + rows.start
             + lax.broadcasted_iota(jnp.int32, (nb, n_rows), 1))
    own = query // MOBA_BLOCK
    for c in range(acc.shape[1] // HEAD_DIM):
        cols = slice(c * HEAD_DIM, (c + 1) * HEAD_DIM)
        qt = _head_rms_cols(acc, gain_ref, c).T.astype(qt_ref.dtype)
        qt_ref[cols, rows] = qt
        gate = jnp.dot(kmean_ref[:, cols].astype(BF16), qt, preferred_element_type=F32)
        pen_ref[c * LANES:c * LANES + nb, rows] = _block_gate(gate, own).astype(pen_ref.dtype)
        pen_ref[c * LANES + nb:(c + 1) * LANES, rows] = jnp.zeros((LANES - nb, n_rows), pen_ref.dtype)


def _mm_query_gate(a, w, layer, gain_row, kmean):
    m = a.shape[0]
    n = w.shape[2]
    nb = kmean.shape[0]
    tm, tn = PROJ_TM, PROJ_TN
    assert nb <= LANES and LANES == HEAD_DIM
    transposed = pl.BlockSpec((tn, tm), lambda j, i: (j, i))
    return _wmatmul(
        _query_gate_epilogue, a, [(w, layer, 0)],
        [(gain_row, _gain_spec(tn)), (kmean, pl.BlockSpec((nb, tn), lambda j, i: (0, j)))],
        [(jax.ShapeDtypeStruct((n, m), BF16), transposed),
         (jax.ShapeDtypeStruct((n, m), BF16), transposed)],
        n_cols=n, tm=tm, tn=tn, name="mm_query_gate")


def _key_epilogue(accs, extra_refs, out_refs, rows):
    (acc,), (gain_ref,), (kx_ref, mean_ref) = accs, extra_refs, out_refs
    n_rows, tn = acc.shape
    assert rows.start % MOBA_BLOCK == 0 and n_rows % MOBA_BLOCK == 0
    row = (pl.program_id(1) * kx_ref.shape[0] + rows.start
           + lax.broadcasted_iota(jnp.int32, (n_rows, LANES), 0))
    onehot = (row // MOBA_BLOCK == lax.broadcasted_iota(jnp.int32, (n_rows, LANES), 1)
              ).astype(kx_ref.dtype)
    for c in range(tn // HEAD_DIM):
        cols = slice(c * HEAD_DIM, (c + 1) * HEAD_DIM)
        t = _head_rms_cols(acc, gain_ref, c)
        kx_ref[rows, 2 * c * HEAD_DIM:(2 * c + 1) * HEAD_DIM] = t.astype(kx_ref.dtype)
        kx_ref[rows, (2 * c + 1) * HEAD_DIM:(2 * c + 2) * HEAD_DIM] = onehot
        for r in range(n_rows // MOBA_BLOCK):
            blk = t[r * MOBA_BLOCK:(r + 1) * MOBA_BLOCK, :]
            r_out = rows.start // MOBA_BLOCK + r
            mean_ref[r_out:r_out + 1, cols] = jnp.mean(blk, axis=0, keepdims=True)


def _transpose_epilogue(accs, extra_refs, out_refs, rows):
    out_refs[0][:, rows] = accs[0].T.astype(out_refs[0].dtype)


def _mm_shared_kv(a, w_kv, gain_row):
    m = a.shape[0]
    d = w_kv.shape[1] // 2
    w = w_kv[None]
    tm, tn = PROJ_TM, PROJ_TN
    rows = tm // MOBA_BLOCK
    assert m // MOBA_BLOCK <= LANES
    kn, kmean = _wmatmul(
        _key_epilogue, a, [(w, 0, 0)], [(gain_row, _gain_spec(tn))],
        [(jax.ShapeDtypeStruct((m, 2 * d), BF16), pl.BlockSpec((tm, 2 * tn), lambda j, i: (i, j))),
         (jax.ShapeDtypeStruct((m // tm, rows, d), F32),
          pl.BlockSpec((None, rows, tn), lambda j, i: (i, 0, j)))],
        n_cols=d, tm=tm, tn=tn, name="mm_keys")
    tile = MOBA_TILE_BLOCKS * MOBA_BLOCK
    per_tile = tile // tm
    assert tile % tm == 0
    (vt,) = _wmatmul(
        _transpose_epilogue, a, [(w, 0, d // tn)], [],
        [(jax.ShapeDtypeStruct((m // tile, d, tile), BF16),
          pl.BlockSpec((None, tn, tm), lambda j, i: (i // per_tile, j, i % per_tile)))],
        n_cols=d, tm=tm, tn=tn, name="mm_values_t")
    return kn, kmean.reshape(m // MOBA_BLOCK, d), vt


def _residual_epilogue(accs, extra_refs, out_refs, rows):
    out_refs[0][rows, :] = extra_refs[0][rows, :] + accs[0]


def _mm_residual(a, w, layer, res, tm, tn):
    m = a.shape[0]
    n = w.shape[2]
    tile = pl.BlockSpec((tm, tn), lambda j, i: (i, j))
    (out,) = _wmatmul(_residual_epilogue, a, [(w, layer, 0)], [(res, tile)],
                      [(jax.ShapeDtypeStruct((m, n), F32), tile)],
                      n_cols=n, tm=tm, tn=tn, name="mm_residual")
    return out


def _swiglu_epilogue(accs, extra_refs, out_refs, rows):
    g, u = accs
    out_refs[0][rows, :] = (g * (1.0 / (1.0 + jnp.exp(-g))) * u).astype(out_refs[0].dtype)


def _mm_swiglu(a, w_gate_up, layer):
    m = a.shape[0]
    tm, tn = 1024, 256
    (out,) = _wmatmul(
        _swiglu_epilogue, a, [(w_gate_up, layer, 0), (w_gate_up, layer, D_FF // tn)], [],
        [(jax.ShapeDtypeStruct((m, D_FF), BF16), pl.BlockSpec((tm, tn), lambda j, i: (i, j)))],
        n_cols=D_FF, tm=tm, tn=tn, name="mm_swiglu")
    return out


CAST_ROWS = 256


def _cast_rider_specs(w, layer, n_inner):
    _, k, n = w.shape
    assert k % CAST_ROWS == 0
    last = k // CAST_ROWS - 1

    def block(a, b):
        return jnp.minimum(a * n_inner + b, last)

    return (pl.BlockSpec((None, CAST_ROWS, n), lambda a, b: (layer, block(a, b), 0)),
            pl.BlockSpec((CAST_ROWS, n), lambda a, b: (block(a, b), 0)),
            jax.ShapeDtypeStruct((k, n), BF16))


def _cast_rider(w_ref, wb_ref, n_blocks):
    step = pl.program_id(0) * pl.num_programs(1) + pl.program_id(1)

    @pl.when(step < n_blocks)
    def _():
        wb_ref[...] = w_ref[...].astype(wb_ref.dtype)


def _dilated_kernel(q0_ref, q1_ref, q2_ref, k_ref, v_ref, bias_ref, w_ref, o_ref, wb_ref, *scratch,
                    n_cast_blocks):
    og, mg, lg = scratch[0:3], scratch[3:6], scratch[6:9]
    u = pl.program_id(1)
    q_refs = (q0_ref, q1_ref, q2_ref)
    _cast_rider(w_ref, wb_ref, n_cast_blocks)

    for g, (_, d) in enumerate(DILATED_PATTERNS):
        shift = d.bit_length() - 1

        def block(b, g=g, d=d, shift=shift):
            r = jnp.bitwise_and(b, d - 1)
            nl = jnp.right_shift(b, shift)
            rel = r + (BAND * d) * nl
            rows = pl.ds(rel, BAND, stride=d) if d > 1 else pl.ds(rel, BAND)
            q = q_refs[g][rows, :].astype(BF16)
            first = jnp.logical_and(u == 0, nl == 0)
            ks = jnp.where(first, r, u * UNIT + rel - BAND * d)
            krows = pl.ds(ks, 2 * BAND, stride=d) if d > 1 else pl.ds(ks, 2 * BAND)
            k = k_ref[krows, :].astype(BF16)
            v = v_ref[krows, :].astype(BF16)
            s = lax.dot_general(q, k, (((1,), (1,)), ((), ())), preferred_element_type=F32)
            s = s + bias_ref[g, first.astype(jnp.int32)]
            m = jnp.max(s, axis=1, keepdims=True)
            p = jnp.exp2(s - m)
            l = jnp.sum(p, axis=1, keepdims=True)
            o = jnp.dot(p.astype(BF16), v, preferred_element_type=F32)
            og[g][rows, :] = o
            mg[g][rows, :] = jnp.broadcast_to(m, (BAND, HEAD_DIM))
            lg[g][rows, :] = jnp.broadcast_to(l, (BAND, HEAD_DIM))

        def blocks(it, carry, block=block):
            for c in range(DIL_UNROLL):
                block(it * DIL_UNROLL + c)
            return carry

        lax.fori_loop(0, BLOCKS_PER_UNIT // DIL_UNROLL, blocks, 0)

    chunk = 256

    def merge(c, carry):
        rows = pl.ds(pl.multiple_of(c * chunk, chunk), chunk)
        ms = [mg[g][rows, :] for g in range(N_GROUPS)]
        top = jnp.maximum(jnp.maximum(ms[0], ms[1]), ms[2])
        num = jnp.zeros((chunk, HEAD_DIM), F32)
        den = jnp.zeros((chunk, HEAD_DIM), F32)
        for g in range(N_GROUPS):
            w = jnp.exp2(ms[g] - top)
            num = num + w * og[g][rows, :]
            den = den + w * lg[g][rows, :]
        o_ref[rows, :] = (num / den).astype(o_ref.dtype)
        return carry

    lax.fori_loop(0, UNIT // chunk, merge, 0)


def _dilated_attention(qk, v, bias, w_cast, cast_layer):
    s = qk.shape[0]
    assert s % UNIT == 0
    n_units = s // UNIT
    w_in, w_out, w_shape = _cast_rider_specs(w_cast, cast_layer, n_units)
    n_cast_blocks = w_cast.shape[1] // CAST_ROWS
    assert n_cast_blocks <= N_HEADS * n_units
    hcols = N_HEADS
    q_specs = [pl.BlockSpec((UNIT, HEAD_DIM), lambda h, u, g=g: (u, g * hcols + h))
               for g in range(N_GROUPS)]
    k_spec = pl.BlockSpec((s, HEAD_DIM), lambda h, u: (0, N_GROUPS * hcols + h))
    v_spec = pl.BlockSpec((s, HEAD_DIM), lambda h, u: (0, h))
    b_spec = pl.BlockSpec((None, N_GROUPS, 2, BAND, 2 * BAND), lambda h, u: (h, 0, 0, 0, 0))
    blk = UNIT * HEAD_DIM * 4
    vmem = 2 * (3 * blk + 2 * s * HEAD_DIM * 4 + N_GROUPS * 2 * BAND * 2 * BAND * 4 + blk // 2) \
        + 9 * blk + 2 * CAST_ROWS * w_cast.shape[2] * 6 + 8 * 2**20
    return pl.pallas_call(
        functools.partial(_dilated_kernel, n_cast_blocks=n_cast_blocks),
        out_shape=[jax.ShapeDtypeStruct((s, D_MODEL), BF16), w_shape],
        grid=(N_HEADS, n_units),
        in_specs=q_specs + [k_spec, v_spec, b_spec, w_in],
        out_specs=[pl.BlockSpec((UNIT, HEAD_DIM), lambda h, u: (u, h)), w_out],
        scratch_shapes=[pltpu.VMEM((UNIT, HEAD_DIM), F32)] * 9,
        compiler_params=_params(("arbitrary", "arbitrary"), vmem),
        name="dilated_attention",
    )(qk, qk, qk, qk, v, bias, w_cast)


def _moba_kernel(qt_ref, pen_ref, qt_next_ref, pen_next_ref, kx_ref, vt_ref, bias_ref, w_ref,
                 o_ref, wb_ref, s0_ref, s1_ref, p0_ref, p1_ref, *, n_cast_blocks):
    step = pl.program_id(1)
    i = step * MOBA_QUERY_BLOCKS
    i_last = i + MOBA_QUERY_BLOCKS - 1
    width = MOBA_QUERY_BLOCKS * MOBA_BLOCK
    tile = MOBA_TILE_BLOCKS * MOBA_BLOCK
    last_tile = vt_ref.shape[0] - 1
    _cast_rider(w_ref, wb_ref, n_cast_blocks)
    qx = jnp.concatenate([qt_ref[...], pen_ref[...]], axis=0)
    qx_next = jnp.concatenate([qt_next_ref[...], pen_next_ref[...]], axis=0)

    def scores(t, s_ref, qx):
        t = jnp.minimum(t, last_tile)
        rows = pl.ds(pl.multiple_of(t * tile, tile), tile)
        s_ref[...] = jnp.dot(kx_ref[rows, :], qx, preferred_element_type=F32)

    def value_dot(t, p_ref):
        return jnp.dot(vt_ref[jnp.clip(t, 0, last_tile)], p_ref[...], preferred_element_type=F32)

    def update(t, s_ref, p_ref, p_prev_ref, carry, with_bias):
        m, l, acc, alpha_prev = carry
        pv = value_dot(t - 1, p_prev_ref)
        if with_bias:
            def bias_tile(c, a):
                return bias_ref[jnp.clip(i + a - (t * MOBA_TILE_BLOCKS + c), 0, N_BIAS_TILES - 1)]
            s_ref[...] += jnp.concatenate(
                [jnp.concatenate([bias_tile(c, a) for a in range(MOBA_QUERY_BLOCKS)], axis=1)
                 for c in range(MOBA_TILE_BLOCKS)], axis=0)
        m_new = jnp.maximum(m, jnp.max(s_ref[...], axis=0, keepdims=True))
        alpha = jnp.exp2(m - m_new)
        p = jnp.exp2(s_ref[...] - m_new)
        l = alpha * l + jnp.sum(p, axis=0, keepdims=True)
        p_ref[...] = p.astype(BF16)
        return m_new, l, alpha_prev * acc + pv, alpha

    n_tiles = (i_last + MOBA_TILE_BLOCKS) // MOBA_TILE_BLOCKS
    n_far_pairs = jnp.maximum(i - (N_BIAS_TILES - 2), 0) // (2 * MOBA_TILE_BLOCKS)
    n_pairs = n_tiles // 2
    odd = n_tiles % 2

    def tile_pair(u, carry, with_bias):
        t0 = 2 * u
        scores(t0 + 1, s1_ref, qx)
        carry = update(t0, s0_ref, p0_ref, p1_ref, carry, with_bias)
        if with_bias:
            is_last = jnp.logical_and(u == n_pairs - 1, odd == 0)
            scores(jnp.where(is_last, 0, t0 + 2), s0_ref, jnp.where(is_last, qx_next, qx))
        else:
            scores(t0 + 2, s0_ref, qx)
        return update(t0 + 1, s1_ref, p1_ref, p0_ref, carry, with_bias)

    def odd_tile(_, carry):
        carry = update(n_tiles - 1, s0_ref, p0_ref, p1_ref, carry, True)
        scores(0, s0_ref, qx_next)
        return carry

    @pl.when(step == 0)
    def _():
        scores(0, s0_ref, qx)

    p1_ref[...] = jnp.zeros(p1_ref.shape, p1_ref.dtype)
    carry = (jnp.full((1, width), NEG, F32), jnp.zeros((1, width), F32),
             jnp.zeros((HEAD_DIM, width), F32), jnp.ones((1, width), F32))
    carry = lax.fori_loop(0, n_far_pairs, functools.partial(tile_pair, with_bias=False), carry)
    carry = lax.fori_loop(n_far_pairs, n_pairs, functools.partial(tile_pair, with_bias=True), carry)
    _, l, acc, alpha = lax.fori_loop(0, odd, odd_tile, carry)
    p_last = jnp.where(odd == 1, p0_ref[...], p1_ref[...])
    acc = alpha * acc + jnp.dot(vt_ref[n_tiles - 1], p_last, preferred_element_type=F32)
    o_ref[...] = (acc / l).T.astype(o_ref.dtype)


def _moba_attention(qt, pen, kx, vt, bias, w_cast, cast_layer):
    s = kx.shape[0]
    nb = s // MOBA_BLOCK
    tile = MOBA_TILE_BLOCKS * MOBA_BLOCK
    assert nb % (2 * MOBA_TILE_BLOCKS) == 0 and nb <= LANES and vt.shape == (s // tile, D_MODEL, tile)
    kv_blk = s * HEAD_DIM * 2
    width = MOBA_QUERY_BLOCKS * MOBA_BLOCK
    n_steps = nb // MOBA_QUERY_BLOCKS
    assert nb % MOBA_QUERY_BLOCKS == 0
    w_in, w_out, w_shape = _cast_rider_specs(w_cast, cast_layer, n_steps)
    n_cast_blocks = w_cast.shape[1] // CAST_ROWS
    assert n_cast_blocks <= N_HEADS * n_steps
    vmem = 2 * (3 * kv_blk + N_BIAS_TILES * MOBA_BLOCK * MOBA_BLOCK * 4) + 12 * tile * width \
        + 2 * CAST_ROWS * w_cast.shape[2] * 6 + 24 * 2**20
    cur = lambda h, i: (h, i)
    nxt = lambda h, i: (h, jnp.minimum(i + 1, n_steps - 1))
    return pl.pallas_call(
        functools.partial(_moba_kernel, n_cast_blocks=n_cast_blocks),
        out_shape=[jax.ShapeDtypeStruct((s, D_MODEL), BF16), w_shape],
        grid=(N_HEADS, n_steps),
        in_specs=[pl.BlockSpec((HEAD_DIM, width), cur),
                  pl.BlockSpec((LANES, width), cur),
                  pl.BlockSpec((HEAD_DIM, width), nxt),
                  pl.BlockSpec((LANES, width), nxt),
                  pl.BlockSpec((s, 2 * HEAD_DIM), lambda h, i: (0, h)),
                  pl.BlockSpec((s // tile, HEAD_DIM, tile), lambda h, i: (0, h, 0)),
                  pl.BlockSpec((None, N_BIAS_TILES, MOBA_BLOCK, MOBA_BLOCK),
                               lambda h, i: (h, 0, 0, 0)),
                  w_in],
        out_specs=[pl.BlockSpec((width, HEAD_DIM), lambda h, i: (i, h)), w_out],
        scratch_shapes=[pltpu.VMEM((tile, width), F32)] * 2
        + [pltpu.VMEM((tile, width), BF16)] * 2,
        compiler_params=_params(("arbitrary", "arbitrary"), vmem),
        name="moba_attention",
    )(qt, pen, qt, pen, kx, vt, bias, w_cast)


def _ffn(x, norm_gain, w_gate_up, layer, w_down_bf16):
    (h,) = _rmsnorm(x, norm_gain[None, :])
    act = _mm_swiglu(h, w_gate_up, layer)
    return _mm_residual(act, w_down_bf16[None], 0, x, tm=512, tn=512)


def kernel(x, rel_bias, attn_norm, ffn_norm, w_qkv_a, q_norm_a, k_norm_a, w_o_a, kv_norm, w_kv_b, k_norm_b, w_q_b, q_norm_b, w_o_b, w_gate_up, w_down):
    assert x.shape[0] == 1 and x.shape[2] == D_MODEL
    xs = x[0].astype(F32)
    d = D_MODEL

    table = rel_bias.astype(F32) * LOG2E
    bias_dil = _bias_expand(_dilated_bucket_tiles(), table)
    bias_moba = _bias_expand(_moba_bucket_tiles(), table - table[N_BUCKETS - 1:, :])

    (h,) = _rmsnorm(xs, attn_norm[0][None, :])
    gain_a = jnp.concatenate(
        [jnp.tile(q_norm_a[0][g].astype(F32) * (SCALE * LOG2E), N_HEADS) for g in range(N_GROUPS)]
        + [jnp.tile(k_norm_a[0].astype(F32), N_HEADS)])[None, :]
    qk, v = _mm_headnorm(h, w_qkv_a, 0, gain_a, (N_GROUPS + 1) * d, F32)
    o, w_down_0 = _dilated_attention(qk, v, bias_dil, w_down, 0)
    xs = _mm_residual(o, w_o_a, 0, xs, tm=RES_TM, tn=RES_TN)
    xs = _ffn(xs, ffn_norm[0], w_gate_up, 0, w_down_0)

    hk, hq = _rmsnorm(xs, jnp.stack([kv_norm, attn_norm[1]]))
    gain_k = jnp.tile(k_norm_b.astype(F32), N_HEADS)[None, :]
    k, kmean, vt = _mm_shared_kv(hk, w_kv_b, gain_k)
    gain_q = jnp.tile(q_norm_b[0].astype(F32) * (SCALE * LOG2E), N_HEADS)[None, :]
    qt, pen = _mm_query_gate(hq, w_q_b, 0, gain_q, kmean)
    o, w_down_1 = _moba_attention(qt, pen, k, vt, bias_moba, w_down, 1)
    xs = _mm_residual(o, w_o_b, 0, xs, tm=RES_TM, tn=RES_TN)
    xs = _ffn(xs, ffn_norm[1], w_gate_up, 1, w_down_1)
    return xs[None].astype(x.dtype)
```

```python
import functools
import math

import numpy as np
import jax
import jax.numpy as jnp
from jax import lax
from jax.experimental import pallas as pl
from jax.experimental.pallas import tpu as pltpu

D_MODEL = 4096
HEAD_DIM = 128
N_HEADS = D_MODEL // HEAD_DIM
DILATED_PATTERNS = ((128, 1), (512, 4), (2048, 16))
N_GROUPS = len(DILATED_PATTERNS)
BAND = 128
MOBA_BLOCK = 256
MOBA_TOPK = 3
N_BUCKETS = 32
MAX_DISTANCE = 2048
D_FF = 11008
EPS = 1e-6
NEG = -1e30
SCALE = HEAD_DIM ** -0.5
LOG2E = math.log2(math.e)

UNIT = 2048
BLOCKS_PER_UNIT = UNIT // BAND
DIL_UNROLL = 16
N_BIAS_TILES = 8
MOBA_TILE_BLOCKS = 4
MOBA_QUERY_BLOCKS = 2

V7X_SCOPED_VMEM_BYTES = 60000 * 1024
LANES = 128

F32 = jnp.float32
BF16 = jnp.bfloat16


def _params(semantics, vmem_bytes):
    return pltpu.CompilerParams(dimension_semantics=semantics,
                                vmem_limit_bytes=min(int(vmem_bytes), V7X_SCOPED_VMEM_BYTES))


def _rel_bucket_np(dist):
    n = np.maximum(dist, 0)
    exact = N_BUCKETS // 2
    nf = np.maximum(n, exact).astype(np.float64)
    large = exact + (np.log(nf / exact) / math.log(MAX_DISTANCE / exact)
                     * (N_BUCKETS - exact)).astype(np.int64)
    return np.where(n < exact, n, np.minimum(large, N_BUCKETS - 1)).astype(np.int32)


MASKED = N_BUCKETS


def _dilated_bucket_tiles():
    i = np.arange(BAND)[:, None]
    j = np.arange(2 * BAND)[None, :]
    tiles = []
    for window, d in DILATED_PATTERNS:
        w_sub = window // d
        per = []
        for first in (False, True):
            dist = (i - j) if first else (BAND + i - j)
            ok = (dist >= 0) & (dist <= w_sub)
            per.append(np.where(ok, _rel_bucket_np(dist * d), MASKED))
        tiles.append(np.stack(per))
    return np.stack(tiles).astype(np.int32)


def _moba_bucket_tiles():
    i = np.arange(MOBA_BLOCK)[None, :]
    j = np.arange(MOBA_BLOCK)[:, None]
    tiles = []
    for delta in range(N_BIAS_TILES - 1):
        dist = delta * MOBA_BLOCK + i - j
        b = _rel_bucket_np(dist)
        tiles.append(np.where(dist >= 0, b, MASKED) if delta == 0 else b)
    far = (N_BIAS_TILES - 2) * MOBA_BLOCK + 1
    assert _rel_bucket_np(np.array([far]))[0] == N_BUCKETS - 1
    tiles.append(np.full((MOBA_BLOCK, MOBA_BLOCK), N_BUCKETS - 1))
    return np.stack(tiles).astype(np.int32)


TABLE_ROWS = 64


def _bias_expand_kernel(idx_ref, tab_ref, o_ref):
    idx = idx_ref[...]
    rows = lax.broadcasted_iota(jnp.int32, (TABLE_ROWS, idx.shape[1]), 0)
    onehot = (rows == idx).astype(BF16)
    tab = tab_ref[...]
    hi = tab.astype(BF16)
    rest = tab - hi.astype(F32)
    mid = rest.astype(BF16)
    lo = (rest - mid.astype(F32)).astype(BF16)
    hi, mid, lo = (jnp.dot(piece, onehot, preferred_element_type=F32) for piece in (hi, mid, lo))
    o_ref[...] = (hi + mid) + lo


def _bias_expand(idx_np, table):
    idx_flat = jnp.asarray(idx_np.ravel())
    n = idx_flat.shape[0]
    tn = 8192
    assert n % tn == 0
    tab = jnp.zeros((N_HEADS, TABLE_ROWS), F32).at[:, :N_BUCKETS].set(table.astype(F32).T)
    tab = tab.at[:, MASKED].set(NEG)
    return pl.pallas_call(
        _bias_expand_kernel,
        out_shape=jax.ShapeDtypeStruct((N_HEADS, n), F32),
        grid=(n // tn,),
        in_specs=[pl.BlockSpec((1, tn), lambda i: (0, i)),
                  pl.BlockSpec((N_HEADS, TABLE_ROWS), lambda i: (0, 0))],
        out_specs=pl.BlockSpec((N_HEADS, tn), lambda i: (0, i)),
        compiler_params=_params(("parallel",), 16 * 2**20),
        name="bias_expand",
    )(idx_flat.reshape(1, n), tab).reshape((N_HEADS,) + idx_np.shape)


def _rmsnorm_kernel(x_ref, g_ref, *o_refs):
    x = x_ref[...]
    y = x * lax.rsqrt(jnp.mean(x * x, axis=-1, keepdims=True) + EPS)
    for n, o_ref in enumerate(o_refs):
        o_ref[...] = (y * g_ref[n:n + 1, :]).astype(o_ref.dtype)


def _rmsnorm(x, gains):
    s, d = x.shape
    n_out = gains.shape[0]
    tr = 256
    outs = pl.pallas_call(
        _rmsnorm_kernel,
        out_shape=[jax.ShapeDtypeStruct((s, d), BF16)] * n_out,
        grid=(s // tr,),
        in_specs=[pl.BlockSpec((tr, d), lambda i: (i, 0)),
                  pl.BlockSpec((n_out, d), lambda i: (0, 0))],
        out_specs=[pl.BlockSpec((tr, d), lambda i: (i, 0))] * n_out,
        compiler_params=_params(("parallel",), 2 * tr * d * (4 + 2 * n_out) + 8 * 2**20),
        name="rmsnorm",
    )(x, gains.astype(F32))
    return outs


def _head_rms_cols(acc, gain_ref, c):
    t = acc[:, c * HEAD_DIM:(c + 1) * HEAD_DIM]
    t = t * lax.rsqrt(jnp.mean(t * t, axis=-1, keepdims=True) + EPS)
    return t * gain_ref[:, c * HEAD_DIM:(c + 1) * HEAD_DIM]


def _wmatmul_kernel(*refs, epilogue, n_w, n_extra, n_out, cast, scaled):
    a_ref = refs[0]
    w_refs = refs[1:1 + n_w]
    extra_refs = refs[1 + n_w:1 + n_w + n_extra]
    out_refs = refs[1 + n_w + n_extra:1 + n_w + n_extra + n_out]
    if scaled:
        scale_ref, *extra_refs = extra_refs
    if cast:
        wb_refs = refs[1 + n_w + n_extra + n_out:]

        @pl.when(pl.program_id(1) == 0)
        def _():
            for w_ref, wb_ref in zip(w_refs, wb_refs):
                wb_ref[...] = w_ref[...].astype(BF16)
    else:
        wb_refs = w_refs
    half = a_ref.shape[0] // MM_ROW_SPLIT
    for h in range(MM_ROW_SPLIT):
        rows = slice(h * half, (h + 1) * half)
        a = a_ref[rows, :]
        accs = [jnp.dot(a, wb_ref[...], preferred_element_type=F32) for wb_ref in wb_refs]
        if scaled:
            scale = jnp.tile(scale_ref[rows, :], (1, accs[0].shape[1] // LANES))
            accs = [acc * scale for acc in accs]
        epilogue(accs, extra_refs, out_refs, rows)


def _wmatmul(epilogue, a, weights, extra, outs, *, n_cols, tm, tn, name, row_scale=None):
    m, k = a.shape
    assert m % tm == 0 and n_cols % tn == 0
    cast = weights[0][0].dtype != BF16
    if row_scale is not None:
        extra = [(row_scale, pl.BlockSpec((tm, LANES), lambda j, i: (i, 0)))] + list(extra)
    w_specs = [pl.BlockSpec((None, k, tn), lambda j, i, layer=layer, off=off: (layer, 0, j + off))
               for _, layer, off in weights]
    wsz = weights[0][0].dtype.itemsize
    blocks = tm * k * 2 + len(weights) * k * tn * wsz
    for arr, spec in list(extra) + [(o, s) for o, s in outs]:
        blocks += math.prod(d for d in spec.block_shape if d is not None) * jnp.dtype(arr.dtype).itemsize
    vmem = 2 * blocks + (len(weights) * k * tn * 2 if cast else 0) + (2 + 2 * len(weights)) * tm * tn * 4
    return pl.pallas_call(
        functools.partial(_wmatmul_kernel, epilogue=epilogue, n_w=len(weights), n_extra=len(extra),
                          n_out=len(outs), cast=cast, scaled=row_scale is not None),
        out_shape=[o for o, _ in outs],
        grid=(n_cols // tn, m // tm),
        in_specs=[pl.BlockSpec((tm, k), lambda j, i: (i, 0))] + w_specs + [s for _, s in extra],
        out_specs=[s for _, s in outs],
        scratch_shapes=[pltpu.VMEM((k, tn), BF16)] * len(weights) if cast else [],
        compiler_params=_params(("parallel", "arbitrary"), vmem),
        name=name,
    )(a, *[w for w, _, _ in weights], *[arr for arr, _ in extra])


PROJ_TM, PROJ_TN = 512, 1024
RES_TM, RES_TN = 1024, 512
MM_ROW_SPLIT = 2


def _gain_spec(tn):
    return pl.BlockSpec((1, tn), lambda j, i: (0, j))


def _headnorm_epilogue(accs, extra_refs, out_refs, rows):
    (acc,), (gain_ref,), (o_ref,) = accs, extra_refs, out_refs
    for c in range(acc.shape[1] // HEAD_DIM):
        t = _head_rms_cols(acc, gain_ref, c)
        o_ref[rows, c * HEAD_DIM:(c + 1) * HEAD_DIM] = t.astype(o_ref.dtype)


def _store_epilogue(accs, extra_refs, out_refs, rows):
    out_refs[0][rows, :] = accs[0].astype(out_refs[0].dtype)


def _mm_headnorm(a, w, layer, gain_row, n_norm_cols, out_dtype):
    m = a.shape[0]
    n = w.shape[2]
    tm, tn = PROJ_TM, PROJ_TN
    assert n_norm_cols % tn == 0
    tile = pl.BlockSpec((tm, tn), lambda j, i: (i, j))
    (normed,) = _wmatmul(
        _headnorm_epilogue, a, [(w, layer, 0)], [(gain_row, _gain_spec(tn))],
        [(jax.ShapeDtypeStruct((m, n_norm_cols), out_dtype), tile)],
        n_cols=n_norm_cols, tm=tm, tn=tn, name="mm_headnorm")
    (rest,) = _wmatmul(
        _store_epilogue, a, [(w, layer, n_norm_cols // tn)], [],
        [(jax.ShapeDtypeStruct((m, n - n_norm_cols), out_dtype), tile)],
        n_cols=n - n_norm_cols, tm=tm, tn=tn, name="mm_plain")
    return normed, rest


def _block_gate(gate, own):
    blk = lax.broadcasted_iota(jnp.int32, gate.shape, 0)
    blk_f = blk.astype(F32)
    past = blk < own
    gm = jnp.where(past, gate, NEG)
    picked = jnp.zeros(gate.shape, jnp.bool_)
    for _ in range(MOBA_TOPK):
        mx = jnp.max(gm, axis=0, keepdims=True)
        first = jnp.min(jnp.where(gm == mx, blk_f, float(LANES)), axis=0, keepdims=True)
        pick = blk_f == first
        picked = jnp.logical_or(picked, pick)
        gm = jnp.where(pick, -jnp.inf, gm)
    allowed = jnp.logical_or(jnp.logical_and(picked, past), blk == own)
    return jnp.where(allowed, 0.0, NEG)


def _query_gate_epilogue(accs, extra_refs, out_refs, rows):
    (acc,), (gain_ref, kmean_ref), (qt_ref, pen_ref) = accs, extra_refs, out_refs
    n_rows = acc.shape[0]
    nb = kmean_ref.shape[0]
    query = (pl.program_id(1) * qt_ref.shape[1] + rows.start
             + lax.broadcasted_iota(jnp.int32, (nb, n_rows), 1))
    own = query // MOBA_BLOCK
    for c in range(acc.shape[1] // HEAD_DIM):
        cols = slice(c * HEAD_DIM, (c + 1) * HEAD_DIM)
        qt = _head_rms_cols(acc, gain_ref, c).T.astype(qt_ref.dtype)
        qt_ref[cols, rows] = qt
        gate = jnp.dot(kmean_ref[:, cols].astype(BF16), qt, preferred_element_type=F32)
        pen_ref[c * LANES:c * LANES + nb, rows] = _block_gate(gate, own).astype(pen_ref.dtype)
        pen_ref[c * LANES + nb:(c + 1) * LANES, rows] = jnp.zeros((LANES - nb, n_rows), pen_ref.dtype)


def _mm_query_gate(a, w, layer, gain_row, kmean, row_scale):
    m = a.shape[0]
    n = w.shape[2]
    nb = kmean.shape[0]
    tm, tn = PROJ_TM, PROJ_TN
    assert nb <= LANES and LANES == HEAD_DIM
    transposed = pl.BlockSpec((tn, tm), lambda j, i: (j, i))
    return _wmatmul(
        _query_gate_epilogue, a, [(w, layer, 0)],
        [(gain_row, _gain_spec(tn)), (kmean, pl.BlockSpec((nb, tn), lambda j, i: (0, j)))],
        [(jax.ShapeDtypeStruct((n, m), BF16), transposed),
         (jax.ShapeDtypeStruct((n, m), BF16), transposed)],
        n_cols=n, tm=tm, tn=tn, name="mm_query_gate", row_scale=row_scale)


def _key_epilogue(accs, extra_refs, out_refs, rows):
    (acc,), (gain_ref,), (kx_ref, mean_ref) = accs, extra_refs, out_refs
    n_rows, tn = acc.shape
    assert rows.start % MOBA_BLOCK == 0 and n_rows % MOBA_BLOCK == 0
    row = (pl.program_id(1) * kx_ref.shape[0] + rows.start
           + lax.broadcasted_iota(jnp.int32, (n_rows, LANES), 0))
    onehot = (row // MOBA_BLOCK == lax.broadcasted_iota(jnp.int32, (n_rows, LANES), 1)
              ).astype(kx_ref.dtype)
    for c in range(tn // HEAD_DIM):
        cols = slice(c * HEAD_DIM, (c + 1) * HEAD_DIM)
        t = _head_rms_cols(acc, gain_ref, c)
        kx_ref[rows, 2 * c * HEAD_DIM:(2 * c + 1) * HEAD_DIM] = t.astype(kx_ref.dtype)
        kx_ref[rows, (2 * c + 1) * HEAD_DIM:(2 * c + 2) * HEAD_DIM] = onehot
        for r in range(n_rows // MOBA_BLOCK):
            blk = t[r * MOBA_BLOCK:(r + 1) * MOBA_BLOCK, :]
            r_out = rows.start // MOBA_BLOCK + r
            mean_ref[r_out:r_out + 1, cols] = jnp.mean(blk, axis=0, keepdims=True)


def _transpose_epilogue(accs, extra_refs, out_refs, rows):
    out_refs[0][:, rows] = accs[0].T.astype(out_refs[0].dtype)


def _mm_shared_kv(a, w_kv, gain_row, row_scale):
    m = a.shape[0]
    d = w_kv.shape[1] // 2
    w = w_kv[None]
    tm, tn = PROJ_TM, PROJ_TN
    rows = tm // MOBA_BLOCK
    assert m // MOBA_BLOCK <= LANES
    kn, kmean = _wmatmul(
        _key_epilogue, a, [(w, 0, 0)], [(gain_row, _gain_spec(tn))],
        [(jax.ShapeDtypeStruct((m, 2 * d), BF16), pl.BlockSpec((tm, 2 * tn), lambda j, i: (i, j))),
         (jax.ShapeDtypeStruct((m // tm, rows, d), F32),
          pl.BlockSpec((None, rows, tn), lambda j, i: (i, 0, j)))],
        n_cols=d, tm=tm, tn=tn, name="mm_keys", row_scale=row_scale)
    tile = MOBA_TILE_BLOCKS * MOBA_BLOCK
    per_tile = tile // tm
    assert tile % tm == 0
    (vt,) = _wmatmul(
        _transpose_epilogue, a, [(w, 0, d // tn)], [],
        [(jax.ShapeDtypeStruct((m // tile, d, tile), BF16),
          pl.BlockSpec((None, tn, tm), lambda j, i: (i // per_tile, j, i % per_tile)))],
        n_cols=d, tm=tm, tn=tn, name="mm_values_t", row_scale=row_scale)
    return kn, kmean.reshape(m // MOBA_BLOCK, d), vt


def _residual_epilogue(accs, extra_refs, out_refs, rows):
    res_ref, *gain_refs = extra_refs
    x_ref, *norm_refs = out_refs
    x = res_ref[rows, :] + accs[0]
    x_ref[rows, :] = x
    if gain_refs:
        (gain_ref,) = gain_refs
        *scaled_refs, ssq_ref = norm_refs
        for n, scaled_ref in enumerate(scaled_refs):
            scaled_ref[rows, :] = (x * gain_ref[n:n + 1, :]).astype(scaled_ref.dtype)
        ssq = jnp.sum(x * x, axis=1, keepdims=True)
        ssq_ref[rows, :] = jnp.broadcast_to(ssq, (x.shape[0], LANES))


def _rstd_kernel(ssq_ref, o_ref, *, width):
    ssq = ssq_ref[...]
    total = ssq[:, 0:LANES]
    for c in range(1, ssq.shape[1] // LANES):
        total = total + ssq[:, c * LANES:(c + 1) * LANES]
    o_ref[...] = lax.rsqrt(total / width + EPS)


def _mm_residual(a, w, layer, res, tm, tn, gains=None):
    m = a.shape[0]
    n = w.shape[2]
    tile = pl.BlockSpec((tm, tn), lambda j, i: (i, j))
    extra = [(res, tile)]
    outs = [(jax.ShapeDtypeStruct((m, n), F32), tile)]
    if gains is not None:
        n_gains = gains.shape[0]
        extra.append((gains.astype(F32), pl.BlockSpec((n_gains, tn), lambda j, i: (0, j))))
        outs += [(jax.ShapeDtypeStruct((m, n), BF16), tile)] * n_gains
        outs.append((jax.ShapeDtypeStruct((m, n // tn * LANES), F32),
                     pl.BlockSpec((tm, LANES), lambda j, i: (i, j))))
    results = _wmatmul(_residual_epilogue, a, [(w, layer, 0)], extra, outs,
                       n_cols=n, tm=tm, tn=tn, name="mm_residual")
    if gains is None:
        return results[0]
    x, *scaled, ssq = results
    tr = 1024
    rstd = pl.pallas_call(
        functools.partial(_rstd_kernel, width=n),
        out_shape=jax.ShapeDtypeStruct((m, LANES), F32),
        grid=(m // tr,),
        in_specs=[pl.BlockSpec((tr, ssq.shape[1]), lambda i: (i, 0))],
        out_specs=pl.BlockSpec((tr, LANES), lambda i: (i, 0)),
        compiler_params=_params(("parallel",), 32 * 2**20),
        name="rstd",
    )(ssq)
    return x, scaled, rstd


def _swiglu_epilogue(accs, extra_refs, out_refs, rows):
    g, u = accs
    out_refs[0][rows, :] = (g * (1.0 / (1.0 + jnp.exp(-g))) * u).astype(out_refs[0].dtype)


def _mm_swiglu(a, w_gate_up, layer, row_scale):
    m = a.shape[0]
    tm, tn = 1024, 256
    (out,) = _wmatmul(
        _swiglu_epilogue, a, [(w_gate_up, layer, 0), (w_gate_up, layer, D_FF // tn)], [],
        [(jax.ShapeDtypeStruct((m, D_FF), BF16), pl.BlockSpec((tm, tn), lambda j, i: (i, j)))],
        n_cols=D_FF, tm=tm, tn=tn, name="mm_swiglu", row_scale=row_scale)
    return out


CAST_ROWS = 256


def _cast_rider_specs(w, layer, n_inner):
    _, k, n = w.shape
    assert k % CAST_ROWS == 0
    last = k // CAST_ROWS - 1

    def block(a, b):
        return jnp.minimum(a * n_inner + b, last)

    return (pl.BlockSpec((None, CAST_ROWS, n), lambda a, b: (layer, block(a, b), 0)),
            pl.BlockSpec((CAST_ROWS, n), lambda a, b: (block(a, b), 0)),
            jax.ShapeDtypeStruct((k, n), BF16))


def _cast_rider(w_ref, wb_ref, n_blocks):
    step = pl.program_id(0) * pl.num_programs(1) + pl.program_id(1)

    @pl.when(step < n_blocks)
    def _():
        wb_ref[...] = w_ref[...].astype(wb_ref.dtype)


def _dilated_kernel(q0_ref, q1_ref, q2_ref, k_ref, v_ref, bias_ref, w_ref, o_ref, wb_ref, *scratch,
                    n_cast_blocks):
    og, mg, lg = scratch[0:3], scratch[3:6], scratch[6:9]
    u = pl.program_id(1)
    q_refs = (q0_ref, q1_ref, q2_ref)
    _cast_rider(w_ref, wb_ref, n_cast_blocks)

    for g, (_, d) in enumerate(DILATED_PATTERNS):
        shift = d.bit_length() - 1

        def block(b, g=g, d=d, shift=shift):
            r = jnp.bitwise_and(b, d - 1)
            nl = jnp.right_shift(b, shift)
            rel = r + (BAND * d) * nl
            rows = pl.ds(rel, BAND, stride=d) if d > 1 else pl.ds(rel, BAND)
            q = q_refs[g][rows, :].astype(BF16)
            first = jnp.logical_and(u == 0, nl == 0)
            ks = jnp.where(first, r, u * UNIT + rel - BAND * d)
            krows = pl.ds(ks, 2 * BAND, stride=d) if d > 1 else pl.ds(ks, 2 * BAND)
            k = k_ref[krows, :].astype(BF16)
            v = v_ref[krows, :].astype(BF16)
            s = lax.dot_general(q, k, (((1,), (1,)), ((), ())), preferred_element_type=F32)
            s = s + bias_ref[g, first.astype(jnp.int32)]
            m = jnp.max(s, axis=1, keepdims=True)
            p = jnp.exp2(s - m)
            l = jnp.sum(p, axis=1, keepdims=True)
            o = jnp.dot(p.astype(BF16), v, preferred_element_type=F32)
            og[g][rows, :] = o
            mg[g][rows, :] = jnp.broadcast_to(m, (BAND, HEAD_DIM))
            lg[g][rows, :] = jnp.broadcast_to(l, (BAND, HEAD_DIM))

        def blocks(it, carry, block=block):
            for c in range(DIL_UNROLL):
                block(it * DIL_UNROLL + c)
            return carry

        lax.fori_loop(0, BLOCKS_PER_UNIT // DIL_UNROLL, blocks, 0)

    chunk = 256

    def merge(c, carry):
        rows = pl.ds(pl.multiple_of(c * chunk, chunk), chunk)
        ms = [mg[g][rows, :] for g in range(N_GROUPS)]
        top = jnp.maximum(jnp.maximum(ms[0], ms[1]), ms[2])
        num = jnp.zeros((chunk, HEAD_DIM), F32)
        den = jnp.zeros((chunk, HEAD_DIM), F32)
        for g in range(N_GROUPS):
            w = jnp.exp2(ms[g] - top)
            num = num + w * og[g][rows, :]
            den = den + w * lg[g][rows, :]
        o_ref[rows, :] = (num / den).astype(o_ref.dtype)
        return carry

    lax.fori_loop(0, UNIT // chunk, merge, 0)


def _dilated_attention(qk, v, bias, w_cast, cast_layer):
    s = qk.shape[0]
    assert s % UNIT == 0
    n_units = s // UNIT
    w_in, w_out, w_shape = _cast_rider_specs(w_cast, cast_layer, n_units)
    n_cast_blocks = w_cast.shape[1] // CAST_ROWS
    assert n_cast_blocks <= N_HEADS * n_units
    hcols = N_HEADS
    q_specs = [pl.BlockSpec((UNIT, HEAD_DIM), lambda h, u, g=g: (u, g * hcols + h))
               for g in range(N_GROUPS)]
    k_spec = pl.BlockSpec((s, HEAD_DIM), lambda h, u: (0, N_GROUPS * hcols + h))
    v_spec = pl.BlockSpec((s, HEAD_DIM), lambda h, u: (0, h))
    b_spec = pl.BlockSpec((None, N_GROUPS, 2, BAND, 2 * BAND), lambda h, u: (h, 0, 0, 0, 0))
    blk = UNIT * HEAD_DIM * 4
    vmem = 2 * (3 * blk + 2 * s * HEAD_DIM * 4 + N_GROUPS * 2 * BAND * 2 * BAND * 4 + blk // 2) \
        + 9 * blk + 2 * CAST_ROWS * w_cast.shape[2] * 6 + 8 * 2**20
    return pl.pallas_call(
        functools.partial(_dilated_kernel, n_cast_blocks=n_cast_blocks),
        out_shape=[jax.ShapeDtypeStruct((s, D_MODEL), BF16), w_shape],
        grid=(N_HEADS, n_units),
        in_specs=q_specs + [k_spec, v_spec, b_spec, w_in],
        out_specs=[pl.BlockSpec((UNIT, HEAD_DIM), lambda h, u: (u, h)), w_out],
        scratch_shapes=[pltpu.VMEM((UNIT, HEAD_DIM), F32)] * 9,
        compiler_params=_params(("arbitrary", "arbitrary"), vmem),
        name="dilated_attention",
    )(qk, qk, qk, qk, v, bias, w_cast)


def _moba_kernel(qt_ref, pen_ref, qt_next_ref, pen_next_ref, kx_ref, vt_ref, bias_ref, w_ref,
                 o_ref, wb_ref, s0_ref, s1_ref, p0_ref, p1_ref, *, n_cast_blocks):
    step = pl.program_id(1)
    i = step * MOBA_QUERY_BLOCKS
    i_last = i + MOBA_QUERY_BLOCKS - 1
    width = MOBA_QUERY_BLOCKS * MOBA_BLOCK
    tile = MOBA_TILE_BLOCKS * MOBA_BLOCK
    last_tile = vt_ref.shape[0] - 1
    _cast_rider(w_ref, wb_ref, n_cast_blocks)
    qx = jnp.concatenate([qt_ref[...], pen_ref[...]], axis=0)
    qx_next = jnp.concatenate([qt_next_ref[...], pen_next_ref[...]], axis=0)

    def scores(t, s_ref, qx):
        t = jnp.minimum(t, last_tile)
        rows = pl.ds(pl.multiple_of(t * tile, tile), tile)
        s_ref[...] = jnp.dot(kx_ref[rows, :], qx, preferred_element_type=F32)

    def value_dot(t, p_ref):
        return jnp.dot(vt_ref[jnp.clip(t, 0, last_tile)], p_ref[...], preferred_element_type=F32)

    def update(t, s_ref, p_ref, p_prev_ref, carry, with_bias):
        m, l, acc, alpha_prev = carry
        pv = value_dot(t - 1, p_prev_ref)
        if with_bias:
            def bias_tile(c, a):
                return bias_ref[jnp.clip(i + a - (t * MOBA_TILE_BLOCKS + c), 0, N_BIAS_TILES - 1)]
            s_ref[...] += jnp.concatenate(
                [jnp.concatenate([bias_tile(c, a) for a in range(MOBA_QUERY_BLOCKS)], axis=1)
                 for c in range(MOBA_TILE_BLOCKS)], axis=0)
        m_new = jnp.maximum(m, jnp.max(s_ref[...], axis=0, keepdims=True))
        alpha = jnp.exp2(m - m_new)
        p = jnp.exp2(s_ref[...] - m_new)
        l = alpha * l + jnp.sum(p, axis=0, keepdims=True)
        p_ref[...] = p.astype(BF16)
        return m_new, l, alpha_prev * acc + pv, alpha

    n_tiles = (i_last + MOBA_TILE_BLOCKS) // MOBA_TILE_BLOCKS
    n_far_pairs = jnp.maximum(i - (N_BIAS_TILES - 2), 0) // (2 * MOBA_TILE_BLOCKS)
    n_pairs = n_tiles // 2
    odd = n_tiles % 2

    def tile_pair(u, carry, with_bias):
        t0 = 2 * u
        scores(t0 + 1, s1_ref, qx)
        carry = update(t0, s0_ref, p0_ref, p1_ref, carry, with_bias)
        if with_bias:
            is_last = jnp.logical_and(u == n_pairs - 1, odd == 0)
            scores(jnp.where(is_last, 0, t0 + 2), s0_ref, jnp.where(is_last, qx_next, qx))
        else:
            scores(t0 + 2, s0_ref, qx)
        return update(t0 + 1, s1_ref, p1_ref, p0_ref, carry, with_bias)

    def odd_tile(_, carry):
        carry = update(n_tiles - 1, s0_ref, p0_ref, p1_ref, carry, True)
        scores(0, s0_ref, qx_next)
        return carry

    @pl.when(step == 0)
    def _():
        scores(0, s0_ref, qx)

    p1_ref[...] = jnp.zeros(p1_ref.shape, p1_ref.dtype)
    carry = (jnp.full((1, width), NEG, F32), jnp.zeros((1, width), F32),
             jnp.zeros((HEAD_DIM, width), F32), jnp.ones((1, width), F32))
    carry = lax.fori_loop(0, n_far_pairs, functools.partial(tile_pair, with_bias=False), carry)
    carry = lax.fori_loop(n_far_pairs, n_pairs, functools.partial(tile_pair, with_bias=True), carry)
    _, l, acc, alpha = lax.fori_loop(0, odd, odd_tile, carry)
    p_last = jnp.where(odd == 1, p0_ref[...], p1_ref[...])
    acc = alpha * acc + jnp.dot(vt_ref[n_tiles - 1], p_last, preferred_element_type=F32)
    o_ref[...] = (acc / l).T.astype(o_ref.dtype)


def _moba_attention(qt, pen, kx, vt, bias, w_cast, cast_layer):
    s = kx.shape[0]
    nb = s // MOBA_BLOCK
    tile = MOBA_TILE_BLOCKS * MOBA_BLOCK
    assert nb % (2 * MOBA_TILE_BLOCKS) == 0 and nb <= LANES and vt.shape == (s // tile, D_MODEL, tile)
    kv_blk = s * HEAD_DIM * 2
    width = MOBA_QUERY_BLOCKS * MOBA_BLOCK
    n_steps = nb // MOBA_QUERY_BLOCKS
    assert nb % MOBA_QUERY_BLOCKS == 0
    w_in, w_out, w_shape = _cast_rider_specs(w_cast, cast_layer, n_steps)
    n_cast_blocks = w_cast.shape[1] // CAST_ROWS
    assert n_cast_blocks <= N_HEADS * n_steps
    vmem = 2 * (3 * kv_blk + N_BIAS_TILES * MOBA_BLOCK * MOBA_BLOCK * 4) + 12 * tile * width \
        + 2 * CAST_ROWS * w_cast.shape[2] * 6 + 24 * 2**20
    cur = lambda h, i: (h, i)
    nxt = lambda h, i: (h, jnp.minimum(i + 1, n_steps - 1))
    return pl.pallas_call(
        functools.partial(_moba_kernel, n_cast_blocks=n_cast_blocks),
        out_shape=[jax.ShapeDtypeStruct((s, D_MODEL), BF16), w_shape],
        grid=(N_HEADS, n_steps),
        in_specs=[pl.BlockSpec((HEAD_DIM, width), cur),
                  pl.BlockSpec((LANES, width), cur),
                  pl.BlockSpec((HEAD_DIM, width), nxt),
                  pl.BlockSpec((LANES, width), nxt),
                  pl.BlockSpec((s, 2 * HEAD_DIM), lambda h, i: (0, h)),
                  pl.BlockSpec((s // tile, HEAD_DIM, tile), lambda h, i: (0, h, 0)),
                  pl.BlockSpec((None, N_BIAS_TILES, MOBA_BLOCK, MOBA_BLOCK),
                               lambda h, i: (h, 0, 0, 0)),
                  w_in],
        out_specs=[pl.BlockSpec((width, HEAD_DIM), lambda h, i: (i, h)), w_out],
        scratch_shapes=[pltpu.VMEM((tile, width), F32)] * 2
        + [pltpu.VMEM((tile, width), BF16)] * 2,
        compiler_params=_params(("arbitrary", "arbitrary"), vmem),
        name="moba_attention",
    )(qt, pen, qt, pen, kx, vt, bias, w_cast)


def _mixer_out_and_ffn(o, w_o, x, ffn_gain, w_gate_up, layer, w_down_bf16, next_gains=None):
    x, (h,), rstd = _mm_residual(o, w_o, 0, x, tm=RES_TM, tn=RES_TN, gains=ffn_gain[None, :])
    act = _mm_swiglu(h, w_gate_up, layer, rstd)
    return _mm_residual(act, w_down_bf16[None], 0, x, tm=512, tn=512, gains=next_gains)


def kernel(x, rel_bias, attn_norm, ffn_norm, w_qkv_a, q_norm_a, k_norm_a, w_o_a, kv_norm, w_kv_b, k_norm_b, w_q_b, q_norm_b, w_o_b, w_gate_up, w_down):
    assert x.shape[0] == 1 and x.shape[2] == D_MODEL
    xs = x[0].astype(F32)
    d = D_MODEL

    table = rel_bias.astype(F32) * LOG2E
    bias_dil = _bias_expand(_dilated_bucket_tiles(), table)
    bias_moba = _bias_expand(_moba_bucket_tiles(), table - table[N_BUCKETS - 1:, :])

    (h,) = _rmsnorm(xs, attn_norm[0][None, :])
    gain_a = jnp.concatenate(
        [jnp.tile(q_norm_a[0][g].astype(F32) * (SCALE * LOG2E), N_HEADS) for g in range(N_GROUPS)]
        + [jnp.tile(k_norm_a[0].astype(F32), N_HEADS)])[None, :]
    qk, v = _mm_headnorm(h, w_qkv_a, 0, gain_a, (N_GROUPS + 1) * d, F32)
    o, w_down_0 = _dilated_attention(qk, v, bias_dil, w_down, 0)
    xs, (hk, hq), rstd = _mixer_out_and_ffn(o, w_o_a, xs, ffn_norm[0], w_gate_up, 0, w_down_0,
                                           next_gains=jnp.stack([kv_norm, attn_norm[1]]))

    gain_k = jnp.tile(k_norm_b.astype(F32), N_HEADS)[None, :]
    k, kmean, vt = _mm_shared_kv(hk, w_kv_b, gain_k, rstd)
    gain_q = jnp.tile(q_norm_b[0].astype(F32) * (SCALE * LOG2E), N_HEADS)[None, :]
    qt, pen = _mm_query_gate(hq, w_q_b, 0, gain_q, kmean, rstd)
    o, w_down_1 = _moba_attention(qt, pen, k, vt, bias_moba, w_down, 1)
    xs = _mixer_out_and_ffn(o, w_o_b, xs, ffn_norm[1], w_gate_up, 1, w_down_1)
    return xs[None].astype(x.dtype)
```

```python
import functools
import math

import numpy as np
import jax
import jax.numpy as jnp
from jax import lax
from jax.experimental import pallas as pl
from jax.experimental.pallas import tpu as pltpu

D_MODEL = 4096
HEAD_DIM = 128
N_HEADS = D_MODEL // HEAD_DIM
DILATED_PATTERNS = ((128, 1), (512, 4), (2048, 16))
N_GROUPS = len(DILATED_PATTERNS)
BAND = 128
MOBA_BLOCK = 256
MOBA_TOPK = 3
N_BUCKETS = 32
MAX_DISTANCE = 2048
D_FF = 11008
EPS = 1e-6
NEG = -1e30
SCALE = HEAD_DIM ** -0.5
LOG2E = math.log2(math.e)

UNIT = 2048
BLOCKS_PER_UNIT = UNIT // BAND
DIL_UNROLL = 16
N_BIAS_TILES = 8
MOBA_TILE_BLOCKS = 4
MOBA_QUERY_BLOCKS = 2

V7X_SCOPED_VMEM_BYTES = 60000 * 1024
LANES = 128

F32 = jnp.float32
BF16 = jnp.bfloat16


def _params(semantics, vmem_bytes):
    return pltpu.CompilerParams(dimension_semantics=semantics,
                                vmem_limit_bytes=min(int(vmem_bytes), V7X_SCOPED_VMEM_BYTES))


def _rel_bucket_np(dist):
    n = np.maximum(dist, 0)
    exact = N_BUCKETS // 2
    nf = np.maximum(n, exact).astype(np.float64)
    large = exact + (np.log(nf / exact) / math.log(MAX_DISTANCE / exact)
                     * (N_BUCKETS - exact)).astype(np.int64)
    return np.where(n < exact, n, np.minimum(large, N_BUCKETS - 1)).astype(np.int32)


MASKED = N_BUCKETS


def _dilated_bucket_tiles():
    i = np.arange(BAND)[:, None]
    j = np.arange(2 * BAND)[None, :]
    tiles = []
    for window, d in DILATED_PATTERNS:
        w_sub = window // d
        per = []
        for first in (False, True):
            dist = (i - j) if first else (BAND + i - j)
            ok = (dist >= 0) & (dist <= w_sub)
            per.append(np.where(ok, _rel_bucket_np(dist * d), MASKED))
        tiles.append(np.stack(per))
    return np.stack(tiles).astype(np.int32)


def _moba_bucket_tiles():
    i = np.arange(MOBA_BLOCK)[None, :]
    j = np.arange(MOBA_BLOCK)[:, None]
    tiles = []
    for delta in range(N_BIAS_TILES - 1):
        dist = delta * MOBA_BLOCK + i - j
        b = _rel_bucket_np(dist)
        tiles.append(np.where(dist >= 0, b, MASKED) if delta == 0 else b)
    far = (N_BIAS_TILES - 2) * MOBA_BLOCK + 1
    assert _rel_bucket_np(np.array([far]))[0] == N_BUCKETS - 1
    tiles.append(np.full((MOBA_BLOCK, MOBA_BLOCK), N_BUCKETS - 1))
    return np.stack(tiles).astype(np.int32)


TABLE_ROWS = 64


def _bias_expand_kernel(idx_ref, tab_ref, o_ref):
    idx = idx_ref[...]
    rows = lax.broadcasted_iota(jnp.int32, (TABLE_ROWS, idx.shape[1]), 0)
    onehot = (rows == idx).astype(BF16)
    tab = tab_ref[...]
    hi = tab.astype(BF16)
    rest = tab - hi.astype(F32)
    mid = rest.astype(BF16)
    lo = (rest - mid.astype(F32)).astype(BF16)
    hi, mid, lo = (jnp.dot(piece, onehot, preferred_element_type=F32) for piece in (hi, mid, lo))
    o_ref[...] = (hi + mid) + lo


def _bias_expand(idx_np, table):
    idx_flat = jnp.asarray(idx_np.ravel())
    n = idx_flat.shape[0]
    tn = 8192
    assert n % tn == 0
    tab = jnp.zeros((N_HEADS, TABLE_ROWS), F32).at[:, :N_BUCKETS].set(table.astype(F32).T)
    tab = tab.at[:, MASKED].set(NEG)
    return pl.pallas_call(
        _bias_expand_kernel,
        out_shape=jax.ShapeDtypeStruct((N_HEADS, n), F32),
        grid=(n // tn,),
        in_specs=[pl.BlockSpec((1, tn), lambda i: (0, i)),
                  pl.BlockSpec((N_HEADS, TABLE_ROWS), lambda i: (0, 0))],
        out_specs=pl.BlockSpec((N_HEADS, tn), lambda i: (0, i)),
        compiler_params=_params(("parallel",), 16 * 2**20),
        name="bias_expand",
    )(idx_flat.reshape(1, n), tab).reshape((N_HEADS,) + idx_np.shape)


def _rmsnorm_kernel(x_ref, g_ref, *o_refs):
    x = x_ref[...]
    y = x * lax.rsqrt(jnp.mean(x * x, axis=-1, keepdims=True) + EPS)
    for n, o_ref in enumerate(o_refs):
        o_ref[...] = (y * g_ref[n:n + 1, :]).astype(o_ref.dtype)


def _rmsnorm(x, gains):
    s, d = x.shape
    n_out = gains.shape[0]
    tr = 256
    outs = pl.pallas_call(
        _rmsnorm_kernel,
        out_shape=[jax.ShapeDtypeStruct((s, d), BF16)] * n_out,
        grid=(s // tr,),
        in_specs=[pl.BlockSpec((tr, d), lambda i: (i, 0)),
                  pl.BlockSpec((n_out, d), lambda i: (0, 0))],
        out_specs=[pl.BlockSpec((tr, d), lambda i: (i, 0))] * n_out,
        compiler_params=_params(("parallel",), 2 * tr * d * (4 + 2 * n_out) + 8 * 2**20),
        name="rmsnorm",
    )(x, gains.astype(F32))
    return outs


def _head_rms_cols(acc, gain_ref, c):
    t = acc[:, c * HEAD_DIM:(c + 1) * HEAD_DIM]
    t = t * lax.rsqrt(jnp.mean(t * t, axis=-1, keepdims=True) + EPS)
    return t * gain_ref[:, c * HEAD_DIM:(c + 1) * HEAD_DIM]


def _wmatmul_kernel(*refs, epilogue, n_w, n_extra, n_out, cast, scaled):
    a_ref = refs[0]
    w_refs = refs[1:1 + n_w]
    extra_refs = refs[1 + n_w:1 + n_w + n_extra]
    out_refs = refs[1 + n_w + n_extra:1 + n_w + n_extra + n_out]
    if scaled:
        scale_ref, *extra_refs = extra_refs
    if cast:
        wb_refs = refs[1 + n_w + n_extra + n_out:]

        @pl.when(pl.program_id(1) == 0)
        def _():
            for w_ref, wb_ref in zip(w_refs, wb_refs):
                wb_ref[...] = w_ref[...].astype(BF16)
    else:
        wb_refs = w_refs
    half = a_ref.shape[0] // MM_ROW_SPLIT
    for h in range(MM_ROW_SPLIT):
        rows = slice(h * half, (h + 1) * half)
        a = a_ref[rows, :]
        accs = [jnp.dot(a, wb_ref[...], preferred_element_type=F32) for wb_ref in wb_refs]
        if scaled:
            scale = jnp.tile(scale_ref[rows, :], (1, accs[0].shape[1] // LANES))
            accs = [acc * scale for acc in accs]
        epilogue(accs, extra_refs, out_refs, rows)


def _wmatmul(epilogue, a, weights, extra, outs, *, n_cols, tm, tn, name, row_scale=None, w_tn=None):
    m, k = a.shape
    assert m % tm == 0 and n_cols % tn == 0
    out_tn, tn = tn, (w_tn or tn)
    cast = weights[0][0].dtype != BF16
    if row_scale is not None:
        extra = [(row_scale, pl.BlockSpec((tm, LANES), lambda j, i: (i, 0)))] + list(extra)

    def w_spec(layer, col):
        col_fn = col if callable(col) else (lambda j: j + col)
        return pl.BlockSpec((None, k, tn), lambda j, i: (layer, 0, col_fn(j)))

    w_specs = [w_spec(layer, col) for _, layer, col in weights]
    wsz = weights[0][0].dtype.itemsize
    blocks = tm * k * 2 + len(weights) * k * tn * wsz
    for arr, spec in list(extra) + [(o, s) for o, s in outs]:
        blocks += math.prod(d for d in spec.block_shape if d is not None) * jnp.dtype(arr.dtype).itemsize
    vmem = 2 * blocks + (len(weights) * k * tn * 2 if cast else 0) + (2 + 2 * len(weights)) * tm * tn * 4
    return pl.pallas_call(
        functools.partial(_wmatmul_kernel, epilogue=epilogue, n_w=len(weights), n_extra=len(extra),
                          n_out=len(outs), cast=cast, scaled=row_scale is not None),
        out_shape=[o for o, _ in outs],
        grid=(n_cols // out_tn, m // tm),
        in_specs=[pl.BlockSpec((tm, k), lambda j, i: (i, 0))] + w_specs + [s for _, s in extra],
        out_specs=[s for _, s in outs],
        scratch_shapes=[pltpu.VMEM((k, tn), BF16)] * len(weights) if cast else [],
        compiler_params=_params(("parallel", "arbitrary"), vmem),
        name=name,
    )(a, *[w for w, _, _ in weights], *[arr for arr, _ in extra])


PROJ_TM, PROJ_TN = 512, 1024
RES_TM, RES_TN = 1024, 512
MM_ROW_SPLIT = 2


def _gain_spec(tn):
    return pl.BlockSpec((1, tn), lambda j, i: (0, j))


def _headnorm_epilogue(accs, extra_refs, out_refs, rows):
    (acc,), (gain_ref,), (o_ref,) = accs, extra_refs, out_refs
    for c in range(acc.shape[1] // HEAD_DIM):
        t = _head_rms_cols(acc, gain_ref, c)
        o_ref[rows, c * HEAD_DIM:(c + 1) * HEAD_DIM] = t.astype(o_ref.dtype)


def _store_epilogue(accs, extra_refs, out_refs, rows):
    out_refs[0][rows, :] = accs[0].astype(out_refs[0].dtype)


def _mm_headnorm(a, w, layer, gain_row, n_norm_cols, out_dtype):
    m = a.shape[0]
    n = w.shape[2]
    tm, tn = PROJ_TM, PROJ_TN
    assert n_norm_cols % tn == 0
    tile = pl.BlockSpec((tm, tn), lambda j, i: (i, j))
    (normed,) = _wmatmul(
        _headnorm_epilogue, a, [(w, layer, 0)], [(gain_row, _gain_spec(tn))],
        [(jax.ShapeDtypeStruct((m, n_norm_cols), out_dtype), tile)],
        n_cols=n_norm_cols, tm=tm, tn=tn, name="mm_headnorm")
    (rest,) = _wmatmul(
        _store_epilogue, a, [(w, layer, n_norm_cols // tn)], [],
        [(jax.ShapeDtypeStruct((m, n - n_norm_cols), out_dtype), tile)],
        n_cols=n - n_norm_cols, tm=tm, tn=tn, name="mm_plain")
    return normed, rest


def _block_gate(gate, own):
    blk = lax.broadcasted_iota(jnp.int32, gate.shape, 0)
    blk_f = blk.astype(F32)
    past = blk < own
    gm = jnp.where(past, gate, NEG)
    picked = jnp.zeros(gate.shape, jnp.bool_)
    for _ in range(MOBA_TOPK):
        mx = jnp.max(gm, axis=0, keepdims=True)
        first = jnp.min(jnp.where(gm == mx, blk_f, float(LANES)), axis=0, keepdims=True)
        pick = blk_f == first
        picked = jnp.logical_or(picked, pick)
        gm = jnp.where(pick, -jnp.inf, gm)
    allowed = jnp.logical_or(jnp.logical_and(picked, past), blk == own)
    return jnp.where(allowed, 0.0, NEG)


def _query_gate_epilogue(accs, extra_refs, out_refs, rows):
    (acc,), (gain_ref, kmean_ref), (qt_ref, pen_ref) = accs, extra_refs, out_refs
    n_rows = acc.shape[0]
    nb = kmean_ref.shape[0]
    query = (pl.program_id(1) * qt_ref.shape[1] + rows.start
             + lax.broadcasted_iota(jnp.int32, (nb, n_rows), 1))
    own = query // MOBA_BLOCK
    for c in range(acc.shape[1] // HEAD_DIM):
        cols = slice(c * HEAD_DIM, (c + 1) * HEAD_DIM)
        qt = _head_rms_cols(acc, gain_ref, c).T.astype(qt_ref.dtype)
        qt_ref[cols, rows] = qt
        gate = jnp.dot(kmean_ref[:, cols].astype(BF16), qt, preferred_element_type=F32)
        pen_ref[c * LANES:c * LANES + nb, rows] = _block_gate(gate, own).astype(pen_ref.dtype)
        pen_ref[c * LANES + nb:(c + 1) * LANES, rows] = jnp.zeros((LANES - nb, n_rows), pen_ref.dtype)


def _mm_query_gate(a, w, layer, gain_row, kmean, row_scale):
    m = a.shape[0]
    n = w.shape[2]
    nb = kmean.shape[0]
    tm, tn = PROJ_TM, PROJ_TN
    assert nb <= LANES and LANES == HEAD_DIM
    transposed = pl.BlockSpec((tn, tm), lambda j, i: (j, i))
    return _wmatmul(
        _query_gate_epilogue, a, [(w, layer, 0)],
        [(gain_row, _gain_spec(tn)), (kmean, pl.BlockSpec((nb, tn), lambda j, i: (0, j)))],
        [(jax.ShapeDtypeStruct((n, m), BF16), transposed),
         (jax.ShapeDtypeStruct((n, m), BF16), transposed)],
        n_cols=n, tm=tm, tn=tn, name="mm_query_gate", row_scale=row_scale)


def _key_epilogue(accs, extra_refs, out_refs, rows):
    (acc,), (gain_ref,), (kx_ref, mean_ref) = accs, extra_refs, out_refs
    n_rows, tn = acc.shape
    assert rows.start % MOBA_BLOCK == 0 and n_rows % MOBA_BLOCK == 0
    row = (pl.program_id(1) * kx_ref.shape[0] + rows.start
           + lax.broadcasted_iota(jnp.int32, (n_rows, LANES), 0))
    onehot = (row // MOBA_BLOCK == lax.broadcasted_iota(jnp.int32, (n_rows, LANES), 1)
              ).astype(kx_ref.dtype)
    for c in range(tn // HEAD_DIM):
        cols = slice(c * HEAD_DIM, (c + 1) * HEAD_DIM)
        t = _head_rms_cols(acc, gain_ref, c)
        kx_ref[rows, 2 * c * HEAD_DIM:(2 * c + 1) * HEAD_DIM] = t.astype(kx_ref.dtype)
        kx_ref[rows, (2 * c + 1) * HEAD_DIM:(2 * c + 2) * HEAD_DIM] = onehot
        for r in range(n_rows // MOBA_BLOCK):
            blk = t[r * MOBA_BLOCK:(r + 1) * MOBA_BLOCK, :]
            r_out = rows.start // MOBA_BLOCK + r
            mean_ref[r_out:r_out + 1, cols] = jnp.mean(blk, axis=0, keepdims=True)


def _transpose_epilogue(accs, extra_refs, out_refs, rows):
    out_refs[0][:, rows] = accs[0].T.astype(out_refs[0].dtype)


def _mm_shared_kv(a, w_kv, gain_row, row_scale):
    m = a.shape[0]
    d = w_kv.shape[1] // 2
    w = w_kv[None]
    tm, tn = PROJ_TM, PROJ_TN
    rows = tm // MOBA_BLOCK
    assert m // MOBA_BLOCK <= LANES
    kn, kmean = _wmatmul(
        _key_epilogue, a, [(w, 0, 0)], [(gain_row, _gain_spec(tn))],
        [(jax.ShapeDtypeStruct((m, 2 * d), BF16), pl.BlockSpec((tm, 2 * tn), lambda j, i: (i, j))),
         (jax.ShapeDtypeStruct((m // tm, rows, d), F32),
          pl.BlockSpec((None, rows, tn), lambda j, i: (i, 0, j)))],
        n_cols=d, tm=tm, tn=tn, name="mm_keys", row_scale=row_scale)
    tile = MOBA_TILE_BLOCKS * MOBA_BLOCK
    per_tile = tile // tm
    assert tile % tm == 0
    (vt,) = _wmatmul(
        _transpose_epilogue, a, [(w, 0, d // tn)], [],
        [(jax.ShapeDtypeStruct((m // tile, d, tile), BF16),
          pl.BlockSpec((None, tn, tm), lambda j, i: (i // per_tile, j, i % per_tile)))],
        n_cols=d, tm=tm, tn=tn, name="mm_values_t", row_scale=row_scale)
    return kn, kmean.reshape(m // MOBA_BLOCK, d), vt


def _residual_epilogue(accs, extra_refs, out_refs, rows):
    res_ref, *gain_refs = extra_refs
    x_ref, *norm_refs = out_refs
    x = res_ref[rows, :] + accs[0]
    x_ref[rows, :] = x
    if gain_refs:
        (gain_ref,) = gain_refs
        *scaled_refs, ssq_ref = norm_refs
        for n, scaled_ref in enumerate(scaled_refs):
            scaled_ref[rows, :] = (x * gain_ref[n:n + 1, :]).astype(scaled_ref.dtype)
        ssq = jnp.sum(x * x, axis=1, keepdims=True)
        ssq_ref[rows, :] = jnp.broadcast_to(ssq, (x.shape[0], LANES))


def _rstd_kernel(ssq_ref, o_ref, *, width):
    ssq = ssq_ref[...]
    total = ssq[:, 0:LANES]
    for c in range(1, ssq.shape[1] // LANES):
        total = total + ssq[:, c * LANES:(c + 1) * LANES]
    o_ref[...] = lax.rsqrt(total / width + EPS)


def _mm_residual(a, w, layer, res, tm, tn, gains=None):
    m = a.shape[0]
    n = w.shape[2]
    tile = pl.BlockSpec((tm, tn), lambda j, i: (i, j))
    extra = [(res, tile)]
    outs = [(jax.ShapeDtypeStruct((m, n), F32), tile)]
    if gains is not None:
        n_gains = gains.shape[0]
        extra.append((gains.astype(F32), pl.BlockSpec((n_gains, tn), lambda j, i: (0, j))))
        outs += [(jax.ShapeDtypeStruct((m, n), BF16), tile)] * n_gains
        outs.append((jax.ShapeDtypeStruct((m, n // tn * LANES), F32),
                     pl.BlockSpec((tm, LANES), lambda j, i: (i, j))))
    results = _wmatmul(_residual_epilogue, a, [(w, layer, 0)], extra, outs,
                       n_cols=n, tm=tm, tn=tn, name="mm_residual")
    if gains is None:
        return results[0]
    x, *scaled, ssq = results
    tr = 1024
    rstd = pl.pallas_call(
        functools.partial(_rstd_kernel, width=n),
        out_shape=jax.ShapeDtypeStruct((m, LANES), F32),
        grid=(m // tr,),
        in_specs=[pl.BlockSpec((tr, ssq.shape[1]), lambda i: (i, 0))],
        out_specs=pl.BlockSpec((tr, LANES), lambda i: (i, 0)),
        compiler_params=_params(("parallel",), 32 * 2**20),
        name="rstd",
    )(ssq)
    return x, scaled, rstd


SWIGLU_W_TN = 256
SWIGLU_TILES = 2


def _swiglu_epilogue(accs, extra_refs, out_refs, rows):
    n = len(accs) // 2
    for q, (g, u) in enumerate(zip(accs[:n], accs[n:])):
        cols = slice(q * g.shape[1], (q + 1) * g.shape[1])
        out_refs[0][rows, cols] = (g * (1.0 / (1.0 + jnp.exp(-g))) * u).astype(out_refs[0].dtype)


def _mm_swiglu(a, w_gate_up, layer, row_scale):
    m = a.shape[0]
    tm, w_tn = 512, SWIGLU_W_TN
    n_tiles = D_FF // w_tn
    tn = SWIGLU_TILES * w_tn
    n_steps = -(-n_tiles // SWIGLU_TILES)

    def tile(q, first):
        return lambda j: first + jnp.minimum(SWIGLU_TILES * j + q, n_tiles - 1)

    weights = ([(w_gate_up, layer, tile(q, 0)) for q in range(SWIGLU_TILES)]
               + [(w_gate_up, layer, tile(q, n_tiles)) for q in range(SWIGLU_TILES)])
    (out,) = _wmatmul(
        _swiglu_epilogue, a, weights, [],
        [(jax.ShapeDtypeStruct((m, D_FF), BF16), pl.BlockSpec((tm, tn), lambda j, i: (i, j)))],
        n_cols=n_steps * tn, tm=tm, tn=tn, w_tn=w_tn, name="mm_swiglu", row_scale=row_scale)
    return out


CAST_ROWS = 256


def _cast_rider_specs(w, layer, n_inner):
    _, k, n = w.shape
    assert k % CAST_ROWS == 0
    last = k // CAST_ROWS - 1

    def block(a, b):
        return jnp.minimum(a * n_inner + b, last)

    return (pl.BlockSpec((None, CAST_ROWS, n), lambda a, b: (layer, block(a, b), 0)),
            pl.BlockSpec((CAST_ROWS, n), lambda a, b: (block(a, b), 0)),
            jax.ShapeDtypeStruct((k, n), BF16))


def _cast_rider(w_ref, wb_ref, n_blocks):
    step = pl.program_id(0) * pl.num_programs(1) + pl.program_id(1)

    @pl.when(step < n_blocks)
    def _():
        wb_ref[...] = w_ref[...].astype(wb_ref.dtype)


def _dilated_kernel(q0_ref, q1_ref, q2_ref, k_ref, v_ref, bias_ref, w_ref, o_ref, wb_ref, *scratch,
                    n_cast_blocks):
    og, mg, lg = scratch[0:3], scratch[3:6], scratch[6:9]
    u = pl.program_id(1)
    q_refs = (q0_ref, q1_ref, q2_ref)
    _cast_rider(w_ref, wb_ref, n_cast_blocks)

    for g, (_, d) in enumerate(DILATED_PATTERNS):
        shift = d.bit_length() - 1

        def block(b, g=g, d=d, shift=shift):
            r = jnp.bitwise_and(b, d - 1)
            nl = jnp.right_shift(b, shift)
            rel = r + (BAND * d) * nl
            rows = pl.ds(rel, BAND, stride=d) if d > 1 else pl.ds(rel, BAND)
            q = q_refs[g][rows, :].astype(BF16)
            first = jnp.logical_and(u == 0, nl == 0)
            ks = jnp.where(first, r, u * UNIT + rel - BAND * d)
            krows = pl.ds(ks, 2 * BAND, stride=d) if d > 1 else pl.ds(ks, 2 * BAND)
            k = k_ref[krows, :].astype(BF16)
            v = v_ref[krows, :].astype(BF16)
            s = lax.dot_general(q, k, (((1,), (1,)), ((), ())), preferred_element_type=F32)
            s = s + bias_ref[g, first.astype(jnp.int32)]
            m = jnp.max(s, axis=1, keepdims=True)
            p = jnp.exp2(s - m)
            l = jnp.sum(p, axis=1, keepdims=True)
            o = jnp.dot(p.astype(BF16), v, preferred_element_type=F32)
            og[g][rows, :] = o
            mg[g][rows, :] = jnp.broadcast_to(m, (BAND, HEAD_DIM))
            lg[g][rows, :] = jnp.broadcast_to(l, (BAND, HEAD_DIM))

        def blocks(it, carry, block=block):
            for c in range(DIL_UNROLL):
                block(it * DIL_UNROLL + c)
            return carry

        lax.fori_loop(0, BLOCKS_PER_UNIT // DIL_UNROLL, blocks, 0)

    chunk = 256

    def merge(c, carry):
        rows = pl.ds(pl.multiple_of(c * chunk, chunk), chunk)
        ms = [mg[g][rows, :] for g in range(N_GROUPS)]
        top = jnp.maximum(jnp.maximum(ms[0], ms[1]), ms[2])
        num = jnp.zeros((chunk, HEAD_DIM), F32)
        den = jnp.zeros((chunk, HEAD_DIM), F32)
        for g in range(N_GROUPS):
            w = jnp.exp2(ms[g] - top)
            num = num + w * og[g][rows, :]
            den = den + w * lg[g][rows, :]
        o_ref[rows, :] = (num / den).astype(o_ref.dtype)
        return carry

    lax.fori_loop(0, UNIT // chunk, merge, 0)


def _dilated_attention(qk, v, bias, w_cast, cast_layer):
    s = qk.shape[0]
    assert s % UNIT == 0
    n_units = s // UNIT
    w_in, w_out, w_shape = _cast_rider_specs(w_cast, cast_layer, n_units)
    n_cast_blocks = w_cast.shape[1] // CAST_ROWS
    assert n_cast_blocks <= N_HEADS * n_units
    hcols = N_HEADS
    q_specs = [pl.BlockSpec((UNIT, HEAD_DIM), lambda h, u, g=g: (u, g * hcols + h))
               for g in range(N_GROUPS)]
    k_spec = pl.BlockSpec((s, HEAD_DIM), lambda h, u: (0, N_GROUPS * hcols + h))
    v_spec = pl.BlockSpec((s, HEAD_DIM), lambda h, u: (0, h))
    b_spec = pl.BlockSpec((None, N_GROUPS, 2, BAND, 2 * BAND), lambda h, u: (h, 0, 0, 0, 0))
    blk = UNIT * HEAD_DIM * 4
    vmem = 2 * (3 * blk + 2 * s * HEAD_DIM * 4 + N_GROUPS * 2 * BAND * 2 * BAND * 4 + blk // 2) \
        + 9 * blk + 2 * CAST_ROWS * w_cast.shape[2] * 6 + 8 * 2**20
    return pl.pallas_call(
        functools.partial(_dilated_kernel, n_cast_blocks=n_cast_blocks),
        out_shape=[jax.ShapeDtypeStruct((s, D_MODEL), BF16), w_shape],
        grid=(N_HEADS, n_units),
        in_specs=q_specs + [k_spec, v_spec, b_spec, w_in],
        out_specs=[pl.BlockSpec((UNIT, HEAD_DIM), lambda h, u: (u, h)), w_out],
        scratch_shapes=[pltpu.VMEM((UNIT, HEAD_DIM), F32)] * 9,
        compiler_params=_params(("arbitrary", "arbitrary"), vmem),
        name="dilated_attention",
    )(qk, qk, qk, qk, v, bias, w_cast)


def _moba_kernel(qt_ref, pen_ref, qt_next_ref, pen_next_ref, kx_ref, vt_ref, bias_ref, w_ref,
                 o_ref, wb_ref, s0_ref, s1_ref, p0_ref, p1_ref, *, n_cast_blocks):
    step = pl.program_id(1)
    i = step * MOBA_QUERY_BLOCKS
    i_last = i + MOBA_QUERY_BLOCKS - 1
    width = MOBA_QUERY_BLOCKS * MOBA_BLOCK
    tile = MOBA_TILE_BLOCKS * MOBA_BLOCK
    last_tile = vt_ref.shape[0] - 1
    _cast_rider(w_ref, wb_ref, n_cast_blocks)
    qx = jnp.concatenate([qt_ref[...], pen_ref[...]], axis=0)
    qx_next = jnp.concatenate([qt_next_ref[...], pen_next_ref[...]], axis=0)

    def scores(t, s_ref, qx):
        t = jnp.minimum(t, last_tile)
        rows = pl.ds(pl.multiple_of(t * tile, tile), tile)
        s_ref[...] = jnp.dot(kx_ref[rows, :], qx, preferred_element_type=F32)

    def value_dot(t, p_ref):
        return jnp.dot(vt_ref[jnp.clip(t, 0, last_tile)], p_ref[...], preferred_element_type=F32)

    def update(t, s_ref, p_ref, p_prev_ref, carry, with_bias):
        m, l, acc, alpha_prev = carry
        pv = value_dot(t - 1, p_prev_ref)
        if with_bias:
            def bias_tile(c, a):
                return bias_ref[jnp.clip(i + a - (t * MOBA_TILE_BLOCKS + c), 0, N_BIAS_TILES - 1)]
            s_ref[...] += jnp.concatenate(
                [jnp.concatenate([bias_tile(c, a) for a in range(MOBA_QUERY_BLOCKS)], axis=1)
                 for c in range(MOBA_TILE_BLOCKS)], axis=0)
        m_new = jnp.maximum(m, jnp.max(s_ref[...], axis=0, keepdims=True))
        alpha = jnp.exp2(m - m_new)
        p = jnp.exp2(s_ref[...] - m_new)
        l = alpha * l + jnp.sum(p, axis=0, keepdims=True)
        p_ref[...] = p.astype(BF16)
        return m_new, l, alpha_prev * acc + pv, alpha

    n_tiles = (i_last + MOBA_TILE_BLOCKS) // MOBA_TILE_BLOCKS
    n_far_pairs = jnp.maximum(i - (N_BIAS_TILES - 2), 0) // (2 * MOBA_TILE_BLOCKS)
    n_pairs = n_tiles // 2
    odd = n_tiles % 2

    def tile_pair(u, carry, with_bias):
        t0 = 2 * u
        scores(t0 + 1, s1_ref, qx)
        carry = update(t0, s0_ref, p0_ref, p1_ref, carry, with_bias)
        if with_bias:
            is_last = jnp.logical_and(u == n_pairs - 1, odd == 0)
            scores(jnp.where(is_last, 0, t0 + 2), s0_ref, jnp.where(is_last, qx_next, qx))
        else:
            scores(t0 + 2, s0_ref, qx)
        return update(t0 + 1, s1_ref, p1_ref, p0_ref, carry, with_bias)

    def odd_tile(_, carry):
        carry = update(n_tiles - 1, s0_ref, p0_ref, p1_ref, carry, True)
        scores(0, s0_ref, qx_next)
        return carry

    @pl.when(step == 0)
    def _():
        scores(0, s0_ref, qx)

    p1_ref[...] = jnp.zeros(p1_ref.shape, p1_ref.dtype)
    carry = (jnp.full((1, width), NEG, F32), jnp.zeros((1, width), F32),
             jnp.zeros((HEAD_DIM, width), F32), jnp.ones((1, width), F32))
    carry = lax.fori_loop(0, n_far_pairs, functools.partial(tile_pair, with_bias=False), carry)
    carry = lax.fori_loop(n_far_pairs, n_pairs, functools.partial(tile_pair, with_bias=True), carry)
    _, l, acc, alpha = lax.fori_loop(0, odd, odd_tile, carry)
    p_last = jnp.where(odd == 1, p0_ref[...], p1_ref[...])
    acc = alpha * acc + jnp.dot(vt_ref[n_tiles - 1], p_last, preferred_element_type=F32)
    o_ref[...] = (acc / l).T.astype(o_ref.dtype)


def _moba_attention(qt, pen, kx, vt, bias, w_cast, cast_layer):
    s = kx.shape[0]
    nb = s // MOBA_BLOCK
    tile = MOBA_TILE_BLOCKS * MOBA_BLOCK
    assert nb % (2 * MOBA_TILE_BLOCKS) == 0 and nb <= LANES and vt.shape == (s // tile, D_MODEL, tile)
    kv_blk = s * HEAD_DIM * 2
    width = MOBA_QUERY_BLOCKS * MOBA_BLOCK
    n_steps = nb // MOBA_QUERY_BLOCKS
    assert nb % MOBA_QUERY_BLOCKS == 0
    w_in, w_out, w_shape = _cast_rider_specs(w_cast, cast_layer, n_steps)
    n_cast_blocks = w_cast.shape[1] // CAST_ROWS
    assert n_cast_blocks <= N_HEADS * n_steps
    vmem = 2 * (3 * kv_blk + N_BIAS_TILES * MOBA_BLOCK * MOBA_BLOCK * 4) + 12 * tile * width \
        + 2 * CAST_ROWS * w_cast.shape[2] * 6 + 24 * 2**20
    cur = lambda h, i: (h, i)
    nxt = lambda h, i: (h, jnp.minimum(i + 1, n_steps - 1))
    return pl.pallas_call(
        functools.partial(_moba_kernel, n_cast_blocks=n_cast_blocks),
        out_shape=[jax.ShapeDtypeStruct((s, D_MODEL), BF16), w_shape],
        grid=(N_HEADS, n_steps),
        in_specs=[pl.BlockSpec((HEAD_DIM, width), cur),
                  pl.BlockSpec((LANES, width), cur),
                  pl.BlockSpec((HEAD_DIM, width), nxt),
                  pl.BlockSpec((LANES, width), nxt),
                  pl.BlockSpec((s, 2 * HEAD_DIM), lambda h, i: (0, h)),
                  pl.BlockSpec((s // tile, HEAD_DIM, tile), lambda h, i: (0, h, 0)),
                  pl.BlockSpec((None, N_BIAS_TILES, MOBA_BLOCK, MOBA_BLOCK),
                               lambda h, i: (h, 0, 0, 0)),
                  w_in],
        out_specs=[pl.BlockSpec((width, HEAD_DIM), lambda h, i: (i, h)), w_out],
        scratch_shapes=[pltpu.VMEM((tile, width), F32)] * 2
        + [pltpu.VMEM((tile, width), BF16)] * 2,
        compiler_params=_params(("arbitrary", "arbitrary"), vmem),
        name="moba_attention",
    )(qt, pen, qt, pen, kx, vt, bias, w_cast)


def _mixer_out_and_ffn(o, w_o, x, ffn_gain, w_gate_up, layer, w_down_bf16, next_gains=None):
    x, (h,), rstd = _mm_residual(o, w_o, 0, x, tm=RES_TM, tn=RES_TN, gains=ffn_gain[None, :])
    act = _mm_swiglu(h, w_gate_up, layer, rstd)
    return _mm_residual(act, w_down_bf16[None], 0, x, tm=512, tn=512, gains=next_gains)


def kernel(x, rel_bias, attn_norm, ffn_norm, w_qkv_a, q_norm_a, k_norm_a, w_o_a, kv_norm, w_kv_b, k_norm_b, w_q_b, q_norm_b, w_o_b, w_gate_up, w_down):
    assert x.shape[0] == 1 and x.shape[2] == D_MODEL
    xs = x[0].astype(F32)
    d = D_MODEL

    table = rel_bias.astype(F32) * LOG2E
    bias_dil = _bias_expand(_dilated_bucket_tiles(), table)
    bias_moba = _bias_expand(_moba_bucket_tiles(), table - table[N_BUCKETS - 1:, :])

    (h,) = _rmsnorm(xs, attn_norm[0][None, :])
    gain_a = jnp.concatenate(
        [jnp.tile(q_norm_a[0][g].astype(F32) * (SCALE * LOG2E), N_HEADS) for g in range(N_GROUPS)]
        + [jnp.tile(k_norm_a[0].astype(F32), N_HEADS)])[None, :]
    qk, v = _mm_headnorm(h, w_qkv_a, 0, gain_a, (N_GROUPS + 1) * d, F32)
    o, w_down_0 = _dilated_attention(qk, v, bias_dil, w_down, 0)
    xs, (hk, hq), rstd = _mixer_out_and_ffn(o, w_o_a, xs, ffn_norm[0], w_gate_up, 0, w_down_0,
                                           next_gains=jnp.stack([kv_norm, attn_norm[1]]))

    gain_k = jnp.tile(k_norm_b.astype(F32), N_HEADS)[None, :]
    k, kmean, vt = _mm_shared_kv(hk, w_kv_b, gain_k, rstd)
    gain_q = jnp.tile(q_norm_b[0].astype(F32) * (SCALE * LOG2E), N_HEADS)[None, :]
    qt, pen = _mm_query_gate(hq, w_q_b, 0, gain_q, kmean, rstd)
    o, w_down_1 = _moba_attention(qt, pen, k, vt, bias_moba, w_down, 1)
    xs = _mixer_out_and_ffn(o, w_o_b, xs, ffn_norm[1], w_gate_up, 1, w_down_1)
    return xs[None].astype(x.dtype)
```

```python
import functools
import math

import numpy as np
import jax
import jax.numpy as jnp
from jax import lax
from jax.experimental import pallas as pl
from jax.experimental.pallas import tpu as pltpu

D_MODEL = 4096
HEAD_DIM = 128
N_HEADS = D_MODEL // HEAD_DIM
DILATED_PATTERNS = ((128, 1), (512, 4), (2048, 16))
N_GROUPS = len(DILATED_PATTERNS)
BAND = 128
MOBA_BLOCK = 256
MOBA_TOPK = 3
N_BUCKETS = 32
MAX_DISTANCE = 2048
D_FF = 11008
EPS = 1e-6
NEG = -1e30
SCALE = HEAD_DIM ** -0.5
LOG2E = math.log2(math.e)

UNIT = 2048
BLOCKS_PER_UNIT = UNIT // BAND
DIL_UNROLL = 16
N_BIAS_TILES = 8
MOBA_TILE_BLOCKS = 4
MOBA_QUERY_BLOCKS = 4

V7X_SCOPED_VMEM_BYTES = 60000 * 1024
LANES = 128

F32 = jnp.float32
BF16 = jnp.bfloat16


def _params(semantics, vmem_bytes):
    return pltpu.CompilerParams(dimension_semantics=semantics,
                                vmem_limit_bytes=min(int(vmem_bytes), V7X_SCOPED_VMEM_BYTES))


def _rel_bucket_np(dist):
    n = np.maximum(dist, 0)
    exact = N_BUCKETS // 2
    nf = np.maximum(n, exact).astype(np.float64)
    large = exact + (np.log(nf / exact) / math.log(MAX_DISTANCE / exact)
                     * (N_BUCKETS - exact)).astype(np.int64)
    return np.where(n < exact, n, np.minimum(large, N_BUCKETS - 1)).astype(np.int32)


MASKED = N_BUCKETS


def _dilated_bucket_tiles():
    i = np.arange(BAND)[:, None]
    j = np.arange(2 * BAND)[None, :]
    tiles = []
    for window, d in DILATED_PATTERNS:
        w_sub = window // d
        per = []
        for first in (False, True):
            dist = (i - j) if first else (BAND + i - j)
            ok = (dist >= 0) & (dist <= w_sub)
            per.append(np.where(ok, _rel_bucket_np(dist * d), MASKED))
        tiles.append(np.stack(per))
    return np.stack(tiles).astype(np.int32)


def _moba_bucket_tiles():
    i = np.arange(MOBA_BLOCK)[None, :]
    j = np.arange(MOBA_BLOCK)[:, None]
    tiles = []
    for delta in range(N_BIAS_TILES - 1):
        dist = delta * MOBA_BLOCK + i - j
        b = _rel_bucket_np(dist)
        tiles.append(np.where(dist >= 0, b, MASKED) if delta == 0 else b)
    far = (N_BIAS_TILES - 2) * MOBA_BLOCK + 1
    assert _rel_bucket_np(np.array([far]))[0] == N_BUCKETS - 1
    tiles.append(np.full((MOBA_BLOCK, MOBA_BLOCK), N_BUCKETS - 1))
    return np.stack(tiles).astype(np.int32)


TABLE_ROWS = 64


def _bias_expand_kernel(idx_ref, tab_ref, o_ref):
    idx = idx_ref[...]
    rows = lax.broadcasted_iota(jnp.int32, (TABLE_ROWS, idx.shape[1]), 0)
    onehot = (rows == idx).astype(BF16)
    tab = tab_ref[...]
    hi = tab.astype(BF16)
    rest = tab - hi.astype(F32)
    mid = rest.astype(BF16)
    lo = (rest - mid.astype(F32)).astype(BF16)
    hi, mid, lo = (jnp.dot(piece, onehot, preferred_element_type=F32) for piece in (hi, mid, lo))
    o_ref[...] = (hi + mid) + lo


def _bias_expand(idx_np, table):
    idx_flat = jnp.asarray(idx_np.ravel())
    n = idx_flat.shape[0]
    tn = 8192
    assert n % tn == 0
    tab = jnp.zeros((N_HEADS, TABLE_ROWS), F32).at[:, :N_BUCKETS].set(table.astype(F32).T)
    tab = tab.at[:, MASKED].set(NEG)
    return pl.pallas_call(
        _bias_expand_kernel,
        out_shape=jax.ShapeDtypeStruct((N_HEADS, n), F32),
        grid=(n // tn,),
        in_specs=[pl.BlockSpec((1, tn), lambda i: (0, i)),
                  pl.BlockSpec((N_HEADS, TABLE_ROWS), lambda i: (0, 0))],
        out_specs=pl.BlockSpec((N_HEADS, tn), lambda i: (0, i)),
        compiler_params=_params(("parallel",), 16 * 2**20),
        name="bias_expand",
    )(idx_flat.reshape(1, n), tab).reshape((N_HEADS,) + idx_np.shape)


def _rmsnorm_kernel(x_ref, g_ref, *o_refs):
    x = x_ref[...]
    y = x * lax.rsqrt(jnp.mean(x * x, axis=-1, keepdims=True) + EPS)
    for n, o_ref in enumerate(o_refs):
        o_ref[...] = (y * g_ref[n:n + 1, :]).astype(o_ref.dtype)


def _rmsnorm(x, gains):
    s, d = x.shape
    n_out = gains.shape[0]
    tr = 512
    outs = pl.pallas_call(
        _rmsnorm_kernel,
        out_shape=[jax.ShapeDtypeStruct((s, d), BF16)] * n_out,
        grid=(s // tr,),
        in_specs=[pl.BlockSpec((tr, d), lambda i: (i, 0)),
                  pl.BlockSpec((n_out, d), lambda i: (0, 0))],
        out_specs=[pl.BlockSpec((tr, d), lambda i: (i, 0))] * n_out,
        compiler_params=_params(("parallel",), 2 * tr * d * (4 + 2 * n_out) + 8 * 2**20),
        name="rmsnorm",
    )(x, gains.astype(F32))
    return outs


def _head_rms_cols(acc, gain_ref, c):
    t = acc[:, c * HEAD_DIM:(c + 1) * HEAD_DIM]
    t = t * lax.rsqrt(jnp.mean(t * t, axis=-1, keepdims=True) + EPS)
    return t * gain_ref[:, c * HEAD_DIM:(c + 1) * HEAD_DIM]


def _wmatmul_kernel(*refs, epilogue, n_w, n_extra, n_out, cast, scaled):
    a_ref = refs[0]
    w_refs = refs[1:1 + n_w]
    extra_refs = refs[1 + n_w:1 + n_w + n_extra]
    out_refs = refs[1 + n_w + n_extra:1 + n_w + n_extra + n_out]
    if scaled:
        scale_ref, *extra_refs = extra_refs
    if cast:
        wb_refs = refs[1 + n_w + n_extra + n_out:]

        @pl.when(pl.program_id(1) == 0)
        def _():
            for w_ref, wb_ref in zip(w_refs, wb_refs):
                wb_ref[...] = w_ref[...].astype(BF16)
    else:
        wb_refs = w_refs
    half = a_ref.shape[0] // MM_ROW_SPLIT
    for h in range(MM_ROW_SPLIT):
        rows = slice(h * half, (h + 1) * half)
        a = a_ref[rows, :]
        accs = [jnp.dot(a, wb_ref[...], preferred_element_type=F32) for wb_ref in wb_refs]
        if scaled:
            scale = jnp.tile(scale_ref[rows, :], (1, accs[0].shape[1] // LANES))
            accs = [acc * scale for acc in accs]
        epilogue(accs, extra_refs, out_refs, rows)


def _wmatmul(epilogue, a, weights, extra, outs, *, n_cols, tm, tn, name, row_scale=None):
    m, k = a.shape
    assert m % tm == 0 and n_cols % tn == 0
    cast = weights[0][0].dtype != BF16
    if row_scale is not None:
        extra = [(row_scale, pl.BlockSpec((tm, LANES), lambda j, i: (i, 0)))] + list(extra)
    w_specs = [pl.BlockSpec((None, k, tn), lambda j, i, layer=layer, off=off: (layer, 0, j + off))
               for _, layer, off in weights]
    wsz = weights[0][0].dtype.itemsize
    blocks = tm * k * 2 + len(weights) * k * tn * wsz
    for arr, spec in list(extra) + [(o, s) for o, s in outs]:
        blocks += math.prod(d for d in spec.block_shape if d is not None) * jnp.dtype(arr.dtype).itemsize
    vmem = 2 * blocks + (len(weights) * k * tn * 2 if cast else 0) + (2 + 2 * len(weights)) * tm * tn * 4
    return pl.pallas_call(
        functools.partial(_wmatmul_kernel, epilogue=epilogue, n_w=len(weights), n_extra=len(extra),
                          n_out=len(outs), cast=cast, scaled=row_scale is not None),
        out_shape=[o for o, _ in outs],
        grid=(n_cols // tn, m // tm),
        in_specs=[pl.BlockSpec((tm, k), lambda j, i: (i, 0))] + w_specs + [s for _, s in extra],
        out_specs=[s for _, s in outs],
        scratch_shapes=[pltpu.VMEM((k, tn), BF16)] * len(weights) if cast else [],
        compiler_params=_params(("parallel", "arbitrary"), vmem),
        name=name,
    )(a, *[w for w, _, _ in weights], *[arr for arr, _ in extra])


PROJ_TM, PROJ_TN = 512, 1024
RES_TM, RES_TN = 1024, 512
MM_ROW_SPLIT = 2


def _gain_spec(tn):
    return pl.BlockSpec((1, tn), lambda j, i: (0, j))


def _headnorm_epilogue(accs, extra_refs, out_refs, rows):
    (acc,), (gain_ref,), (o_ref,) = accs, extra_refs, out_refs
    for c in range(acc.shape[1] // HEAD_DIM):
        t = _head_rms_cols(acc, gain_ref, c)
        o_ref[rows, c * HEAD_DIM:(c + 1) * HEAD_DIM] = t.astype(o_ref.dtype)


def _store_epilogue(accs, extra_refs, out_refs, rows):
    out_refs[0][rows, :] = accs[0].astype(out_refs[0].dtype)


def _mm_headnorm(a, w, layer, gain_row, n_norm_cols, out_dtype):
    m = a.shape[0]
    n = w.shape[2]
    tm, tn = PROJ_TM, PROJ_TN
    assert n_norm_cols % tn == 0
    tile = pl.BlockSpec((tm, tn), lambda j, i: (i, j))
    (normed,) = _wmatmul(
        _headnorm_epilogue, a, [(w, layer, 0)], [(gain_row, _gain_spec(tn))],
        [(jax.ShapeDtypeStruct((m, n_norm_cols), out_dtype), tile)],
        n_cols=n_norm_cols, tm=tm, tn=tn, name="mm_headnorm")
    (rest,) = _wmatmul(
        _store_epilogue, a, [(w, layer, n_norm_cols // tn)], [],
        [(jax.ShapeDtypeStruct((m, n - n_norm_cols), out_dtype), tile)],
        n_cols=n - n_norm_cols, tm=tm, tn=tn, name="mm_plain")
    return normed, rest


def _block_gate(gate, own):
    blk = lax.broadcasted_iota(jnp.int32, gate.shape, 0)
    blk_f = blk.astype(F32)
    past = blk < own
    gm = jnp.where(past, gate, NEG)
    picked = jnp.zeros(gate.shape, jnp.bool_)
    for _ in range(MOBA_TOPK):
        mx = jnp.max(gm, axis=0, keepdims=True)
        first = jnp.min(jnp.where(gm == mx, blk_f, float(LANES)), axis=0, keepdims=True)
        pick = blk_f == first
        picked = jnp.logical_or(picked, pick)
        gm = jnp.where(pick, -jnp.inf, gm)
    allowed = jnp.logical_or(jnp.logical_and(picked, past), blk == own)
    return jnp.where(allowed, 0.0, NEG)


def _query_gate_epilogue(accs, extra_refs, out_refs, rows):
    (acc,), (gain_ref, kmean_ref), (qt_ref, pen_ref) = accs, extra_refs, out_refs
    n_rows = acc.shape[0]
    nb = kmean_ref.shape[0]
    query = (pl.program_id(1) * qt_ref.shape[1] + rows.start
             + lax.broadcasted_iota(jnp.int32, (nb, n_rows), 1))
    own = query // MOBA_BLOCK
    for c in range(acc.shape[1] // HEAD_DIM):
        cols = slice(c * HEAD_DIM, (c + 1) * HEAD_DIM)
        qt = _head_rms_cols(acc, gain_ref, c).T.astype(qt_ref.dtype)
        qt_ref[cols, rows] = qt
        gate = jnp.dot(kmean_ref[:, cols].astype(BF16), qt, preferred_element_type=F32)
        pen_ref[c * LANES:c * LANES + nb, rows] = _block_gate(gate, own).astype(pen_ref.dtype)
        pen_ref[c * LANES + nb:(c + 1) * LANES, rows] = jnp.zeros((LANES - nb, n_rows), pen_ref.dtype)


def _mm_query_gate(a, w, layer, gain_row, kmean, row_scale):
    m = a.shape[0]
    n = w.shape[2]
    nb = kmean.shape[0]
    tm, tn = PROJ_TM, PROJ_TN
    assert nb <= LANES and LANES == HEAD_DIM
    transposed = pl.BlockSpec((tn, tm), lambda j, i: (j, i))
    return _wmatmul(
        _query_gate_epilogue, a, [(w, layer, 0)],
        [(gain_row, _gain_spec(tn)), (kmean, pl.BlockSpec((nb, tn), lambda j, i: (0, j)))],
        [(jax.ShapeDtypeStruct((n, m), BF16), transposed),
         (jax.ShapeDtypeStruct((n, m), BF16), transposed)],
        n_cols=n, tm=tm, tn=tn, name="mm_query_gate", row_scale=row_scale)


def _key_epilogue(accs, extra_refs, out_refs, rows):
    (acc,), (gain_ref,), (kx_ref, mean_ref) = accs, extra_refs, out_refs
    n_rows, tn = acc.shape
    assert rows.start % MOBA_BLOCK == 0 and n_rows % MOBA_BLOCK == 0
    row = (pl.program_id(1) * kx_ref.shape[0] + rows.start
           + lax.broadcasted_iota(jnp.int32, (n_rows, LANES), 0))
    onehot = (row // MOBA_BLOCK == lax.broadcasted_iota(jnp.int32, (n_rows, LANES), 1)
              ).astype(kx_ref.dtype)
    for c in range(tn // HEAD_DIM):
        cols = slice(c * HEAD_DIM, (c + 1) * HEAD_DIM)
        t = _head_rms_cols(acc, gain_ref, c)
        kx_ref[rows, 2 * c * HEAD_DIM:(2 * c + 1) * HEAD_DIM] = t.astype(kx_ref.dtype)
        kx_ref[rows, (2 * c + 1) * HEAD_DIM:(2 * c + 2) * HEAD_DIM] = onehot
        for r in range(n_rows // MOBA_BLOCK):
            blk = t[r * MOBA_BLOCK:(r + 1) * MOBA_BLOCK, :]
            r_out = rows.start // MOBA_BLOCK + r
            mean_ref[r_out:r_out + 1, cols] = jnp.mean(blk, axis=0, keepdims=True)


def _transpose_epilogue(accs, extra_refs, out_refs, rows):
    out_refs[0][:, rows] = accs[0].T.astype(out_refs[0].dtype)


def _mm_shared_kv(a, w_kv, gain_row, row_scale):
    m = a.shape[0]
    d = w_kv.shape[1] // 2
    w = w_kv[None]
    tm, tn = PROJ_TM, PROJ_TN
    rows = tm // MOBA_BLOCK
    assert m // MOBA_BLOCK <= LANES
    kn, kmean = _wmatmul(
        _key_epilogue, a, [(w, 0, 0)], [(gain_row, _gain_spec(tn))],
        [(jax.ShapeDtypeStruct((m, 2 * d), BF16), pl.BlockSpec((tm, 2 * tn), lambda j, i: (i, j))),
         (jax.ShapeDtypeStruct((m // tm, rows, d), F32),
          pl.BlockSpec((None, rows, tn), lambda j, i: (i, 0, j)))],
        n_cols=d, tm=tm, tn=tn, name="mm_keys", row_scale=row_scale)
    tile = MOBA_TILE_BLOCKS * MOBA_BLOCK
    per_tile = tile // tm
    assert tile % tm == 0
    (vt,) = _wmatmul(
        _transpose_epilogue, a, [(w, 0, d // tn)], [],
        [(jax.ShapeDtypeStruct((m // tile, d, tile), BF16),
          pl.BlockSpec((None, tn, tm), lambda j, i: (i // per_tile, j, i % per_tile)))],
        n_cols=d, tm=tm, tn=tn, name="mm_values_t", row_scale=row_scale)
    return kn, kmean.reshape(m // MOBA_BLOCK, d), vt


def _residual_epilogue(accs, extra_refs, out_refs, rows):
    res_ref, *gain_refs = extra_refs
    x_ref, *norm_refs = out_refs
    x = res_ref[rows, :] + accs[0]
    x_ref[rows, :] = x
    if gain_refs:
        (gain_ref,) = gain_refs
        *scaled_refs, ssq_ref = norm_refs
        for n, scaled_ref in enumerate(scaled_refs):
            scaled_ref[rows, :] = (x * gain_ref[n:n + 1, :]).astype(scaled_ref.dtype)
        ssq = jnp.sum(x * x, axis=1, keepdims=True)
        ssq_ref[rows, :] = jnp.broadcast_to(ssq, (x.shape[0], LANES))


def _rstd_kernel(ssq_ref, o_ref, *, width):
    ssq = ssq_ref[...]
    total = ssq[:, 0:LANES]
    for c in range(1, ssq.shape[1] // LANES):
        total = total + ssq[:, c * LANES:(c + 1) * LANES]
    o_ref[...] = lax.rsqrt(total / width + EPS)


def _mm_residual(a, w, layer, res, tm, tn, gains=None):
    m = a.shape[0]
    n = w.shape[2]
    tile = pl.BlockSpec((tm, tn), lambda j, i: (i, j))
    extra = [(res, tile)]
    outs = [(jax.ShapeDtypeStruct((m, n), F32), tile)]
    if gains is not None:
        n_gains = gains.shape[0]
        extra.append((gains.astype(F32), pl.BlockSpec((n_gains, tn), lambda j, i: (0, j))))
        outs += [(jax.ShapeDtypeStruct((m, n), BF16), tile)] * n_gains
        outs.append((jax.ShapeDtypeStruct((m, n // tn * LANES), F32),
                     pl.BlockSpec((tm, LANES), lambda j, i: (i, j))))
    results = _wmatmul(_residual_epilogue, a, [(w, layer, 0)], extra, outs,
                       n_cols=n, tm=tm, tn=tn, name="mm_residual")
    if gains is None:
        return results[0]
    x, *scaled, ssq = results
    tr = 1024
    rstd = pl.pallas_call(
        functools.partial(_rstd_kernel, width=n),
        out_shape=jax.ShapeDtypeStruct((m, LANES), F32),
        grid=(m // tr,),
        in_specs=[pl.BlockSpec((tr, ssq.shape[1]), lambda i: (i, 0))],
        out_specs=pl.BlockSpec((tr, LANES), lambda i: (i, 0)),
        compiler_params=_params(("parallel",), 32 * 2**20),
        name="rstd",
    )(ssq)
    return x, scaled, rstd


def _swiglu_epilogue(accs, extra_refs, out_refs, rows):
    g, u = accs
    out_refs[0][rows, :] = (g * (1.0 / (1.0 + jnp.exp(-g))) * u).astype(out_refs[0].dtype)


def _mm_swiglu(a, w_gate_up, layer, row_scale):
    m = a.shape[0]
    tm, tn = 1024, 256
    (out,) = _wmatmul(
        _swiglu_epilogue, a, [(w_gate_up, layer, 0), (w_gate_up, layer, D_FF // tn)], [],
        [(jax.ShapeDtypeStruct((m, D_FF), BF16), pl.BlockSpec((tm, tn), lambda j, i: (i, j)))],
        n_cols=D_FF, tm=tm, tn=tn, name="mm_swiglu", row_scale=row_scale)
    return out


CAST_ROWS = 256


def _cast_rider_specs(w, layer, n_inner):
    _, k, n = w.shape
    assert k % CAST_ROWS == 0
    last = k // CAST_ROWS - 1

    def block(a, b):
        return jnp.minimum(a * n_inner + b, last)

    return (pl.BlockSpec((None, CAST_ROWS, n), lambda a, b: (layer, block(a, b), 0)),
            pl.BlockSpec((CAST_ROWS, n), lambda a, b: (block(a, b), 0)),
            jax.ShapeDtypeStruct((k, n), BF16))


def _cast_rider(w_ref, wb_ref, n_blocks):
    step = pl.program_id(0) * pl.num_programs(1) + pl.program_id(1)

    @pl.when(step < n_blocks)
    def _():
        wb_ref[...] = w_ref[...].astype(wb_ref.dtype)


def _dilated_kernel(q0_ref, q1_ref, q2_ref, k_ref, v_ref, bias_ref, w_ref, o_ref, wb_ref, *scratch,
                    n_cast_blocks):
    og, mg, lg = scratch[0:3], scratch[3:6], scratch[6:9]
    u = pl.program_id(1)
    q_refs = (q0_ref, q1_ref, q2_ref)
    _cast_rider(w_ref, wb_ref, n_cast_blocks)

    for g, (_, d) in enumerate(DILATED_PATTERNS):
        shift = d.bit_length() - 1

        def block(b, g=g, d=d, shift=shift):
            r = jnp.bitwise_and(b, d - 1)
            nl = jnp.right_shift(b, shift)
            rel = r + (BAND * d) * nl
            rows = pl.ds(rel, BAND, stride=d) if d > 1 else pl.ds(rel, BAND)
            q = q_refs[g][rows, :].astype(BF16)
            first = jnp.logical_and(u == 0, nl == 0)
            ks = jnp.where(first, r, u * UNIT + rel - BAND * d)
            krows = pl.ds(ks, 2 * BAND, stride=d) if d > 1 else pl.ds(ks, 2 * BAND)
            k = k_ref[krows, :].astype(BF16)
            v = v_ref[krows, :].astype(BF16)
            s = lax.dot_general(q, k, (((1,), (1,)), ((), ())), preferred_element_type=F32)
            s = s + bias_ref[g, first.astype(jnp.int32)]
            m = jnp.max(s, axis=1, keepdims=True)
            p = jnp.exp2(s - m)
            l = jnp.sum(p, axis=1, keepdims=True)
            o = jnp.dot(p.astype(BF16), v, preferred_element_type=F32)
            og[g][rows, :] = o
            mg[g][rows, :] = jnp.broadcast_to(m, (BAND, HEAD_DIM))
            lg[g][rows, :] = jnp.broadcast_to(l, (BAND, HEAD_DIM))

        def blocks(it, carry, block=block):
            for c in range(DIL_UNROLL):
                block(it * DIL_UNROLL + c)
            return carry

        lax.fori_loop(0, BLOCKS_PER_UNIT // DIL_UNROLL, blocks, 0)

    chunk = 256

    def merge(c, carry):
        rows = pl.ds(pl.multiple_of(c * chunk, chunk), chunk)
        ms = [mg[g][rows, :] for g in range(N_GROUPS)]
        top = jnp.maximum(jnp.maximum(ms[0], ms[1]), ms[2])
        num = jnp.zeros((chunk, HEAD_DIM), F32)
        den = jnp.zeros((chunk, HEAD_DIM), F32)
        for g in range(N_GROUPS):
            w = jnp.exp2(ms[g] - top)
            num = num + w * og[g][rows, :]
            den = den + w * lg[g][rows, :]
        o_ref[rows, :] = (num / den).astype(o_ref.dtype)
        return carry

    lax.fori_loop(0, UNIT // chunk, merge, 0)


def _dilated_attention(qk, v, bias, w_cast, cast_layer):
    s = qk.shape[0]
    assert s % UNIT == 0
    n_units = s // UNIT
    w_in, w_out, w_shape = _cast_rider_specs(w_cast, cast_layer, n_units)
    n_cast_blocks = w_cast.shape[1] // CAST_ROWS
    assert n_cast_blocks <= N_HEADS * n_units
    hcols = N_HEADS
    q_specs = [pl.BlockSpec((UNIT, HEAD_DIM), lambda h, u, g=g: (u, g * hcols + h))
               for g in range(N_GROUPS)]
    k_spec = pl.BlockSpec((s, HEAD_DIM), lambda h, u: (0, N_GROUPS * hcols + h))
    v_spec = pl.BlockSpec((s, HEAD_DIM), lambda h, u: (0, h))
    b_spec = pl.BlockSpec((None, N_GROUPS, 2, BAND, 2 * BAND), lambda h, u: (h, 0, 0, 0, 0))
    blk = UNIT * HEAD_DIM * 4
    vmem = 2 * (3 * blk + 2 * s * HEAD_DIM * 4 + N_GROUPS * 2 * BAND * 2 * BAND * 4 + blk // 2) \
        + 9 * blk + 2 * CAST_ROWS * w_cast.shape[2] * 6 + 8 * 2**20
    return pl.pallas_call(
        functools.partial(_dilated_kernel, n_cast_blocks=n_cast_blocks),
        out_shape=[jax.ShapeDtypeStruct((s, D_MODEL), BF16), w_shape],
        grid=(N_HEADS, n_units),
        in_specs=q_specs + [k_spec, v_spec, b_spec, w_in],
        out_specs=[pl.BlockSpec((UNIT, HEAD_DIM), lambda h, u: (u, h)), w_out],
        scratch_shapes=[pltpu.VMEM((UNIT, HEAD_DIM), F32)] * 9,
        compiler_params=_params(("arbitrary", "arbitrary"), vmem),
        name="dilated_attention",
    )(qk, qk, qk, qk, v, bias, w_cast)


def _moba_kernel(qt_ref, pen_ref, qt_next_ref, pen_next_ref, kx_ref, vt_ref, bias_ref, w_ref,
                 o_ref, wb_ref, s0_ref, s1_ref, p0_ref, p1_ref, *, n_cast_blocks):
    step = pl.program_id(1)
    i = step * MOBA_QUERY_BLOCKS
    i_last = i + MOBA_QUERY_BLOCKS - 1
    width = MOBA_QUERY_BLOCKS * MOBA_BLOCK
    tile = MOBA_TILE_BLOCKS * MOBA_BLOCK
    last_tile = vt_ref.shape[0] - 1
    _cast_rider(w_ref, wb_ref, n_cast_blocks)
    qx = jnp.concatenate([qt_ref[...], pen_ref[...]], axis=0)
    qx_next = jnp.concatenate([qt_next_ref[...], pen_next_ref[...]], axis=0)

    def scores(t, s_ref, qx):
        t = jnp.minimum(t, last_tile)
        rows = pl.ds(pl.multiple_of(t * tile, tile), tile)
        s_ref[...] = jnp.dot(kx_ref[rows, :], qx, preferred_element_type=F32)

    def value_dot(t, p_ref):
        return jnp.dot(vt_ref[jnp.clip(t, 0, last_tile)], p_ref[...], preferred_element_type=F32)

    def update(t, s_ref, p_ref, p_prev_ref, carry, with_bias):
        m, l, acc, alpha_prev = carry
        pv = value_dot(t - 1, p_prev_ref)
        if with_bias:
            def bias_tile(c, a):
                return bias_ref[jnp.clip(i + a - (t * MOBA_TILE_BLOCKS + c), 0, N_BIAS_TILES - 1)]
            s_ref[...] += jnp.concatenate(
                [jnp.concatenate([bias_tile(c, a) for a in range(MOBA_QUERY_BLOCKS)], axis=1)
                 for c in range(MOBA_TILE_BLOCKS)], axis=0)
        m_new = jnp.maximum(m, jnp.max(s_ref[...], axis=0, keepdims=True))
        alpha = jnp.exp2(m - m_new)
        p = jnp.exp2(s_ref[...] - m_new)
        l = alpha * l + jnp.sum(p, axis=0, keepdims=True)
        p_ref[...] = p.astype(BF16)
        return m_new, l, alpha_prev * acc + pv, alpha

    n_tiles = (i_last + MOBA_TILE_BLOCKS) // MOBA_TILE_BLOCKS
    n_far_pairs = jnp.maximum(i - (N_BIAS_TILES - 2), 0) // (2 * MOBA_TILE_BLOCKS)
    n_pairs = n_tiles // 2
    odd = n_tiles % 2

    def tile_pair(u, carry, with_bias):
        t0 = 2 * u
        scores(t0 + 1, s1_ref, qx)
        carry = update(t0, s0_ref, p0_ref, p1_ref, carry, with_bias)
        if with_bias:
            is_last = jnp.logical_and(u == n_pairs - 1, odd == 0)
            scores(jnp.where(is_last, 0, t0 + 2), s0_ref, jnp.where(is_last, qx_next, qx))
        else:
            scores(t0 + 2, s0_ref, qx)
        return update(t0 + 1, s1_ref, p1_ref, p0_ref, carry, with_bias)

    def odd_tile(_, carry):
        carry = update(n_tiles - 1, s0_ref, p0_ref, p1_ref, carry, True)
        scores(0, s0_ref, qx_next)
        return carry

    @pl.when(step == 0)
    def _():
        scores(0, s0_ref, qx)

    p1_ref[...] = jnp.zeros(p1_ref.shape, p1_ref.dtype)
    carry = (jnp.full((1, width), NEG, F32), jnp.zeros((1, width), F32),
             jnp.zeros((HEAD_DIM, width), F32), jnp.ones((1, width), F32))
    carry = lax.fori_loop(0, n_far_pairs, functools.partial(tile_pair, with_bias=False), carry)
    carry = lax.fori_loop(n_far_pairs, n_pairs, functools.partial(tile_pair, with_bias=True), carry)
    _, l, acc, alpha = lax.fori_loop(0, odd, odd_tile, carry)
    p_last = jnp.where(odd == 1, p0_ref[...], p1_ref[...])
    acc = alpha * acc + jnp.dot(vt_ref[n_tiles - 1], p_last, preferred_element_type=F32)
    o_ref[...] = (acc / l).T.astype(o_ref.dtype)


def _moba_attention(qt, pen, kx, vt, bias, w_cast, cast_layer):
    s = kx.shape[0]
    nb = s // MOBA_BLOCK
    tile = MOBA_TILE_BLOCKS * MOBA_BLOCK
    assert nb % (2 * MOBA_TILE_BLOCKS) == 0 and nb <= LANES and vt.shape == (s // tile, D_MODEL, tile)
    kv_blk = s * HEAD_DIM * 2
    width = MOBA_QUERY_BLOCKS * MOBA_BLOCK
    n_steps = nb // MOBA_QUERY_BLOCKS
    assert nb % MOBA_QUERY_BLOCKS == 0
    w_in, w_out, w_shape = _cast_rider_specs(w_cast, cast_layer, n_steps)
    n_cast_blocks = w_cast.shape[1] // CAST_ROWS
    assert n_cast_blocks <= N_HEADS * n_steps
    vmem = 2 * (3 * kv_blk + N_BIAS_TILES * MOBA_BLOCK * MOBA_BLOCK * 4) + 12 * tile * width \
        + 2 * CAST_ROWS * w_cast.shape[2] * 6 + 24 * 2**20
    cur = lambda h, i: (h, i)
    nxt = lambda h, i: (h, jnp.minimum(i + 1, n_steps - 1))
    return pl.pallas_call(
        functools.partial(_moba_kernel, n_cast_blocks=n_cast_blocks),
        out_shape=[jax.ShapeDtypeStruct((s, D_MODEL), BF16), w_shape],
        grid=(N_HEADS, n_steps),
        in_specs=[pl.BlockSpec((HEAD_DIM, width), cur),
                  pl.BlockSpec((LANES, width), cur),
                  pl.BlockSpec((HEAD_DIM, width), nxt),
                  pl.BlockSpec((LANES, width), nxt),
                  pl.BlockSpec((s, 2 * HEAD_DIM), lambda h, i: (0, h)),
                  pl.BlockSpec((s // tile, HEAD_DIM, tile), lambda h, i: (0, h, 0)),
                  pl.BlockSpec((None, N_BIAS_TILES, MOBA_BLOCK, MOBA_BLOCK),
                               lambda h, i: (h, 0, 0, 0)),
                  w_in],
        out_specs=[pl.BlockSpec((width, HEAD_DIM), lambda h, i: (i, h)), w_out],
        scratch_shapes=[pltpu.VMEM((tile, width), F32)] * 2
        + [pltpu.VMEM((tile, width), BF16)] * 2,
        compiler_params=_params(("arbitrary", "arbitrary"), vmem),
        name="moba_attention",
    )(qt, pen, qt, pen, kx, vt, bias, w_cast)


def _mixer_out_and_ffn(o, w_o, x, ffn_gain, w_gate_up, layer, w_down_bf16, next_gains=None):
    x, (h,), rstd = _mm_residual(o, w_o, 0, x, tm=RES_TM, tn=RES_TN, gains=ffn_gain[None, :])
    act = _mm_swiglu(h, w_gate_up, layer, rstd)
    return _mm_residual(act, w_down_bf16[None], 0, x, tm=512, tn=512, gains=next_gains)


def kernel(x, rel_bias, attn_norm, ffn_norm, w_qkv_a, q_norm_a, k_norm_a, w_o_a, kv_norm, w_kv_b, k_norm_b, w_q_b, q_norm_b, w_o_b, w_gate_up, w_down):
    assert x.shape[0] == 1 and x.shape[2] == D_MODEL
    xs = x[0].astype(F32)
    d = D_MODEL

    table = rel_bias.astype(F32) * LOG2E
    bias_dil = _bias_expand(_dilated_bucket_tiles(), table)
    bias_moba = _bias_expand(_moba_bucket_tiles(), table - table[N_BUCKETS - 1:, :])

    (h,) = _rmsnorm(xs, attn_norm[0][None, :])
    gain_a = jnp.concatenate(
        [jnp.tile(q_norm_a[0][g].astype(F32) * (SCALE * LOG2E), N_HEADS) for g in range(N_GROUPS)]
        + [jnp.tile(k_norm_a[0].astype(F32), N_HEADS)])[None, :]
    qk, v = _mm_headnorm(h, w_qkv_a, 0, gain_a, (N_GROUPS + 1) * d, F32)
    o, w_down_0 = _dilated_attention(qk, v, bias_dil, w_down, 0)
    xs, (hk, hq), rstd = _mixer_out_and_ffn(o, w_o_a, xs, ffn_norm[0], w_gate_up, 0, w_down_0,
                                           next_gains=jnp.stack([kv_norm, attn_norm[1]]))

    gain_k = jnp.tile(k_norm_b.astype(F32), N_HEADS)[None, :]
    k, kmean, vt = _mm_shared_kv(hk, w_kv_b, gain_k, rstd)
    gain_q = jnp.tile(q_norm_b[0].astype(F32) * (SCALE * LOG2E), N_HEADS)[None, :]
    qt, pen = _mm_query_gate(hq, w_q_b, 0, gain_q, kmean, rstd)
    o, w_down_1 = _moba_attention(qt, pen, k, vt, bias_moba, w_down, 1)
    xs = _mixer_out_and_ffn(o, w_o_b, xs, ffn_norm[1], w_gate_up, 1, w_down_1)
    return xs[None].astype(x.dtype)
```

```python
import functools
import math

import numpy as np
import jax
import jax.numpy as jnp
from jax import lax
from jax.experimental import pallas as pl
from jax.experimental.pallas import tpu as pltpu

D_MODEL = 4096
HEAD_DIM = 128
N_HEADS = D_MODEL // HEAD_DIM
DILATED_PATTERNS = ((128, 1), (512, 4), (2048, 16))
N_GROUPS = len(DILATED_PATTERNS)
BAND = 128
MOBA_BLOCK = 256
MOBA_TOPK = 3
N_BUCKETS = 32
MAX_DISTANCE = 2048
D_FF = 11008
EPS = 1e-6
NEG = -1e30
SCALE = HEAD_DIM ** -0.5
LOG2E = math.log2(math.e)

UNIT = 2048
BLOCKS_PER_UNIT = UNIT // BAND
DIL_UNROLL = 16
N_BIAS_TILES = 8
MOBA_TILE_BLOCKS = 2
MOBA_QUERY_BLOCKS = 4

V7X_SCOPED_VMEM_BYTES = 60000 * 1024
LANES = 128

F32 = jnp.float32
BF16 = jnp.bfloat16


def _params(semantics, vmem_bytes):
    return pltpu.CompilerParams(dimension_semantics=semantics,
                                vmem_limit_bytes=min(int(vmem_bytes), V7X_SCOPED_VMEM_BYTES))


def _rel_bucket_np(dist):
    n = np.maximum(dist, 0)
    exact = N_BUCKETS // 2
    nf = np.maximum(n, exact).astype(np.float64)
    large = exact + (np.log(nf / exact) / math.log(MAX_DISTANCE / exact)
                     * (N_BUCKETS - exact)).astype(np.int64)
    return np.where(n < exact, n, np.minimum(large, N_BUCKETS - 1)).astype(np.int32)


MASKED = N_BUCKETS


def _dilated_bucket_tiles():
    i = np.arange(BAND)[:, None]
    j = np.arange(2 * BAND)[None, :]
    tiles = []
    for window, d in DILATED_PATTERNS:
        w_sub = window // d
        per = []
        for first in (False, True):
            dist = (i - j) if first else (BAND + i - j)
            ok = (dist >= 0) & (dist <= w_sub)
            per.append(np.where(ok, _rel_bucket_np(dist * d), MASKED))
        tiles.append(np.stack(per))
    return np.stack(tiles).astype(np.int32)


def _moba_bucket_tiles():
    i = np.arange(MOBA_BLOCK)[None, :]
    j = np.arange(MOBA_BLOCK)[:, None]
    tiles = []
    for delta in range(N_BIAS_TILES - 1):
        dist = delta * MOBA_BLOCK + i - j
        b = _rel_bucket_np(dist)
        tiles.append(np.where(dist >= 0, b, MASKED) if delta == 0 else b)
    far = (N_BIAS_TILES - 2) * MOBA_BLOCK + 1
    assert _rel_bucket_np(np.array([far]))[0] == N_BUCKETS - 1
    tiles.append(np.full((MOBA_BLOCK, MOBA_BLOCK), N_BUCKETS - 1))
    return np.stack(tiles).astype(np.int32)


TABLE_ROWS = 64


def _bias_expand_kernel(idx_ref, tab_ref, o_ref):
    idx = idx_ref[...]
    rows = lax.broadcasted_iota(jnp.int32, (TABLE_ROWS, idx.shape[1]), 0)
    onehot = (rows == idx).astype(BF16)
    tab = tab_ref[...]
    hi = tab.astype(BF16)
    rest = tab - hi.astype(F32)
    mid = rest.astype(BF16)
    lo = (rest - mid.astype(F32)).astype(BF16)
    hi, mid, lo = (jnp.dot(piece, onehot, preferred_element_type=F32) for piece in (hi, mid, lo))
    o_ref[...] = (hi + mid) + lo


def _bias_expand(idx_np, table):
    idx_flat = jnp.asarray(idx_np.ravel())
    n = idx_flat.shape[0]
    tn = 8192
    assert n % tn == 0
    tab = jnp.zeros((N_HEADS, TABLE_ROWS), F32).at[:, :N_BUCKETS].set(table.astype(F32).T)
    tab = tab.at[:, MASKED].set(NEG)
    return pl.pallas_call(
        _bias_expand_kernel,
        out_shape=jax.ShapeDtypeStruct((N_HEADS, n), F32),
        grid=(n // tn,),
        in_specs=[pl.BlockSpec((1, tn), lambda i: (0, i)),
                  pl.BlockSpec((N_HEADS, TABLE_ROWS), lambda i: (0, 0))],
        out_specs=pl.BlockSpec((N_HEADS, tn), lambda i: (0, i)),
        compiler_params=_params(("parallel",), 16 * 2**20),
        name="bias_expand",
    )(idx_flat.reshape(1, n), tab).reshape((N_HEADS,) + idx_np.shape)


def _rmsnorm_kernel(x_ref, g_ref, *o_refs):
    x = x_ref[...]
    y = x * lax.rsqrt(jnp.mean(x * x, axis=-1, keepdims=True) + EPS)
    for n, o_ref in enumerate(o_refs):
        o_ref[...] = (y * g_ref[n:n + 1, :]).astype(o_ref.dtype)


def _rmsnorm(x, gains):
    s, d = x.shape
    n_out = gains.shape[0]
    tr = 512
    outs = pl.pallas_call(
        _rmsnorm_kernel,
        out_shape=[jax.ShapeDtypeStruct((s, d), BF16)] * n_out,
        grid=(s // tr,),
        in_specs=[pl.BlockSpec((tr, d), lambda i: (i, 0)),
                  pl.BlockSpec((n_out, d), lambda i: (0, 0))],
        out_specs=[pl.BlockSpec((tr, d), lambda i: (i, 0))] * n_out,
        compiler_params=_params(("parallel",), 2 * tr * d * (4 + 2 * n_out) + 8 * 2**20),
        name="rmsnorm",
    )(x, gains.astype(F32))
    return outs


def _head_rms_cols(acc, gain_ref, c):
    t = acc[:, c * HEAD_DIM:(c + 1) * HEAD_DIM]
    t = t * lax.rsqrt(jnp.mean(t * t, axis=-1, keepdims=True) + EPS)
    return t * gain_ref[:, c * HEAD_DIM:(c + 1) * HEAD_DIM]


def _wmatmul_kernel(*refs, epilogue, n_w, n_extra, n_out, cast, scaled):
    a_ref = refs[0]
    w_refs = refs[1:1 + n_w]
    extra_refs = refs[1 + n_w:1 + n_w + n_extra]
    out_refs = refs[1 + n_w + n_extra:1 + n_w + n_extra + n_out]
    if scaled:
        scale_ref, *extra_refs = extra_refs
    if cast:
        wb_refs = refs[1 + n_w + n_extra + n_out:]

        @pl.when(pl.program_id(1) == 0)
        def _():
            for w_ref, wb_ref in zip(w_refs, wb_refs):
                wb_ref[...] = w_ref[...].astype(BF16)
    else:
        wb_refs = w_refs
    half = a_ref.shape[0] // MM_ROW_SPLIT
    for h in range(MM_ROW_SPLIT):
        rows = slice(h * half, (h + 1) * half)
        a = a_ref[rows, :]
        accs = [jnp.dot(a, wb_ref[...], preferred_element_type=F32) for wb_ref in wb_refs]
        if scaled:
            scale = jnp.tile(scale_ref[rows, :], (1, accs[0].shape[1] // LANES))
            accs = [acc * scale for acc in accs]
        epilogue(accs, extra_refs, out_refs, rows)


def _wmatmul(epilogue, a, weights, extra, outs, *, n_cols, tm, tn, name, row_scale=None):
    m, k = a.shape
    assert m % tm == 0 and n_cols % tn == 0
    cast = weights[0][0].dtype != BF16
    if row_scale is not None:
        extra = [(row_scale, pl.BlockSpec((tm, LANES), lambda j, i: (i, 0)))] + list(extra)
    w_specs = [pl.BlockSpec((None, k, tn), lambda j, i, layer=layer, off=off: (layer, 0, j + off))
               for _, layer, off in weights]
    wsz = weights[0][0].dtype.itemsize
    blocks = tm * k * 2 + len(weights) * k * tn * wsz
    for arr, spec in list(extra) + [(o, s) for o, s in outs]:
        blocks += math.prod(d for d in spec.block_shape if d is not None) * jnp.dtype(arr.dtype).itemsize
    vmem = 2 * blocks + (len(weights) * k * tn * 2 if cast else 0) + (2 + 2 * len(weights)) * tm * tn * 4
    return pl.pallas_call(
        functools.partial(_wmatmul_kernel, epilogue=epilogue, n_w=len(weights), n_extra=len(extra),
                          n_out=len(outs), cast=cast, scaled=row_scale is not None),
        out_shape=[o for o, _ in outs],
        grid=(n_cols // tn, m // tm),
        in_specs=[pl.BlockSpec((tm, k), lambda j, i: (i, 0))] + w_specs + [s for _, s in extra],
        out_specs=[s for _, s in outs],
        scratch_shapes=[pltpu.VMEM((k, tn), BF16)] * len(weights) if cast else [],
        compiler_params=_params(("parallel", "arbitrary"), vmem),
        name=name,
    )(a, *[w for w, _, _ in weights], *[arr for arr, _ in extra])


PROJ_TM, PROJ_TN = 512, 1024
RES_TM, RES_TN = 1024, 512
MM_ROW_SPLIT = 2


def _gain_spec(tn):
    return pl.BlockSpec((1, tn), lambda j, i: (0, j))


def _headnorm_epilogue(accs, extra_refs, out_refs, rows):
    (acc,), (gain_ref,), (o_ref,) = accs, extra_refs, out_refs
    for c in range(acc.shape[1] // HEAD_DIM):
        t = _head_rms_cols(acc, gain_ref, c)
        o_ref[rows, c * HEAD_DIM:(c + 1) * HEAD_DIM] = t.astype(o_ref.dtype)


def _store_epilogue(accs, extra_refs, out_refs, rows):
    out_refs[0][rows, :] = accs[0].astype(out_refs[0].dtype)


def _mm_headnorm(a, w, layer, gain_row, n_norm_cols, out_dtype):
    m = a.shape[0]
    n = w.shape[2]
    tm, tn = PROJ_TM, PROJ_TN
    assert n_norm_cols % tn == 0
    tile = pl.BlockSpec((tm, tn), lambda j, i: (i, j))
    (normed,) = _wmatmul(
        _headnorm_epilogue, a, [(w, layer, 0)], [(gain_row, _gain_spec(tn))],
        [(jax.ShapeDtypeStruct((m, n_norm_cols), out_dtype), tile)],
        n_cols=n_norm_cols, tm=tm, tn=tn, name="mm_headnorm")
    (rest,) = _wmatmul(
        _store_epilogue, a, [(w, layer, n_norm_cols // tn)], [],
        [(jax.ShapeDtypeStruct((m, n - n_norm_cols), out_dtype), tile)],
        n_cols=n - n_norm_cols, tm=tm, tn=tn, name="mm_plain")
    return normed, rest


def _block_gate(gate, own):
    blk = lax.broadcasted_iota(jnp.int32, gate.shape, 0)
    blk_f = blk.astype(F32)
    past = blk < own
    gm = jnp.where(past, gate, NEG)
    picked = jnp.zeros(gate.shape, jnp.bool_)
    for _ in range(MOBA_TOPK):
        mx = jnp.max(gm, axis=0, keepdims=True)
        first = jnp.min(jnp.where(gm == mx, blk_f, float(LANES)), axis=0, keepdims=True)
        pick = blk_f == first
        picked = jnp.logical_or(picked, pick)
        gm = jnp.where(pick, -jnp.inf, gm)
    allowed = jnp.logical_or(jnp.logical_and(picked, past), blk == own)
    return jnp.where(allowed, 0.0, NEG)


def _query_gate_epilogue(accs, extra_refs, out_refs, rows):
    (acc,), (gain_ref, kmean_ref), (qt_ref, pen_ref) = accs, extra_refs, out_refs
    n_rows = acc.shape[0]
    nb = kmean_ref.shape[0]
    query = (pl.program_id(1) * qt_ref.shape[1] + rows.start
             + lax.broadcasted_iota(jnp.int32, (nb, n_rows), 1))
    own = query // MOBA_BLOCK
    for c in range(acc.shape[1] // HEAD_DIM):
        cols = slice(c * HEAD_DIM, (c + 1) * HEAD_DIM)
        qt = _head_rms_cols(acc, gain_ref, c).T.astype(qt_ref.dtype)
        qt_ref[cols, rows] = qt
        gate = jnp.dot(kmean_ref[:, cols].astype(BF16), qt, preferred_element_type=F32)
        pen_ref[c * LANES:c * LANES + nb, rows] = _block_gate(gate, own).astype(pen_ref.dtype)
        pen_ref[c * LANES + nb:(c + 1) * LANES, rows] = jnp.zeros((LANES - nb, n_rows), pen_ref.dtype)


def _mm_query_gate(a, w, layer, gain_row, kmean, row_scale):
    m = a.shape[0]
    n = w.shape[2]
    nb = kmean.shape[0]
    tm, tn = PROJ_TM, PROJ_TN
    assert nb <= LANES and LANES == HEAD_DIM
    transposed = pl.BlockSpec((tn, tm), lambda j, i: (j, i))
    return _wmatmul(
        _query_gate_epilogue, a, [(w, layer, 0)],
        [(gain_row, _gain_spec(tn)), (kmean, pl.BlockSpec((nb, tn), lambda j, i: (0, j)))],
        [(jax.ShapeDtypeStruct((n, m), BF16), transposed),
         (jax.ShapeDtypeStruct((n, m), BF16), transposed)],
        n_cols=n, tm=tm, tn=tn, name="mm_query_gate", row_scale=row_scale)


def _key_epilogue(accs, extra_refs, out_refs, rows):
    (acc,), (gain_ref,), (kx_ref, mean_ref) = accs, extra_refs, out_refs
    n_rows, tn = acc.shape
    assert rows.start % MOBA_BLOCK == 0 and n_rows % MOBA_BLOCK == 0
    row = (pl.program_id(1) * kx_ref.shape[0] + rows.start
           + lax.broadcasted_iota(jnp.int32, (n_rows, LANES), 0))
    onehot = (row // MOBA_BLOCK == lax.broadcasted_iota(jnp.int32, (n_rows, LANES), 1)
              ).astype(kx_ref.dtype)
    for c in range(tn // HEAD_DIM):
        cols = slice(c * HEAD_DIM, (c + 1) * HEAD_DIM)
        t = _head_rms_cols(acc, gain_ref, c)
        kx_ref[rows, 2 * c * HEAD_DIM:(2 * c + 1) * HEAD_DIM] = t.astype(kx_ref.dtype)
        kx_ref[rows, (2 * c + 1) * HEAD_DIM:(2 * c + 2) * HEAD_DIM] = onehot
        for r in range(n_rows // MOBA_BLOCK):
            blk = t[r * MOBA_BLOCK:(r + 1) * MOBA_BLOCK, :]
            r_out = rows.start // MOBA_BLOCK + r
            mean_ref[r_out:r_out + 1, cols] = jnp.mean(blk, axis=0, keepdims=True)


def _transpose_epilogue(accs, extra_refs, out_refs, rows):
    out_refs[0][:, rows] = accs[0].T.astype(out_refs[0].dtype)


def _mm_shared_kv(a, w_kv, gain_row, row_scale):
    m = a.shape[0]
    d = w_kv.shape[1] // 2
    w = w_kv[None]
    tm, tn = PROJ_TM, PROJ_TN
    rows = tm // MOBA_BLOCK
    assert m // MOBA_BLOCK <= LANES
    kn, kmean = _wmatmul(
        _key_epilogue, a, [(w, 0, 0)], [(gain_row, _gain_spec(tn))],
        [(jax.ShapeDtypeStruct((m, 2 * d), BF16), pl.BlockSpec((tm, 2 * tn), lambda j, i: (i, j))),
         (jax.ShapeDtypeStruct((m // tm, rows, d), F32),
          pl.BlockSpec((None, rows, tn), lambda j, i: (i, 0, j)))],
        n_cols=d, tm=tm, tn=tn, name="mm_keys", row_scale=row_scale)
    tile = MOBA_TILE_BLOCKS * MOBA_BLOCK
    per_tile = tile // tm
    assert tile % tm == 0
    (vt,) = _wmatmul(
        _transpose_epilogue, a, [(w, 0, d // tn)], [],
        [(jax.ShapeDtypeStruct((m // tile, d, tile), BF16),
          pl.BlockSpec((None, tn, tm), lambda j, i: (i // per_tile, j, i % per_tile)))],
        n_cols=d, tm=tm, tn=tn, name="mm_values_t", row_scale=row_scale)
    return kn, kmean.reshape(m // MOBA_BLOCK, d), vt


def _residual_epilogue(accs, extra_refs, out_refs, rows):
    res_ref, *gain_refs = extra_refs
    x_ref, *norm_refs = out_refs
    x = res_ref[rows, :] + accs[0]
    x_ref[rows, :] = x
    if gain_refs:
        (gain_ref,) = gain_refs
        *scaled_refs, ssq_ref = norm_refs
        for n, scaled_ref in enumerate(scaled_refs):
            scaled_ref[rows, :] = (x * gain_ref[n:n + 1, :]).astype(scaled_ref.dtype)
        ssq = jnp.sum(x * x, axis=1, keepdims=True)
        ssq_ref[rows, :] = jnp.broadcast_to(ssq, (x.shape[0], LANES))


def _rstd_kernel(ssq_ref, o_ref, *, width):
    ssq = ssq_ref[...]
    total = ssq[:, 0:LANES]
    for c in range(1, ssq.shape[1] // LANES):
        total = total + ssq[:, c * LANES:(c + 1) * LANES]
    o_ref[...] = lax.rsqrt(total / width + EPS)


def _mm_residual(a, w, layer, res, tm, tn, gains=None):
    m = a.shape[0]
    n = w.shape[2]
    tile = pl.BlockSpec((tm, tn), lambda j, i: (i, j))
    extra = [(res, tile)]
    outs = [(jax.ShapeDtypeStruct((m, n), F32), tile)]
    if gains is not None:
        n_gains = gains.shape[0]
        extra.append((gains.astype(F32), pl.BlockSpec((n_gains, tn), lambda j, i: (0, j))))
        outs += [(jax.ShapeDtypeStruct((m, n), BF16), tile)] * n_gains
        outs.append((jax.ShapeDtypeStruct((m, n // tn * LANES), F32),
                     pl.BlockSpec((tm, LANES), lambda j, i: (i, j))))
    results = _wmatmul(_residual_epilogue, a, [(w, layer, 0)], extra, outs,
                       n_cols=n, tm=tm, tn=tn, name="mm_residual")
    if gains is None:
        return results[0]
    x, *scaled, ssq = results
    tr = 1024
    rstd = pl.pallas_call(
        functools.partial(_rstd_kernel, width=n),
        out_shape=jax.ShapeDtypeStruct((m, LANES), F32),
        grid=(m // tr,),
        in_specs=[pl.BlockSpec((tr, ssq.shape[1]), lambda i: (i, 0))],
        out_specs=pl.BlockSpec((tr, LANES), lambda i: (i, 0)),
        compiler_params=_params(("parallel",), 32 * 2**20),
        name="rstd",
    )(ssq)
    return x, scaled, rstd


def _swiglu_epilogue(accs, extra_refs, out_refs, rows):
    g, u = accs
    out_refs[0][rows, :] = (g * (1.0 / (1.0 + jnp.exp(-g))) * u).astype(out_refs[0].dtype)


def _mm_swiglu(a, w_gate_up, layer, row_scale):
    m = a.shape[0]
    tm, tn = 1024, 256
    (out,) = _wmatmul(
        _swiglu_epilogue, a, [(w_gate_up, layer, 0), (w_gate_up, layer, D_FF // tn)], [],
        [(jax.ShapeDtypeStruct((m, D_FF), BF16), pl.BlockSpec((tm, tn), lambda j, i: (i, j)))],
        n_cols=D_FF, tm=tm, tn=tn, name="mm_swiglu", row_scale=row_scale)
    return out


CAST_ROWS = 256


def _cast_rider_specs(w, layer, n_inner):
    _, k, n = w.shape
    assert k % CAST_ROWS == 0
    last = k // CAST_ROWS - 1

    def block(a, b):
        return jnp.minimum(a * n_inner + b, last)

    return (pl.BlockSpec((None, CAST_ROWS, n), lambda a, b: (layer, block(a, b), 0)),
            pl.BlockSpec((CAST_ROWS, n), lambda a, b: (block(a, b), 0)),
            jax.ShapeDtypeStruct((k, n), BF16))


def _cast_rider(w_ref, wb_ref, n_blocks):
    step = pl.program_id(0) * pl.num_programs(1) + pl.program_id(1)

    @pl.when(step < n_blocks)
    def _():
        wb_ref[...] = w_ref[...].astype(wb_ref.dtype)


def _dilated_kernel(q0_ref, q1_ref, q2_ref, k_ref, v_ref, bias_ref, w_ref, o_ref, wb_ref, *scratch,
                    n_cast_blocks):
    og, mg, lg = scratch[0:3], scratch[3:6], scratch[6:9]
    u = pl.program_id(1)
    q_refs = (q0_ref, q1_ref, q2_ref)
    _cast_rider(w_ref, wb_ref, n_cast_blocks)

    for g, (_, d) in enumerate(DILATED_PATTERNS):
        shift = d.bit_length() - 1

        def block(b, g=g, d=d, shift=shift):
            r = jnp.bitwise_and(b, d - 1)
            nl = jnp.right_shift(b, shift)
            rel = r + (BAND * d) * nl
            rows = pl.ds(rel, BAND, stride=d) if d > 1 else pl.ds(rel, BAND)
            q = q_refs[g][rows, :].astype(BF16)
            first = jnp.logical_and(u == 0, nl == 0)
            ks = jnp.where(first, r, u * UNIT + rel - BAND * d)
            krows = pl.ds(ks, 2 * BAND, stride=d) if d > 1 else pl.ds(ks, 2 * BAND)
            k = k_ref[krows, :].astype(BF16)
            v = v_ref[krows, :].astype(BF16)
            s = lax.dot_general(q, k, (((1,), (1,)), ((), ())), preferred_element_type=F32)
            s = s + bias_ref[g, first.astype(jnp.int32)]
            m = jnp.max(s, axis=1, keepdims=True)
            p = jnp.exp2(s - m)
            l = jnp.sum(p, axis=1, keepdims=True)
            o = jnp.dot(p.astype(BF16), v, preferred_element_type=F32)
            og[g][rows, :] = o
            mg[g][rows, :] = jnp.broadcast_to(m, (BAND, HEAD_DIM))
            lg[g][rows, :] = jnp.broadcast_to(l, (BAND, HEAD_DIM))

        def blocks(it, carry, block=block):
            for c in range(DIL_UNROLL):
                block(it * DIL_UNROLL + c)
            return carry

        lax.fori_loop(0, BLOCKS_PER_UNIT // DIL_UNROLL, blocks, 0)

    chunk = 256

    def merge(c, carry):
        rows = pl.ds(pl.multiple_of(c * chunk, chunk), chunk)
        ms = [mg[g][rows, :] for g in range(N_GROUPS)]
        top = jnp.maximum(jnp.maximum(ms[0], ms[1]), ms[2])
        num = jnp.zeros((chunk, HEAD_DIM), F32)
        den = jnp.zeros((chunk, HEAD_DIM), F32)
        for g in range(N_GROUPS):
            w = jnp.exp2(ms[g] - top)
            num = num + w * og[g][rows, :]
            den = den + w * lg[g][rows, :]
        o_ref[rows, :] = (num / den).astype(o_ref.dtype)
        return carry

    lax.fori_loop(0, UNIT // chunk, merge, 0)


def _dilated_attention(qk, v, bias, w_cast, cast_layer):
    s = qk.shape[0]
    assert s % UNIT == 0
    n_units = s // UNIT
    w_in, w_out, w_shape = _cast_rider_specs(w_cast, cast_layer, n_units)
    n_cast_blocks = w_cast.shape[1] // CAST_ROWS
    assert n_cast_blocks <= N_HEADS * n_units
    hcols = N_HEADS
    q_specs = [pl.BlockSpec((UNIT, HEAD_DIM), lambda h, u, g=g: (u, g * hcols + h))
               for g in range(N_GROUPS)]
    k_spec = pl.BlockSpec((s, HEAD_DIM), lambda h, u: (0, N_GROUPS * hcols + h))
    v_spec = pl.BlockSpec((s, HEAD_DIM), lambda h, u: (0, h))
    b_spec = pl.BlockSpec((None, N_GROUPS, 2, BAND, 2 * BAND), lambda h, u: (h, 0, 0, 0, 0))
    blk = UNIT * HEAD_DIM * 4
    vmem = 2 * (3 * blk + 2 * s * HEAD_DIM * 4 + N_GROUPS * 2 * BAND * 2 * BAND * 4 + blk // 2) \
        + 9 * blk + 2 * CAST_ROWS * w_cast.shape[2] * 6 + 8 * 2**20
    return pl.pallas_call(
        functools.partial(_dilated_kernel, n_cast_blocks=n_cast_blocks),
        out_shape=[jax.ShapeDtypeStruct((s, D_MODEL), BF16), w_shape],
        grid=(N_HEADS, n_units),
        in_specs=q_specs + [k_spec, v_spec, b_spec, w_in],
        out_specs=[pl.BlockSpec((UNIT, HEAD_DIM), lambda h, u: (u, h)), w_out],
        scratch_shapes=[pltpu.VMEM((UNIT, HEAD_DIM), F32)] * 9,
        compiler_params=_params(("arbitrary", "arbitrary"), vmem),
        name="dilated_attention",
    )(qk, qk, qk, qk, v, bias, w_cast)


def _moba_kernel(qt_ref, pen_ref, qt_next_ref, pen_next_ref, kx_ref, vt_ref, bias_ref, w_ref,
                 o_ref, wb_ref, s0_ref, s1_ref, p0_ref, p1_ref, *, n_cast_blocks):
    step = pl.program_id(1)
    i = step * MOBA_QUERY_BLOCKS
    i_last = i + MOBA_QUERY_BLOCKS - 1
    width = MOBA_QUERY_BLOCKS * MOBA_BLOCK
    tile = MOBA_TILE_BLOCKS * MOBA_BLOCK
    last_tile = vt_ref.shape[0] - 1
    _cast_rider(w_ref, wb_ref, n_cast_blocks)
    qx = jnp.concatenate([qt_ref[...], pen_ref[...]], axis=0)
    qx_next = jnp.concatenate([qt_next_ref[...], pen_next_ref[...]], axis=0)

    def scores(t, s_ref, qx):
        t = jnp.minimum(t, last_tile)
        rows = pl.ds(pl.multiple_of(t * tile, tile), tile)
        s_ref[...] = jnp.dot(kx_ref[rows, :], qx, preferred_element_type=F32)

    def value_dot(t, p_ref):
        return jnp.dot(vt_ref[jnp.clip(t, 0, last_tile)], p_ref[...], preferred_element_type=F32)

    def update(t, s_ref, p_ref, p_prev_ref, carry, with_bias):
        m, l, acc, alpha_prev = carry
        pv = value_dot(t - 1, p_prev_ref)
        if with_bias:
            def bias_tile(c, a):
                return bias_ref[jnp.clip(i + a - (t * MOBA_TILE_BLOCKS + c), 0, N_BIAS_TILES - 1)]
            s_ref[...] += jnp.concatenate(
                [jnp.concatenate([bias_tile(c, a) for a in range(MOBA_QUERY_BLOCKS)], axis=1)
                 for c in range(MOBA_TILE_BLOCKS)], axis=0)
        m_new = jnp.maximum(m, jnp.max(s_ref[...], axis=0, keepdims=True))
        alpha = jnp.exp2(m - m_new)
        p = jnp.exp2(s_ref[...] - m_new)
        l = alpha * l + jnp.sum(p, axis=0, keepdims=True)
        p_ref[...] = p.astype(BF16)
        return m_new, l, alpha_prev * acc + pv, alpha

    n_tiles = (i_last + MOBA_TILE_BLOCKS) // MOBA_TILE_BLOCKS
    n_far_pairs = jnp.maximum(i - (N_BIAS_TILES - 2), 0) // (2 * MOBA_TILE_BLOCKS)
    n_pairs = n_tiles // 2
    odd = n_tiles % 2

    def tile_pair(u, carry, with_bias):
        t0 = 2 * u
        scores(t0 + 1, s1_ref, qx)
        carry = update(t0, s0_ref, p0_ref, p1_ref, carry, with_bias)
        if with_bias:
            is_last = jnp.logical_and(u == n_pairs - 1, odd == 0)
            scores(jnp.where(is_last, 0, t0 + 2), s0_ref, jnp.where(is_last, qx_next, qx))
        else:
            scores(t0 + 2, s0_ref, qx)
        return update(t0 + 1, s1_ref, p1_ref, p0_ref, carry, with_bias)

    def odd_tile(_, carry):
        carry = update(n_tiles - 1, s0_ref, p0_ref, p1_ref, carry, True)
        scores(0, s0_ref, qx_next)
        return carry

    @pl.when(step == 0)
    def _():
        scores(0, s0_ref, qx)

    p1_ref[...] = jnp.zeros(p1_ref.shape, p1_ref.dtype)
    carry = (jnp.full((1, width), NEG, F32), jnp.zeros((1, width), F32),
             jnp.zeros((HEAD_DIM, width), F32), jnp.ones((1, width), F32))
    carry = lax.fori_loop(0, n_far_pairs, functools.partial(tile_pair, with_bias=False), carry)
    carry = lax.fori_loop(n_far_pairs, n_pairs, functools.partial(tile_pair, with_bias=True), carry)
    _, l, acc, alpha = lax.fori_loop(0, odd, odd_tile, carry)
    p_last = jnp.where(odd == 1, p0_ref[...], p1_ref[...])
    acc = alpha * acc + jnp.dot(vt_ref[n_tiles - 1], p_last, preferred_element_type=F32)
    o_ref[...] = (acc / l).T.astype(o_ref.dtype)


def _moba_attention(qt, pen, kx, vt, bias, w_cast, cast_layer):
    s = kx.shape[0]
    nb = s // MOBA_BLOCK
    tile = MOBA_TILE_BLOCKS * MOBA_BLOCK
    assert nb % (2 * MOBA_TILE_BLOCKS) == 0 and nb <= LANES and vt.shape == (s // tile, D_MODEL, tile)
    kv_blk = s * HEAD_DIM * 2
    width = MOBA_QUERY_BLOCKS * MOBA_BLOCK
    n_steps = nb // MOBA_QUERY_BLOCKS
    assert nb % MOBA_QUERY_BLOCKS == 0
    w_in, w_out, w_shape = _cast_rider_specs(w_cast, cast_layer, n_steps)
    n_cast_blocks = w_cast.shape[1] // CAST_ROWS
    assert n_cast_blocks <= N_HEADS * n_steps
    vmem = 2 * (3 * kv_blk + N_BIAS_TILES * MOBA_BLOCK * MOBA_BLOCK * 4) + 12 * tile * width \
        + 2 * CAST_ROWS * w_cast.shape[2] * 6 + 24 * 2**20
    cur = lambda h, i: (h, i)
    nxt = lambda h, i: (h, jnp.minimum(i + 1, n_steps - 1))
    return pl.pallas_call(
        functools.partial(_moba_kernel, n_cast_blocks=n_cast_blocks),
        out_shape=[jax.ShapeDtypeStruct((s, D_MODEL), BF16), w_shape],
        grid=(N_HEADS, n_steps),
        in_specs=[pl.BlockSpec((HEAD_DIM, width), cur),
                  pl.BlockSpec((LANES, width), cur),
                  pl.BlockSpec((HEAD_DIM, width), nxt),
                  pl.BlockSpec((LANES, width), nxt),
                  pl.BlockSpec((s, 2 * HEAD_DIM), lambda h, i: (0, h)),
                  pl.BlockSpec((s // tile, HEAD_DIM, tile), lambda h, i: (0, h, 0)),
                  pl.BlockSpec((None, N_BIAS_TILES, MOBA_BLOCK, MOBA_BLOCK),
                               lambda h, i: (h, 0, 0, 0)),
                  w_in],
        out_specs=[pl.BlockSpec((width, HEAD_DIM), lambda h, i: (i, h)), w_out],
        scratch_shapes=[pltpu.VMEM((tile, width), F32)] * 2
        + [pltpu.VMEM((tile, width), BF16)] * 2,
        compiler_params=_params(("arbitrary", "arbitrary"), vmem),
        name="moba_attention",
    )(qt, pen, qt, pen, kx, vt, bias, w_cast)


def _mixer_out_and_ffn(o, w_o, x, ffn_gain, w_gate_up, layer, w_down_bf16, next_gains=None):
    x, (h,), rstd = _mm_residual(o, w_o, 0, x, tm=RES_TM, tn=RES_TN, gains=ffn_gain[None, :])
    act = _mm_swiglu(h, w_gate_up, layer, rstd)
    return _mm_residual(act, w_down_bf16[None], 0, x, tm=512, tn=512, gains=next_gains)


def kernel(x, rel_bias, attn_norm, ffn_norm, w_qkv_a, q_norm_a, k_norm_a, w_o_a, kv_norm, w_kv_b, k_norm_b, w_q_b, q_norm_b, w_o_b, w_gate_up, w_down):
    assert x.shape[0] == 1 and x.shape[2] == D_MODEL
    xs = x[0].astype(F32)
    d = D_MODEL

    table = rel_bias.astype(F32) * LOG2E
    bias_dil = _bias_expand(_dilated_bucket_tiles(), table)
    bias_moba = _bias_expand(_moba_bucket_tiles(), table - table[N_BUCKETS - 1:, :])

    (h,) = _rmsnorm(xs, attn_norm[0][None, :])
    gain_a = jnp.concatenate(
        [jnp.tile(q_norm_a[0][g].astype(F32) * (SCALE * LOG2E), N_HEADS) for g in range(N_GROUPS)]
        + [jnp.tile(k_norm_a[0].astype(F32), N_HEADS)])[None, :]
    qk, v = _mm_headnorm(h, w_qkv_a, 0, gain_a, (N_GROUPS + 1) * d, F32)
    o, w_down_0 = _dilated_attention(qk, v, bias_dil, w_down, 0)
    xs, (hk, hq), rstd = _mixer_out_and_ffn(o, w_o_a, xs, ffn_norm[0], w_gate_up, 0, w_down_0,
                                           next_gains=jnp.stack([kv_norm, attn_norm[1]]))

    gain_k = jnp.tile(k_norm_b.astype(F32), N_HEADS)[None, :]
    k, kmean, vt = _mm_shared_kv(hk, w_kv_b, gain_k, rstd)
    gain_q = jnp.tile(q_norm_b[0].astype(F32) * (SCALE * LOG2E), N_HEADS)[None, :]
    qt, pen = _mm_query_gate(hq, w_q_b, 0, gain_q, kmean, rstd)
    o, w_down_1 = _moba_attention(qt, pen, k, vt, bias_moba, w_down, 1)
    xs = _mixer_out_and_ffn(o, w_o_b, xs, ffn_norm[1], w_gate_up, 1, w_down_1)
    return xs[None].astype(x.dtype)
```

```python
import functools
import math

import numpy as np
import jax
import jax.numpy as jnp
from jax import lax
from jax.experimental import pallas as pl
from jax.experimental.pallas import tpu as pltpu

D_MODEL = 4096
HEAD_DIM = 128
N_HEADS = D_MODEL // HEAD_DIM
DILATED_PATTERNS = ((128, 1), (512, 4), (2048, 16))
N_GROUPS = len(DILATED_PATTERNS)
BAND = 128
MOBA_BLOCK = 256
MOBA_TOPK = 3
N_BUCKETS = 32
MAX_DISTANCE = 2048
D_FF = 11008
EPS = 1e-6
NEG = -1e30
SCALE = HEAD_DIM ** -0.5
LOG2E = math.log2(math.e)

UNIT = 2048
BLOCKS_PER_UNIT = UNIT // BAND
DIL_UNROLL = 16
N_BIAS_TILES = 8
MOBA_TILE_BLOCKS = 4
MOBA_QUERY_BLOCKS = 4

V7X_SCOPED_VMEM_BYTES = 60000 * 1024
LANES = 128

F32 = jnp.float32
BF16 = jnp.bfloat16


def _params(semantics, vmem_bytes):
    return pltpu.CompilerParams(dimension_semantics=semantics,
                                vmem_limit_bytes=min(int(vmem_bytes), V7X_SCOPED_VMEM_BYTES))


def _rel_bucket_np(dist):
    n = np.maximum(dist, 0)
    exact = N_BUCKETS // 2
    nf = np.maximum(n, exact).astype(np.float64)
    large = exact + (np.log(nf / exact) / math.log(MAX_DISTANCE / exact)
                     * (N_BUCKETS - exact)).astype(np.int64)
    return np.where(n < exact, n, np.minimum(large, N_BUCKETS - 1)).astype(np.int32)


MASKED = N_BUCKETS


def _dilated_bucket_tiles():
    i = np.arange(BAND)[:, None]
    j = np.arange(2 * BAND)[None, :]
    tiles = []
    for window, d in DILATED_PATTERNS:
        w_sub = window // d
        per = []
        for first in (False, True):
            dist = (i - j) if first else (BAND + i - j)
            ok = (dist >= 0) & (dist <= w_sub)
            per.append(np.where(ok, _rel_bucket_np(dist * d), MASKED))
        tiles.append(np.stack(per))
    return np.stack(tiles).astype(np.int32)


def _moba_bucket_tiles():
    i = np.arange(MOBA_BLOCK)[None, :]
    j = np.arange(MOBA_BLOCK)[:, None]
    tiles = []
    for delta in range(N_BIAS_TILES - 1):
        dist = delta * MOBA_BLOCK + i - j
        b = _rel_bucket_np(dist)
        tiles.append(np.where(dist >= 0, b, MASKED) if delta == 0 else b)
    far = (N_BIAS_TILES - 2) * MOBA_BLOCK + 1
    assert _rel_bucket_np(np.array([far]))[0] == N_BUCKETS - 1
    tiles.append(np.full((MOBA_BLOCK, MOBA_BLOCK), N_BUCKETS - 1))
    return np.stack(tiles).astype(np.int32)


TABLE_ROWS = 64


def _bias_expand_kernel(idx_ref, tab_ref, o_ref):
    idx = idx_ref[...]
    rows = lax.broadcasted_iota(jnp.int32, (TABLE_ROWS, idx.shape[1]), 0)
    onehot = (rows == idx).astype(BF16)
    tab = tab_ref[...]
    hi = tab.astype(BF16)
    rest = tab - hi.astype(F32)
    mid = rest.astype(BF16)
    lo = (rest - mid.astype(F32)).astype(BF16)
    hi, mid, lo = (jnp.dot(piece, onehot, preferred_element_type=F32) for piece in (hi, mid, lo))
    o_ref[...] = (hi + mid) + lo


def _bias_expand(idx_np, table):
    idx_flat = jnp.asarray(idx_np.ravel())
    n = idx_flat.shape[0]
    tn = 8192
    assert n % tn == 0
    tab = jnp.zeros((N_HEADS, TABLE_ROWS), F32).at[:, :N_BUCKETS].set(table.astype(F32).T)
    tab = tab.at[:, MASKED].set(NEG)
    return pl.pallas_call(
        _bias_expand_kernel,
        out_shape=jax.ShapeDtypeStruct((N_HEADS, n), F32),
        grid=(n // tn,),
        in_specs=[pl.BlockSpec((1, tn), lambda i: (0, i)),
                  pl.BlockSpec((N_HEADS, TABLE_ROWS), lambda i: (0, 0))],
        out_specs=pl.BlockSpec((N_HEADS, tn), lambda i: (0, i)),
        compiler_params=_params(("parallel",), 16 * 2**20),
        name="bias_expand",
    )(idx_flat.reshape(1, n), tab).reshape((N_HEADS,) + idx_np.shape)


def _rmsnorm_kernel(x_ref, g_ref, *o_refs):
    x = x_ref[...]
    y = x * lax.rsqrt(jnp.mean(x * x, axis=-1, keepdims=True) + EPS)
    for n, o_ref in enumerate(o_refs):
        o_ref[...] = (y * g_ref[n:n + 1, :]).astype(o_ref.dtype)


def _rmsnorm(x, gains):
    s, d = x.shape
    n_out = gains.shape[0]
    tr = 512
    outs = pl.pallas_call(
        _rmsnorm_kernel,
        out_shape=[jax.ShapeDtypeStruct((s, d), BF16)] * n_out,
        grid=(s // tr,),
        in_specs=[pl.BlockSpec((tr, d), lambda i: (i, 0)),
                  pl.BlockSpec((n_out, d), lambda i: (0, 0))],
        out_specs=[pl.BlockSpec((tr, d), lambda i: (i, 0))] * n_out,
        compiler_params=_params(("parallel",), 2 * tr * d * (4 + 2 * n_out) + 8 * 2**20),
        name="rmsnorm",
    )(x, gains.astype(F32))
    return outs


def _head_rms_cols(acc, gain_ref, c):
    t = acc[:, c * HEAD_DIM:(c + 1) * HEAD_DIM]
    t = t * lax.rsqrt(jnp.mean(t * t, axis=-1, keepdims=True) + EPS)
    return t * gain_ref[:, c * HEAD_DIM:(c + 1) * HEAD_DIM]


def _wmatmul_kernel(*refs, epilogue, n_w, n_extra, n_out, cast, scaled):
    a_ref = refs[0]
    w_refs = refs[1:1 + n_w]
    extra_refs = refs[1 + n_w:1 + n_w + n_extra]
    out_refs = refs[1 + n_w + n_extra:1 + n_w + n_extra + n_out]
    if scaled:
        scale_ref, *extra_refs = extra_refs
    if cast:
        wb_refs = refs[1 + n_w + n_extra + n_out:]

        @pl.when(pl.program_id(1) == 0)
        def _():
            for w_ref, wb_ref in zip(w_refs, wb_refs):
                wb_ref[...] = w_ref[...].astype(BF16)
    else:
        wb_refs = w_refs
    half = a_ref.shape[0] // MM_ROW_SPLIT
    for h in range(MM_ROW_SPLIT):
        rows = slice(h * half, (h + 1) * half)
        a = a_ref[rows, :]
        accs = [jnp.dot(a, wb_ref[...], preferred_element_type=F32) for wb_ref in wb_refs]
        if scaled:
            scale = jnp.tile(scale_ref[rows, :], (1, accs[0].shape[1] // LANES))
            accs = [acc * scale for acc in accs]
        epilogue(accs, extra_refs, out_refs, rows)


def _wmatmul(epilogue, a, weights, extra, outs, *, n_cols, tm, tn, name, row_scale=None):
    m, k = a.shape
    assert m % tm == 0 and n_cols % tn == 0
    cast = weights[0][0].dtype != BF16
    if row_scale is not None:
        extra = [(row_scale, pl.BlockSpec((tm, LANES), lambda j, i: (i, 0)))] + list(extra)
    w_specs = [pl.BlockSpec((None, k, tn), lambda j, i, layer=layer, off=off: (layer, 0, j + off))
               for _, layer, off in weights]
    wsz = weights[0][0].dtype.itemsize
    blocks = tm * k * 2 + len(weights) * k * tn * wsz
    for arr, spec in list(extra) + [(o, s) for o, s in outs]:
        blocks += math.prod(d for d in spec.block_shape if d is not None) * jnp.dtype(arr.dtype).itemsize
    vmem = 2 * blocks + (len(weights) * k * tn * 2 if cast else 0) + (2 + 2 * len(weights)) * tm * tn * 4
    return pl.pallas_call(
        functools.partial(_wmatmul_kernel, epilogue=epilogue, n_w=len(weights), n_extra=len(extra),
                          n_out=len(outs), cast=cast, scaled=row_scale is not None),
        out_shape=[o for o, _ in outs],
        grid=(n_cols // tn, m // tm),
        in_specs=[pl.BlockSpec((tm, k), lambda j, i: (i, 0))] + w_specs + [s for _, s in extra],
        out_specs=[s for _, s in outs],
        scratch_shapes=[pltpu.VMEM((k, tn), BF16)] * len(weights) if cast else [],
        compiler_params=_params(("parallel", "arbitrary"), vmem),
        name=name,
    )(a, *[w for w, _, _ in weights], *[arr for arr, _ in extra])


PROJ_TM, PROJ_TN = 512, 1024
RES_TM, RES_TN = 1024, 512
MM_ROW_SPLIT = 2


def _gain_spec(tn):
    return pl.BlockSpec((1, tn), lambda j, i: (0, j))


def _headnorm_epilogue(accs, extra_refs, out_refs, rows):
    (acc,), (gain_ref,), (o_ref,) = accs, extra_refs, out_refs
    for c in range(acc.shape[1] // HEAD_DIM):
        t = _head_rms_cols(acc, gain_ref, c)
        o_ref[rows, c * HEAD_DIM:(c + 1) * HEAD_DIM] = t.astype(o_ref.dtype)


def _store_epilogue(accs, extra_refs, out_refs, rows):
    out_refs[0][rows, :] = accs[0].astype(out_refs[0].dtype)


def _mm_headnorm(a, w, layer, gain_row, n_norm_cols, out_dtype):
    m = a.shape[0]
    n = w.shape[2]
    tm, tn = PROJ_TM, PROJ_TN
    assert n_norm_cols % tn == 0
    tile = pl.BlockSpec((tm, tn), lambda j, i: (i, j))
    (normed,) = _wmatmul(
        _headnorm_epilogue, a, [(w, layer, 0)], [(gain_row, _gain_spec(tn))],
        [(jax.ShapeDtypeStruct((m, n_norm_cols), out_dtype), tile)],
        n_cols=n_norm_cols, tm=tm, tn=tn, name="mm_headnorm")
    (rest,) = _wmatmul(
        _store_epilogue, a, [(w, layer, n_norm_cols // tn)], [],
        [(jax.ShapeDtypeStruct((m, n - n_norm_cols), out_dtype), tile)],
        n_cols=n - n_norm_cols, tm=tm, tn=tn, name="mm_plain")
    return normed, rest


def _block_gate(gate, own):
    blk = lax.broadcasted_iota(jnp.int32, gate.shape, 0)
    blk_f = blk.astype(F32)
    past = blk < own
    gm = jnp.where(past, gate, NEG)
    picked = jnp.zeros(gate.shape, jnp.bool_)
    for _ in range(MOBA_TOPK):
        mx = jnp.max(gm, axis=0, keepdims=True)
        first = jnp.min(jnp.where(gm == mx, blk_f, float(LANES)), axis=0, keepdims=True)
        pick = blk_f == first
        picked = jnp.logical_or(picked, pick)
        gm = jnp.where(pick, -jnp.inf, gm)
    allowed = jnp.logical_or(jnp.logical_and(picked, past), blk == own)
    return jnp.where(allowed, 0.0, NEG)


def _query_gate_epilogue(accs, extra_refs, out_refs, rows):
    (acc,), (gain_ref, kmean_ref), (qt_ref, pen_ref) = accs, extra_refs, out_refs
    n_rows = acc.shape[0]
    nb = kmean_ref.shape[0]
    query = (pl.program_id(1) * qt_ref.shape[1] + rows.start
             + lax.broadcasted_iota(jnp.int32, (nb, n_rows), 1))
    own = query // MOBA_BLOCK
    for c in range(acc.shape[1] // HEAD_DIM):
        cols = slice(c * HEAD_DIM, (c + 1) * HEAD_DIM)
        qt = _head_rms_cols(acc, gain_ref, c).T.astype(qt_ref.dtype)
        qt_ref[cols, rows] = qt
        gate = jnp.dot(kmean_ref[:, cols].astype(BF16), qt, preferred_element_type=F32)
        pen_ref[c * LANES:c * LANES + nb, rows] = _block_gate(gate, own).astype(pen_ref.dtype)
        pen_ref[c * LANES + nb:(c + 1) * LANES, rows] = jnp.zeros((LANES - nb, n_rows), pen_ref.dtype)


def _mm_query_gate(a, w, layer, gain_row, kmean, row_scale):
    m = a.shape[0]
    n = w.shape[2]
    nb = kmean.shape[0]
    tm, tn = PROJ_TM, PROJ_TN
    assert nb <= LANES and LANES == HEAD_DIM
    transposed = pl.BlockSpec((tn, tm), lambda j, i: (j, i))
    return _wmatmul(
        _query_gate_epilogue, a, [(w, layer, 0)],
        [(gain_row, _gain_spec(tn)), (kmean, pl.BlockSpec((nb, tn), lambda j, i: (0, j)))],
        [(jax.ShapeDtypeStruct((n, m), BF16), transposed),
         (jax.ShapeDtypeStruct((n, m), BF16), transposed)],
        n_cols=n, tm=tm, tn=tn, name="mm_query_gate", row_scale=row_scale)


def _key_epilogue(accs, extra_refs, out_refs, rows):
    (acc,), (gain_ref,), (kx_ref, mean_ref) = accs, extra_refs, out_refs
    n_rows, tn = acc.shape
    assert rows.start % MOBA_BLOCK == 0 and n_rows % MOBA_BLOCK == 0
    row = (pl.program_id(1) * kx_ref.shape[0] + rows.start
           + lax.broadcasted_iota(jnp.int32, (n_rows, LANES), 0))
    onehot = (row // MOBA_BLOCK == lax.broadcasted_iota(jnp.int32, (n_rows, LANES), 1)
              ).astype(kx_ref.dtype)
    for c in range(tn // HEAD_DIM):
        cols = slice(c * HEAD_DIM, (c + 1) * HEAD_DIM)
        t = _head_rms_cols(acc, gain_ref, c)
        kx_ref[rows, 2 * c * HEAD_DIM:(2 * c + 1) * HEAD_DIM] = t.astype(kx_ref.dtype)
        kx_ref[rows, (2 * c + 1) * HEAD_DIM:(2 * c + 2) * HEAD_DIM] = onehot
        for r in range(n_rows // MOBA_BLOCK):
            blk = t[r * MOBA_BLOCK:(r + 1) * MOBA_BLOCK, :]
            r_out = rows.start // MOBA_BLOCK + r
            mean_ref[r_out:r_out + 1, cols] = jnp.mean(blk, axis=0, keepdims=True)


def _transpose_epilogue(accs, extra_refs, out_refs, rows):
    out_refs[0][:, rows] = accs[0].T.astype(out_refs[0].dtype)


def _mm_shared_kv(a, w_kv, gain_row, row_scale):
    m = a.shape[0]
    d = w_kv.shape[1] // 2
    w = w_kv[None]
    tm, tn = PROJ_TM, PROJ_TN
    rows = tm // MOBA_BLOCK
    assert m // MOBA_BLOCK <= LANES
    kn, kmean = _wmatmul(
        _key_epilogue, a, [(w, 0, 0)], [(gain_row, _gain_spec(tn))],
        [(jax.ShapeDtypeStruct((m, 2 * d), BF16), pl.BlockSpec((tm, 2 * tn), lambda j, i: (i, j))),
         (jax.ShapeDtypeStruct((m // tm, rows, d), F32),
          pl.BlockSpec((None, rows, tn), lambda j, i: (i, 0, j)))],
        n_cols=d, tm=tm, tn=tn, name="mm_keys", row_scale=row_scale)
    tile = MOBA_TILE_BLOCKS * MOBA_BLOCK
    per_tile = tile // tm
    assert tile % tm == 0
    (vt,) = _wmatmul(
        _transpose_epilogue, a, [(w, 0, d // tn)], [],
        [(jax.ShapeDtypeStruct((m // tile, d, tile), BF16),
          pl.BlockSpec((None, tn, tm), lambda j, i: (i // per_tile, j, i % per_tile)))],
        n_cols=d, tm=tm, tn=tn, name="mm_values_t", row_scale=row_scale)
    return kn, kmean.reshape(m // MOBA_BLOCK, d), vt


def _residual_epilogue(accs, extra_refs, out_refs, rows):
    res_ref, *gain_refs = extra_refs
    x_ref, *norm_refs = out_refs
    x = res_ref[rows, :] + accs[0]
    x_ref[rows, :] = x
    if gain_refs:
        (gain_ref,) = gain_refs
        *scaled_refs, ssq_ref = norm_refs
        for n, scaled_ref in enumerate(scaled_refs):
            scaled_ref[rows, :] = (x * gain_ref[n:n + 1, :]).astype(scaled_ref.dtype)
        ssq = jnp.sum(x * x, axis=1, keepdims=True)
        ssq_ref[rows, :] = jnp.broadcast_to(ssq, (x.shape[0], LANES))


def _rstd_kernel(ssq_ref, o_ref, *, width):
    ssq = ssq_ref[...]
    total = ssq[:, 0:LANES]
    for c in range(1, ssq.shape[1] // LANES):
        total = total + ssq[:, c * LANES:(c + 1) * LANES]
    o_ref[...] = lax.rsqrt(total / width + EPS)


def _mm_residual(a, w, layer, res, tm, tn, gains=None):
    m = a.shape[0]
    n = w.shape[2]
    tile = pl.BlockSpec((tm, tn), lambda j, i: (i, j))
    extra = [(res, tile)]
    outs = [(jax.ShapeDtypeStruct((m, n), F32), tile)]
    if gains is not None:
        n_gains = gains.shape[0]
        extra.append((gains.astype(F32), pl.BlockSpec((n_gains, tn), lambda j, i: (0, j))))
        outs += [(jax.ShapeDtypeStruct((m, n), BF16), tile)] * n_gains
        outs.append((jax.ShapeDtypeStruct((m, n // tn * LANES), F32),
                     pl.BlockSpec((tm, LANES), lambda j, i: (i, j))))
    results = _wmatmul(_residual_epilogue, a, [(w, layer, 0)], extra, outs,
                       n_cols=n, tm=tm, tn=tn, name="mm_residual")
    if gains is None:
        return results[0]
    x, *scaled, ssq = results
    tr = 1024
    rstd = pl.pallas_call(
        functools.partial(_rstd_kernel, width=n),
        out_shape=jax.ShapeDtypeStruct((m, LANES), F32),
        grid=(m // tr,),
        in_specs=[pl.BlockSpec((tr, ssq.shape[1]), lambda i: (i, 0))],
        out_specs=pl.BlockSpec((tr, LANES), lambda i: (i, 0)),
        compiler_params=_params(("parallel",), 32 * 2**20),
        name="rstd",
    )(ssq)
    return x, scaled, rstd


def _swiglu_epilogue(accs, extra_refs, out_refs, rows):
    g, u = accs
    out_refs[0][rows, :] = (g * (1.0 / (1.0 + jnp.exp(-g))) * u).astype(out_refs[0].dtype)


def _mm_swiglu(a, w_gate_up, layer, row_scale):
    m = a.shape[0]
    tm, tn = 1024, 256
    (out,) = _wmatmul(
        _swiglu_epilogue, a, [(w_gate_up, layer, 0), (w_gate_up, layer, D_FF // tn)], [],
        [(jax.ShapeDtypeStruct((m, D_FF), BF16), pl.BlockSpec((tm, tn), lambda j, i: (i, j)))],
        n_cols=D_FF, tm=tm, tn=tn, name="mm_swiglu", row_scale=row_scale)
    return out


CAST_ROWS = 256


def _cast_rider_specs(w, layer, n_inner):
    _, k, n = w.shape
    assert k % CAST_ROWS == 0
    last = k // CAST_ROWS - 1

    def block(a, b):
        return jnp.minimum(a * n_inner + b, last)

    return (pl.BlockSpec((None, CAST_ROWS, n), lambda a, b: (layer, block(a, b), 0)),
            pl.BlockSpec((CAST_ROWS, n), lambda a, b: (block(a, b), 0)),
            jax.ShapeDtypeStruct((k, n), BF16))


def _cast_rider(w_ref, wb_ref, n_blocks):
    step = pl.program_id(0) * pl.num_programs(1) + pl.program_id(1)

    @pl.when(step < n_blocks)
    def _():
        wb_ref[...] = w_ref[...].astype(wb_ref.dtype)


def _dilated_kernel(q0_ref, q1_ref, q2_ref, k_ref, v_ref, bias_ref, w_ref, o_ref, wb_ref, *scratch,
                    n_cast_blocks):
    og, mg, lg = scratch[0:3], scratch[3:6], scratch[6:9]
    u = pl.program_id(1)
    q_refs = (q0_ref, q1_ref, q2_ref)
    _cast_rider(w_ref, wb_ref, n_cast_blocks)

    for g, (_, d) in enumerate(DILATED_PATTERNS):
        shift = d.bit_length() - 1

        def block(b, g=g, d=d, shift=shift):
            r = jnp.bitwise_and(b, d - 1)
            nl = jnp.right_shift(b, shift)
            rel = r + (BAND * d) * nl
            rows = pl.ds(rel, BAND, stride=d) if d > 1 else pl.ds(rel, BAND)
            q = q_refs[g][rows, :].astype(BF16)
            first = jnp.logical_and(u == 0, nl == 0)
            ks = jnp.where(first, r, u * UNIT + rel - BAND * d)
            krows = pl.ds(ks, 2 * BAND, stride=d) if d > 1 else pl.ds(ks, 2 * BAND)
            k = k_ref[krows, :].astype(BF16)
            v = v_ref[krows, :].astype(BF16)
            s = lax.dot_general(q, k, (((1,), (1,)), ((), ())), preferred_element_type=F32)
            s = s + bias_ref[g, first.astype(jnp.int32)]
            m = jnp.max(s, axis=1, keepdims=True)
            p = jnp.exp2(s - m)
            l = jnp.sum(p, axis=1, keepdims=True)
            o = jnp.dot(p.astype(BF16), v, preferred_element_type=F32)
            og[g][rows, :] = o
            mg[g][rows, :] = jnp.broadcast_to(m, (BAND, HEAD_DIM))
            lg[g][rows, :] = jnp.broadcast_to(l, (BAND, HEAD_DIM))

        def blocks(it, carry, block=block):
            for c in range(DIL_UNROLL):
                block(it * DIL_UNROLL + c)
            return carry

        lax.fori_loop(0, BLOCKS_PER_UNIT // DIL_UNROLL, blocks, 0)

    chunk = 256

    def merge(c, carry):
        rows = pl.ds(pl.multiple_of(c * chunk, chunk), chunk)
        ms = [mg[g][rows, :] for g in range(N_GROUPS)]
        top = jnp.maximum(jnp.maximum(ms[0], ms[1]), ms[2])
        num = jnp.zeros((chunk, HEAD_DIM), F32)
        den = jnp.zeros((chunk, HEAD_DIM), F32)
        for g in range(N_GROUPS):
            w = jnp.exp2(ms[g] - top)
            num = num + w * og[g][rows, :]
            den = den + w * lg[g][rows, :]
        o_ref[rows, :] = (num / den).astype(o_ref.dtype)
        return carry

    lax.fori_loop(0, UNIT // chunk, merge, 0)


def _dilated_attention(qk, v, bias, w_cast, cast_layer):
    s = qk.shape[0]
    assert s % UNIT == 0
    n_units = s // UNIT
    w_in, w_out, w_shape = _cast_rider_specs(w_cast, cast_layer, n_units)
    n_cast_blocks = w_cast.shape[1] // CAST_ROWS
    assert n_cast_blocks <= N_HEADS * n_units
    hcols = N_HEADS
    q_specs = [pl.BlockSpec((UNIT, HEAD_DIM), lambda h, u, g=g: (u, g * hcols + h))
               for g in range(N_GROUPS)]
    k_spec = pl.BlockSpec((s, HEAD_DIM), lambda h, u: (0, N_GROUPS * hcols + h))
    v_spec = pl.BlockSpec((s, HEAD_DIM), lambda h, u: (0, h))
    b_spec = pl.BlockSpec((None, N_GROUPS, 2, BAND, 2 * BAND), lambda h, u: (h, 0, 0, 0, 0))
    blk = UNIT * HEAD_DIM * 4
    vmem = 2 * (3 * blk + 2 * s * HEAD_DIM * 4 + N_GROUPS * 2 * BAND * 2 * BAND * 4 + blk // 2) \
        + 9 * blk + 2 * CAST_ROWS * w_cast.shape[2] * 6 + 8 * 2**20
    return pl.pallas_call(
        functools.partial(_dilated_kernel, n_cast_blocks=n_cast_blocks),
        out_shape=[jax.ShapeDtypeStruct((s, D_MODEL), BF16), w_shape],
        grid=(N_HEADS, n_units),
        in_specs=q_specs + [k_spec, v_spec, b_spec, w_in],
        out_specs=[pl.BlockSpec((UNIT, HEAD_DIM), lambda h, u: (u, h)), w_out],
        scratch_shapes=[pltpu.VMEM((UNIT, HEAD_DIM), F32)] * 9,
        compiler_params=_params(("arbitrary", "arbitrary"), vmem),
        name="dilated_attention",
    )(qk, qk, qk, qk, v, bias, w_cast)


def _moba_kernel(qt_ref, pen_ref, qt_next_ref, pen_next_ref, kx_ref, vt_ref, bias_ref, w_ref,
                 o_ref, wb_ref, s0_ref, s1_ref, p0_ref, p1_ref, *, n_cast_blocks):
    step = pl.program_id(1)
    i = step * MOBA_QUERY_BLOCKS
    i_last = i + MOBA_QUERY_BLOCKS - 1
    width = MOBA_QUERY_BLOCKS * MOBA_BLOCK
    tile = MOBA_TILE_BLOCKS * MOBA_BLOCK
    last_tile = vt_ref.shape[0] - 1
    _cast_rider(w_ref, wb_ref, n_cast_blocks)
    qx = jnp.concatenate([qt_ref[...], pen_ref[...]], axis=0)
    qx_next = jnp.concatenate([qt_next_ref[...], pen_next_ref[...]], axis=0)

    def scores(t, s_ref, qx):
        t = jnp.minimum(t, last_tile)
        rows = pl.ds(pl.multiple_of(t * tile, tile), tile)
        s_ref[...] = jnp.dot(kx_ref[rows, :], qx, preferred_element_type=F32)

    def value_dot(t, p_ref):
        return jnp.dot(vt_ref[jnp.clip(t, 0, last_tile)], p_ref[...], preferred_element_type=F32)

    def update(t, s_ref, p_ref, p_prev_ref, carry, with_bias):
        m, l, acc, alpha_prev = carry
        pv = value_dot(t - 1, p_prev_ref)
        if with_bias:
            def bias_tile(c, a):
                return bias_ref[jnp.clip(i + a - (t * MOBA_TILE_BLOCKS + c), 0, N_BIAS_TILES - 1)]
            s_ref[...] += jnp.concatenate(
                [jnp.concatenate([bias_tile(c, a) for a in range(MOBA_QUERY_BLOCKS)], axis=1)
                 for c in range(MOBA_TILE_BLOCKS)], axis=0)
        m_new = jnp.maximum(m, jnp.max(s_ref[...], axis=0, keepdims=True))
        alpha = jnp.exp2(m - m_new)
        p = jnp.exp2(s_ref[...] - m_new)
        l = alpha * l + jnp.sum(p, axis=0, keepdims=True)
        p_ref[...] = p.astype(BF16)
        return m_new, l, alpha_prev * acc + pv, alpha

    n_tiles = (i_last + MOBA_TILE_BLOCKS) // MOBA_TILE_BLOCKS
    n_far_pairs = jnp.maximum(i - (N_BIAS_TILES - 2), 0) // (2 * MOBA_TILE_BLOCKS)
    n_pairs = n_tiles // 2
    odd = n_tiles % 2

    def tile_pair(u, carry, with_bias):
        t0 = 2 * u
        scores(t0 + 1, s1_ref, qx)
        carry = update(t0, s0_ref, p0_ref, p1_ref, carry, with_bias)
        if with_bias:
            is_last = jnp.logical_and(u == n_pairs - 1, odd == 0)
            scores(jnp.where(is_last, 0, t0 + 2), s0_ref, jnp.where(is_last, qx_next, qx))
        else:
            scores(t0 + 2, s0_ref, qx)
        return update(t0 + 1, s1_ref, p1_ref, p0_ref, carry, with_bias)

    def odd_tile(_, carry):
        carry = update(n_tiles - 1, s0_ref, p1_ref, p1_ref, carry, True)
        scores(0, s0_ref, qx_next)
        return carry

    @pl.when(step == 0)
    def _():
        scores(0, s0_ref, qx)

    p1_ref[...] = jnp.zeros(p1_ref.shape, p1_ref.dtype)
    carry = (jnp.full((1, width), NEG, F32), jnp.zeros((1, width), F32),
             jnp.zeros((HEAD_DIM, width), F32), jnp.ones((1, width), F32))
    carry = lax.fori_loop(0, n_far_pairs, functools.partial(tile_pair, with_bias=False), carry)
    carry = lax.fori_loop(n_far_pairs, n_pairs, functools.partial(tile_pair, with_bias=True), carry)
    _, l, acc, alpha = lax.fori_loop(0, odd, odd_tile, carry)
    acc = alpha * acc + value_dot(n_tiles - 1, p1_ref)
    o_ref[...] = (acc / l).T.astype(o_ref.dtype)


def _moba_attention(qt, pen, kx, vt, bias, w_cast, cast_layer):
    s = kx.shape[0]
    nb = s // MOBA_BLOCK
    tile = MOBA_TILE_BLOCKS * MOBA_BLOCK
    assert nb % (2 * MOBA_TILE_BLOCKS) == 0 and nb <= LANES and vt.shape == (s // tile, D_MODEL, tile)
    kv_blk = s * HEAD_DIM * 2
    width = MOBA_QUERY_BLOCKS * MOBA_BLOCK
    n_steps = nb // MOBA_QUERY_BLOCKS
    assert nb % MOBA_QUERY_BLOCKS == 0
    w_in, w_out, w_shape = _cast_rider_specs(w_cast, cast_layer, n_steps)
    n_cast_blocks = w_cast.shape[1] // CAST_ROWS
    assert n_cast_blocks <= N_HEADS * n_steps
    vmem = 2 * (3 * kv_blk + N_BIAS_TILES * MOBA_BLOCK * MOBA_BLOCK * 4) + 12 * tile * width \
        + 2 * CAST_ROWS * w_cast.shape[2] * 6 + 24 * 2**20
    cur = lambda h, i: (h, i)
    nxt = lambda h, i: (h, jnp.minimum(i + 1, n_steps - 1))
    return pl.pallas_call(
        functools.partial(_moba_kernel, n_cast_blocks=n_cast_blocks),
        out_shape=[jax.ShapeDtypeStruct((s, D_MODEL), BF16), w_shape],
        grid=(N_HEADS, n_steps),
        in_specs=[pl.BlockSpec((HEAD_DIM, width), cur),
                  pl.BlockSpec((LANES, width), cur),
                  pl.BlockSpec((HEAD_DIM, width), nxt),
                  pl.BlockSpec((LANES, width), nxt),
                  pl.BlockSpec((s, 2 * HEAD_DIM), lambda h, i: (0, h)),
                  pl.BlockSpec((s // tile, HEAD_DIM, tile), lambda h, i: (0, h, 0)),
                  pl.BlockSpec((None, N_BIAS_TILES, MOBA_BLOCK, MOBA_BLOCK),
                               lambda h, i: (h, 0, 0, 0)),
                  w_in],
        out_specs=[pl.BlockSpec((width, HEAD_DIM), lambda h, i: (i, h)), w_out],
        scratch_shapes=[pltpu.VMEM((tile, width), F32)] * 2
        + [pltpu.VMEM((tile, width), BF16)] * 2,
        compiler_params=_params(("arbitrary", "arbitrary"), vmem),
        name="moba_attention",
    )(qt, pen, qt, pen, kx, vt, bias, w_cast)


def _mixer_out_and_ffn(o, w_o, x, ffn_gain, w_gate_up, layer, w_down_bf16, next_gains=None):
    x, (h,), rstd = _mm_residual(o, w_o, 0, x, tm=RES_TM, tn=RES_TN, gains=ffn_gain[None, :])
    act = _mm_swiglu(h, w_gate_up, layer, rstd)
    return _mm_residual(act, w_down_bf16[None], 0, x, tm=512, tn=512, gains=next_gains)


def kernel(x, rel_bias, attn_norm, ffn_norm, w_qkv_a, q_norm_a, k_norm_a, w_o_a, kv_norm, w_kv_b, k_norm_b, w_q_b, q_norm_b, w_o_b, w_gate_up, w_down):
    assert x.shape[0] == 1 and x.shape[2] == D_MODEL
    xs = x[0].astype(F32)
    d = D_MODEL

    table = rel_bias.astype(F32) * LOG2E
    bias_dil = _bias_expand(_dilated_bucket_tiles(), table)
    bias_moba = _bias_expand(_moba_bucket_tiles(), table - table[N_BUCKETS - 1:, :])

    (h,) = _rmsnorm(xs, attn_norm[0][None, :])
    gain_a = jnp.concatenate(
        [jnp.tile(q_norm_a[0][g].astype(F32) * (SCALE * LOG2E), N_HEADS) for g in range(N_GROUPS)]
        + [jnp.tile(k_norm_a[0].astype(F32), N_HEADS)])[None, :]
    qk, v = _mm_headnorm(h, w_qkv_a, 0, gain_a, (N_GROUPS + 1) * d, F32)
    o, w_down_0 = _dilated_attention(qk, v, bias_dil, w_down, 0)
    xs, (hk, hq), rstd = _mixer_out_and_ffn(o, w_o_a, xs, ffn_norm[0], w_gate_up, 0, w_down_0,
                                           next_gains=jnp.stack([kv_norm, attn_norm[1]]))

    gain_k = jnp.tile(k_norm_b.astype(F32), N_HEADS)[None, :]
    k, kmean, vt = _mm_shared_kv(hk, w_kv_b, gain_k, rstd)
    gain_q = jnp.tile(q_norm_b[0].astype(F32) * (SCALE * LOG2E), N_HEADS)[None, :]
    qt, pen = _mm_query_gate(hq, w_q_b, 0, gain_q, kmean, rstd)
    o, w_down_1 = _moba_attention(qt, pen, k, vt, bias_moba, w_down, 1)
    xs = _mixer_out_and_ffn(o, w_o_b, xs, ffn_norm[1], w_gate_up, 1, w_down_1)
    return xs[None].astype(x.dtype)
```
